```python
import math
import jax, jax.numpy as jnp
from jax import lax
import numpy as np

D_MODEL = 1024
BATCH = 1
SEQ = 16384
DEPTH = 1
DEC_BATCH = 16
DEC_SEQ = 64
PAST_LEN = 2048

CHUNK = 64
N_MEM = 256
EPS = 1e-6
SWA_HEADS = 16
SWA_KV_HEADS = 2
SWA_HEAD_DIM = 64
SWA_GROUP = SWA_HEADS // SWA_KV_HEADS
WINDOW = 128
ROPE_THETA = 10000.0
GDN_HEADS = 8
GDN_DK = 128
GDN_DV = 128
CONV_W = 4
MEM_HEADS = 4
MEM_HEAD_DIM = 256
N_BRANCH = 3
D_FF = 4 * D_MODEL

SWA_Q = SWA_HEADS * SWA_HEAD_DIM
SWA_KV = SWA_KV_HEADS * SWA_HEAD_DIM
GDN_QK = GDN_HEADS * GDN_DK
GDN_V = GDN_HEADS * GDN_DV
CONV_CH = 2 * GDN_QK + GDN_V
MEM_Q = MEM_HEADS * MEM_HEAD_DIM
SPLIT_SIZES = (SWA_Q, SWA_KV, SWA_KV, GDN_QK, GDN_QK, GDN_V, GDN_V, GDN_HEADS, GDN_HEADS, MEM_Q, N_BRANCH * D_MODEL)
D_IN = sum(SPLIT_SIZES)

kernel_name = 'streaming_hybrid_swa_gdn_mem'


def split_cols(x, sizes):
    out = []
    start = 0
    for s in sizes:
        out.append(x[..., start:start + s])
        start += s
    return out


def rmsnorm(x, g):
    xf = x.astype(jnp.float32)
    y = xf * lax.rsqrt(jnp.mean(xf * xf, axis=-1, keepdims=True) + EPS)
    return (y * g.astype(jnp.float32)).astype(x.dtype)


def l2norm(x):
    return x * lax.rsqrt(jnp.sum(x * x, axis=-1, keepdims=True) + EPS)


def rope(x, pos):
    half = x.shape[-1] // 2
    inv = ROPE_THETA ** (-jnp.arange(half, dtype=jnp.float32) / half)
    ang = pos.astype(jnp.float32)[:, None] * inv[None, :]
    cos = jnp.cos(ang)[:, None, :]
    sin = jnp.sin(ang)[:, None, :]
    xf = x.astype(jnp.float32)
    x1, x2 = xf[..., :half], xf[..., half:]
    return jnp.concatenate([x1 * cos - x2 * sin, x2 * cos + x1 * sin], axis=-1).astype(x.dtype)


def sink_attention(qg, kb, vb, sinks, valid):
    B, C, Q = qg.shape[:3]
    s = jnp.einsum('bcqkgd,bcskd->bckgqs', qg, kb).astype(jnp.float32) * (SWA_HEAD_DIM ** -0.5)
    s = jnp.where(valid[None, :, None, None, None, :], s, -jnp.inf)
    sink = sinks.astype(jnp.float32).reshape(1, 1, SWA_KV_HEADS, SWA_GROUP, 1, 1)
    m = jnp.maximum(jnp.max(s, axis=-1, keepdims=True), sink)
    e = jnp.exp(s - m)
    p = e / (jnp.sum(e, axis=-1, keepdims=True) + jnp.exp(sink - m))
    o = jnp.einsum('bckgqs,bcskd->bcqkgd', p.astype(vb.dtype), vb)
    return o.reshape(B, C * Q, SWA_Q)


def swa_attention(q, k, v, k_hist, v_hist, hist_valid, sinks):
    B, T = q.shape[:2]
    qb = min(T, CHUNK)
    nc = T // qb
    k_all = jnp.concatenate([k_hist, k], axis=1)
    v_all = jnp.concatenate([v_hist, v], axis=1)
    band = jnp.arange(nc)[:, None] * qb + jnp.arange(WINDOW + qb)[None, :]
    valid_all = jnp.concatenate([jnp.full((WINDOW,), hist_valid, dtype=bool), jnp.ones((T,), dtype=bool)])
    valid = valid_all[band]
    kb = jnp.take(k_all, band, axis=1)
    vb = jnp.take(v_all, band, axis=1)
    qg = q.reshape(B, nc, qb, SWA_KV_HEADS, SWA_GROUP, SWA_HEAD_DIM)
    o = sink_attention(qg, kb, vb, sinks, valid)
    return o, k_all[:, -WINDOW:], v_all[:, -WINDOW:]


def short_conv(x, buf, w):
    T = x.shape[1]
    xp = jnp.concatenate([buf, x], axis=1)
    y = xp[:, 0:T] * w[0]
    for i in range(1, CONV_W):
        y = y + xp[:, i:i + T] * w[i]
    return jax.nn.silu(y), xp[:, -(CONV_W - 1):]


def gdn_chunked(q, k, v, log_a, beta, S0, chunk):
    B, T, H, DK = q.shape
    DV = v.shape[-1]
    nc = T // chunk

    def blocks(t):
        t = t.reshape((B, nc, chunk) + t.shape[2:])
        return jnp.moveaxis(t, 2, 3)

    qc, kc, vc, bc = blocks(q), blocks(k), blocks(v), blocks(beta)
    g = jnp.cumsum(blocks(log_a), axis=-1)
    i = jnp.arange(chunk)
    incl = i[:, None] >= i[None, :]
    strict = i[:, None] > i[None, :]
    decay = jnp.exp(jnp.where(incl, g[..., :, None] - g[..., None, :], -jnp.inf))
    kk = jnp.einsum('bnhid,bnhjd->bnhij', kc, kc)
    A = jnp.where(strict, bc[..., :, None] * kk * decay, 0.0) + jnp.eye(chunk, dtype=q.dtype)
    rhs = jnp.concatenate([vc * bc[..., None], kc * (bc * jnp.exp(g))[..., None]], axis=-1)
    sol = lax.linalg.triangular_solve(A, rhs, left_side=True, lower=True, unit_diagonal=True)
    u, w = sol[..., :DV], sol[..., DV:]
    qk = jnp.einsum('bnhid,bnhjd->bnhij', qc, kc) * decay
    q_dec = qc * jnp.exp(g)[..., None]
    k_dec = kc * jnp.exp(g[..., -1:] - g)[..., None]
    g_tot = jnp.exp(g[..., -1])

    def step(S, xs):
        u_n, w_n, qk_n, qd_n, kd_n, gt_n = xs
        v_new = u_n - jnp.einsum('bhid,bhde->bhie', w_n, S)
        o = jnp.einsum('bhid,bhde->bhie', qd_n, S) + jnp.einsum('bhij,bhje->bhie', qk_n, v_new)
        S = S * gt_n[..., None, None] + jnp.einsum('bhid,bhie->bhde', kd_n, v_new)
        return S, o

    xs = (jnp.moveaxis(u, 1, 0), jnp.moveaxis(w, 1, 0), jnp.moveaxis(qk, 1, 0),
          jnp.moveaxis(q_dec, 1, 0), jnp.moveaxis(k_dec, 1, 0), jnp.moveaxis(g_tot, 1, 0))
    S, o = lax.scan(step, S0, xs)
    o = jnp.moveaxis(jnp.moveaxis(o, 0, 1), 3, 2).reshape(B, T, H, DV)
    return o, S


def gated_delta_branch(q, k, v, z, a, b, conv_buf, S0, conv_w, a_log, dt_bias, norm_g):
    B, T, _ = q.shape
    qkv, conv_new = short_conv(jnp.concatenate([q, k, v], axis=-1), conv_buf, conv_w)
    qh, kh, vh = split_cols(qkv, (GDN_QK, GDN_QK, GDN_V))
    qh = l2norm(qh.reshape(B, T, GDN_HEADS, GDN_DK).astype(jnp.float32)) * (GDN_DK ** -0.5)
    kh = l2norm(kh.reshape(B, T, GDN_HEADS, GDN_DK).astype(jnp.float32))
    vh = vh.reshape(B, T, GDN_HEADS, GDN_DV).astype(jnp.float32)
    beta = jax.nn.sigmoid(b.astype(jnp.float32))
    log_a = -jnp.exp(a_log.astype(jnp.float32)) * jax.nn.softplus(a.astype(jnp.float32) + dt_bias.astype(jnp.float32))
    o, S = gdn_chunked(qh, kh, vh, log_a, beta, S0.astype(jnp.float32), min(T, CHUNK))
    o = rmsnorm(o, norm_g) * jax.nn.silu(z.reshape(B, T, GDN_HEADS, GDN_DV).astype(jnp.float32))
    return o.reshape(B, T, GDN_V).astype(q.dtype), S.astype(S0.dtype), conv_new


def memory_kv(mem, g_mem, w_mem_kv):
    B = mem.shape[0]
    k, v = split_cols(rmsnorm(mem, g_mem) @ w_mem_kv, (MEM_Q, MEM_Q))
    return (k.reshape(B, N_MEM, MEM_HEADS, MEM_HEAD_DIM), v.reshape(B, N_MEM, MEM_HEADS, MEM_HEAD_DIM))


def memory_attention(q, mem_k, mem_v):
    B, T = q.shape[:2]
    s = jnp.einsum('bqhd,bmhd->bhqm', q, mem_k).astype(jnp.float32) * (MEM_HEAD_DIM ** -0.5)
    p = jax.nn.softmax(s, axis=-1)
    o = jnp.einsum('bhqm,bmhd->bqhd', p.astype(mem_v.dtype), mem_v)
    return o.reshape(B, T, MEM_Q)


def hybrid_layer(x, pos, k_hist, v_hist, hist_valid, S0, conv_buf, mem_k, mem_v,
                 g_pre_mix, w_in, b_in, swa_sinks, conv_w, gdn_a_log, gdn_dt_bias, gdn_norm_g,
                 w_branch, w_out, g_post_mix, g_pre_ffn, w_up, w_down, g_post_ffn):
    B, T, _ = x.shape
    h = rmsnorm(x, g_pre_mix)
    proj = h @ w_in + b_in
    qa, ka, va, qb, kb, vb, zb, ab, bb, qc, gl = split_cols(proj, SPLIT_SIZES)
    qa = rope(qa.reshape(B, T, SWA_HEADS, SWA_HEAD_DIM), pos)
    ka = rope(ka.reshape(B, T, SWA_KV_HEADS, SWA_HEAD_DIM), pos)
    va = va.reshape(B, T, SWA_KV_HEADS, SWA_HEAD_DIM)
    o_a, k_new, v_new = swa_attention(qa, ka, va, k_hist, v_hist, hist_valid, swa_sinks)
    o_b, S_new, conv_new = gated_delta_branch(qb, kb, vb, zb, ab, bb, conv_buf, S0, conv_w,
                                              gdn_a_log, gdn_dt_bias, gdn_norm_g)
    o_c = memory_attention(qc.reshape(B, T, MEM_HEADS, MEM_HEAD_DIM), mem_k, mem_v)
    branches = jnp.stack([o_a, o_b, o_c], axis=2)
    br = jnp.einsum('btnc,ncd->btnd', branches, w_branch)
    gates = jax.nn.sigmoid(gl.reshape(B, T, N_BRANCH, D_MODEL).astype(jnp.float32)).astype(x.dtype)
    merged = jnp.sum(gates * br, axis=2)
    x = x + rmsnorm(merged @ w_out, g_post_mix)
    hf = rmsnorm(x, g_pre_ffn)
    f = jnp.square(jax.nn.relu(hf @ w_up)) @ w_down
    x = x + rmsnorm(f, g_post_ffn)
    return x, k_new, v_new, S_new, conv_new


def setup_inputs(seed: int = 0) -> dict:
    key = jax.random.key(seed)
    ks = jax.random.split(key, 32)
    f32 = jnp.float32

    def nrm(k, shape, scale):
        return jax.random.normal(k, shape, f32) * scale

    def gain(k, shape):
        return 1.0 + 0.05 * jax.random.normal(k, shape, f32)

    dt = jnp.exp(jax.random.uniform(ks[9], (DEPTH, GDN_HEADS), f32, math.log(1e-3), math.log(1e-1)))
    return {
        'x_prompt': nrm(ks[0], (BATCH, SEQ, D_MODEL), 1.0),
        'x_sample': nrm(ks[1], (DEC_BATCH, DEC_SEQ, D_MODEL), 1.0),
        'mem_prompt': nrm(ks[2], (BATCH, N_MEM, D_MODEL), 1.0),
        'cache_swa_k': nrm(ks[3], (DEPTH, DEC_BATCH, WINDOW, SWA_KV_HEADS, SWA_HEAD_DIM), 1.0),
        'cache_swa_v': nrm(ks[4], (DEPTH, DEC_BATCH, WINDOW, SWA_KV_HEADS, SWA_HEAD_DIM), 1.0),
        'state_gdn': nrm(ks[5], (DEPTH, DEC_BATCH, GDN_HEADS, GDN_DK, GDN_DV), 0.1),
        'state_conv': nrm(ks[6], (DEPTH, DEC_BATCH, CONV_W - 1, CONV_CH), 1.0),
        'cache_mem_k': nrm(ks[7], (DEPTH, DEC_BATCH, N_MEM, MEM_HEADS, MEM_HEAD_DIM), 1.0),
        'cache_mem_v': nrm(ks[8], (DEPTH, DEC_BATCH, N_MEM, MEM_HEADS, MEM_HEAD_DIM), 1.0),
        'g_pre_mix': gain(ks[10], (DEPTH, D_MODEL)),
        'w_in': nrm(ks[11], (DEPTH, D_MODEL, D_IN), D_MODEL ** -0.5),
        'b_in': nrm(ks[12], (DEPTH, D_IN), 0.02),
        'swa_sinks': nrm(ks[13], (DEPTH, SWA_HEADS), 1.0),
        'conv_w': nrm(ks[14], (DEPTH, CONV_W, CONV_CH), CONV_W ** -0.5),
        'gdn_a_log': jnp.log(jax.random.uniform(ks[15], (DEPTH, GDN_HEADS), f32, 1.0, 16.0)),
        'gdn_dt_bias': dt + jnp.log(-jnp.expm1(-dt)),
        'gdn_norm_g': gain(ks[16], (DEPTH, GDN_DV)),
        'g_mem': gain(ks[17], (DEPTH, D_MODEL)),
        'w_mem_kv': nrm(ks[18], (DEPTH, D_MODEL, 2 * MEM_Q), D_MODEL ** -0.5),
        'w_branch': nrm(ks[19], (DEPTH, N_BRANCH, D_MODEL, D_MODEL), D_MODEL ** -0.5),
        'w_out': nrm(ks[20], (DEPTH, D_MODEL, D_MODEL), D_MODEL ** -0.5),
        'g_post_mix': gain(ks[21], (DEPTH, D_MODEL)),
        'g_pre_ffn': gain(ks[22], (DEPTH, D_MODEL)),
        'w_up': nrm(ks[23], (DEPTH, D_MODEL, D_FF), D_MODEL ** -0.5),
        'w_down': nrm(ks[24], (DEPTH, D_FF, D_MODEL), D_FF ** -0.5),
        'g_post_ffn': gain(ks[25], (DEPTH, D_MODEL)),
    }


def reference(x_prompt, x_sample, mem_prompt, cache_swa_k, cache_swa_v, state_gdn, state_conv,
              cache_mem_k, cache_mem_v, g_pre_mix, w_in, b_in, swa_sinks, conv_w, gdn_a_log,
              gdn_dt_bias, gdn_norm_g, g_mem, w_mem_kv, w_branch, w_out, g_post_mix, g_pre_ffn,
              w_up, w_down, g_post_ffn):
    xp, xs = x_prompt, x_sample
    Bp, Tp = xp.shape[0], xp.shape[1]
    Ts = xs.shape[1]
    pos_p = jnp.arange(Tp, dtype=jnp.int32)
    pos_s = PAST_LEN + jnp.arange(Ts, dtype=jnp.int32)
    zero_kv = jnp.zeros((Bp, WINDOW, SWA_KV_HEADS, SWA_HEAD_DIM), xp.dtype)
    zero_S = jnp.zeros((Bp, GDN_HEADS, GDN_DK, GDN_DV), xp.dtype)
    zero_conv = jnp.zeros((Bp, CONV_W - 1, CONV_CH), xp.dtype)
    p_k, p_v, p_S, p_c, p_mk, p_mv = [], [], [], [], [], []
    s_k, s_v, s_S, s_c = [], [], [], []
    for l in range(DEPTH):
        mk, mv = memory_kv(mem_prompt, g_mem[l], w_mem_kv[l])
        xp, kp_, vp_, Sp_, cp_ = hybrid_layer(
            xp, pos_p, zero_kv, zero_kv, False, zero_S, zero_conv, mk, mv,
            g_pre_mix[l], w_in[l], b_in[l], swa_sinks[l], conv_w[l], gdn_a_log[l], gdn_dt_bias[l],
            gdn_norm_g[l], w_branch[l], w_out[l], g_post_mix[l], g_pre_ffn[l], w_up[l], w_down[l], g_post_ffn[l])
        xs, ks_, vs_, Ss_, cs_ = hybrid_layer(
            xs, pos_s, cache_swa_k[l], cache_swa_v[l], True, state_gdn[l], state_conv[l],
            cache_mem_k[l], cache_mem_v[l],
            g_pre_mix[l], w_in[l], b_in[l], swa_sinks[l], conv_w[l], gdn_a_log[l], gdn_dt_bias[l],
            gdn_norm_g[l], w_branch[l], w_out[l], g_post_mix[l], g_pre_ffn[l], w_up[l], w_down[l], g_post_ffn[l])
        p_k.append(kp_); p_v.append(vp_); p_S.append(Sp_); p_c.append(cp_); p_mk.append(mk); p_mv.append(mv)
        s_k.append(ks_); s_v.append(vs_); s_S.append(Ss_); s_c.append(cs_)
    return (xp, xs,
            jnp.stack(p_k), jnp.stack(p_v), jnp.stack(p_S), jnp.stack(p_c), jnp.stack(p_mk), jnp.stack(p_mv),
            jnp.stack(s_k), jnp.stack(s_v), jnp.stack(s_S), jnp.stack(s_c))
```

```python
import functools

import jax
import jax.numpy as jnp
from jax import lax
from jax.experimental import pallas as pl
from jax.experimental.pallas import tpu as pltpu

F32 = jnp.float32
BF16 = jnp.bfloat16

D_MODEL = 1024
PAST_LEN = 2048
CHUNK = 64
N_MEM = 256
EPS = 1e-6
SWA_HEADS = 16
SWA_KV_HEADS = 2
SWA_HEAD_DIM = 64
WINDOW = 128
ROPE_THETA = 10000.0
GDN_HEADS = 8
GDN_DK = 128
GDN_DV = 128
CONV_W = 4
MEM_HEADS = 4
MEM_HEAD_DIM = 256
N_BRANCH = 3
D_FF = 4 * D_MODEL

LANE = 128
SWA_Q = SWA_HEADS * SWA_HEAD_DIM
SWA_KV = SWA_KV_HEADS * SWA_HEAD_DIM
GDN_QK = GDN_HEADS * GDN_DK
GDN_V = GDN_HEADS * GDN_DV
MEM_Q = MEM_HEADS * MEM_HEAD_DIM

COL_QA = 0
COL_QB = 8
COL_KB = 16
COL_VB = 24
COL_ZB = 32
COL_QC = 40
COL_GL = 48
COL_KA = 72
COL_VA = 73
COL_AB = 74
N_PROJ = 75 * LANE

VMEM_LIMIT = 48 * 1024 * 1024


def _dot(a, b):
    return jnp.dot(a, b, preferred_element_type=F32)


def _dot_nt(a, b):
    return lax.dot_general(a, b, (((1,), (1,)), ((), ())), preferred_element_type=F32)


def _dot_tn(a, b):
    return lax.dot_general(a, b, (((0,), (0,)), ((), ())), preferred_element_type=F32)


def _rms(x, g):
    return x * lax.rsqrt(jnp.mean(x * x, axis=-1, keepdims=True) + EPS) * g


def _sigmoid(x):
    return 1.0 / (1.0 + jnp.exp(-x))


def _split3(x):
    hi = x.astype(BF16)
    r = x - hi.astype(F32)
    mid = r.astype(BF16)
    lo = (r - mid.astype(F32)).astype(BF16)
    return hi, mid, lo


def _mem_kv_kernel(m_ref, g_ref, w_ref, o_ref):
    h = _rms(m_ref[...], g_ref[...]).astype(BF16)
    o_ref[...] = _dot(h, w_ref[...])


def _mem_kv(mem, g_mem, w_bf16):
    n = w_bf16.shape[1]
    tn = 1024
    return pl.pallas_call(
        _mem_kv_kernel,
        grid=(n // tn,),
        in_specs=[
            pl.BlockSpec((N_MEM, D_MODEL), lambda j: (0, 0)),
            pl.BlockSpec((1, D_MODEL), lambda j: (0, 0)),
            pl.BlockSpec((D_MODEL, tn), lambda j: (0, j)),
        ],
        out_specs=pl.BlockSpec((N_MEM, tn), lambda j: (0, j)),
        out_shape=jax.ShapeDtypeStruct((N_MEM, n), F32),
        compiler_params=pltpu.CompilerParams(dimension_semantics=("arbitrary",), vmem_limit_bytes=VMEM_LIMIT),
        name="mem_kv",
    )(mem, g_mem, w_bf16)


def _in_proj_kernel(x_ref, g_ref, w_ref, b_ref, o_ref, h_ref):
    @pl.when(pl.program_id(1) == 0)
    def _():
        h_ref[...] = _rms(x_ref[...], g_ref[...]).astype(BF16)

    o_ref[...] = _dot(h_ref[...], w_ref[...]) + b_ref[...]


def _in_proj(x, g, w_bf16, b, tm, tn):
    m = x.shape[0]
    return pl.pallas_call(
        _in_proj_kernel,
        grid=(m // tm, N_PROJ // tn),
        in_specs=[
            pl.BlockSpec((tm, D_MODEL), lambda i, j: (i, 0)),
            pl.BlockSpec((1, D_MODEL), lambda i, j: (0, 0)),
            pl.BlockSpec((D_MODEL, tn), lambda i, j: (0, j)),
            pl.BlockSpec((1, tn), lambda i, j: (0, j)),
        ],
        out_specs=pl.BlockSpec((tm, tn), lambda i, j: (i, j)),
        out_shape=jax.ShapeDtypeStruct((m, N_PROJ), F32),
        scratch_shapes=[pltpu.VMEM((tm, D_MODEL), BF16)],
        compiler_params=pltpu.CompilerParams(dimension_semantics=("parallel", "arbitrary"),
                                             vmem_limit_bytes=VMEM_LIMIT),
        name="in_proj",
    )(x, g, w_bf16, b)


def _rope(x, cos, sin_signed):
    lane = lax.broadcasted_iota(jnp.int32, x.shape, 1)
    first = (lane % SWA_HEAD_DIM) < (SWA_HEAD_DIM // 2)
    rot = jnp.where(first, pltpu.roll(x, LANE - SWA_HEAD_DIM // 2, 1), pltpu.roll(x, SWA_HEAD_DIM // 2, 1))
    return x * cos + rot * sin_signed


KEYS = WINDOW + CHUNK
KPAD = 256


def _swa_kernel(sink_ref, q_ref, k_ref, v_ref, kh_ref, vh_ref, cos_ref, sin_ref, cosh_ref, sinh_ref,
                o_ref, kout_ref, qs_ref, ka_ref, kb_ref, va_ref, vb_ref, *, tq, cache):
    i = pl.program_id(1)
    cos = cos_ref[...]
    sin = sin_ref[...]
    kr = _rope(k_ref[...], cos, sin)
    kout_ref[...] = kr
    if cache:
        khr = kh_ref[...]
    else:
        khr = _rope(kh_ref[...], cosh_ref[...], sinh_ref[...])
    zpad = jnp.zeros((KPAD - KEYS, LANE), F32)
    kall = jnp.concatenate([khr, kr, zpad], axis=0)
    vall = jnp.concatenate([vh_ref[...], v_ref[...], zpad], axis=0)
    lo = lax.broadcasted_iota(jnp.int32, kall.shape, 1) < SWA_HEAD_DIM
    zero = jnp.zeros_like(kall)
    for src, a_ref, b_ref in ((kall, ka_ref, kb_ref), (vall, va_ref, vb_ref)):
        rolled = pltpu.roll(src, SWA_HEAD_DIM, 1)
        a_ref[0] = jnp.where(lo, src, zero).astype(BF16)
        b_ref[0] = jnp.where(lo, zero, rolled).astype(BF16)
        a_ref[1] = jnp.where(lo, rolled, zero).astype(BF16)
        b_ref[1] = jnp.where(lo, zero, src).astype(BF16)

    scale = SWA_HEAD_DIM ** -0.5
    for p in range(SWA_HEADS // 2):
        sl = slice(p * LANE, (p + 1) * LANE)
        qs_ref[:, sl] = (_rope(q_ref[:, sl], cos, sin) * scale).astype(BF16)

    jj = lax.broadcasted_iota(jnp.int32, (CHUNK, KPAD), 1)
    pairs_per_kv = SWA_HEADS // SWA_KV_HEADS // 2
    for c in range(tq // CHUNK):
        rows = slice(c * CHUNK, (c + 1) * CHUNK)
        if cache:
            valid = jj < KEYS
        else:
            valid = (jj < KEYS) & ((i > 0) | (jj + c * CHUNK >= WINDOW))
        krows = slice(c * CHUNK, c * CHUNK + KPAD)
        for g in range(SWA_KV_HEADS):
            kbd = jnp.concatenate([ka_ref[g, krows, :], kb_ref[g, krows, :]], axis=0)
            vbd = jnp.concatenate([va_ref[g, krows, :], vb_ref[g, krows, :]], axis=0)
            qst = jnp.concatenate(
                [qs_ref[rows, (g * pairs_per_kv + j) * LANE:(g * pairs_per_kv + j + 1) * LANE]
                 for j in range(pairs_per_kv)], axis=0)
            s = _dot_nt(qst, kbd)
            pblocks = []
            for j in range(pairs_per_kv):
                halves = []
                for half in range(2):
                    sh = s[j * CHUNK:(j + 1) * CHUNK, half * KPAD:(half + 1) * KPAD]
                    sh = jnp.where(valid, sh, -jnp.inf)
                    sink = sink_ref[2 * (g * pairs_per_kv + j) + half]
                    m = jnp.maximum(jnp.max(sh, axis=-1, keepdims=True), sink)
                    e = jnp.exp(sh - m)
                    den = jnp.sum(e, axis=-1, keepdims=True) + jnp.exp(sink - m)
                    halves.append((e * (1.0 / den)).astype(BF16))
                pblocks.append(jnp.concatenate(halves, axis=1))
            pmat = jnp.concatenate(pblocks, axis=0)
            o = _dot(pmat, vbd)
            for j in range(pairs_per_kv):
                p = g * pairs_per_kv + j
                o_ref[rows, p * LANE:(p + 1) * LANE] = o[j * CHUNK:(j + 1) * CHUNK]


def _swa(proj, k_hist, v_hist, cos, sin, sinks, nb, t, tq, cache):
    nt = t // tq
    kv_rows = WINDOW + tq + (KPAD - KEYS)
    if cache:
        hist_k_spec = pl.BlockSpec((None, WINDOW, LANE), lambda b, i: (b, 0, 0))
        hist_v_spec = pl.BlockSpec((None, WINDOW, LANE), lambda b, i: (b, 0, 0))
        hk, hv = k_hist, v_hist
        tab_spec_h = pl.BlockSpec((WINDOW, LANE), lambda b, i: (0, 0))
        cos_h = jnp.zeros((WINDOW, LANE), F32)
        sin_h = cos_h
    else:
        r = tq // WINDOW

        def hidx(b, i):
            return jnp.maximum((b * nt + i) * r - 1, 0)

        hist_k_spec = pl.BlockSpec((WINDOW, LANE), lambda b, i: (hidx(b, i), COL_KA))
        hist_v_spec = pl.BlockSpec((WINDOW, LANE), lambda b, i: (hidx(b, i), COL_VA))
        hk, hv = proj, proj
        tab_spec_h = pl.BlockSpec((WINDOW, LANE), lambda b, i: (jnp.maximum(i * r - 1, 0), 0))
        cos_h, sin_h = cos, sin
    return pl.pallas_call(
        functools.partial(_swa_kernel, tq=tq, cache=cache),
        grid=(nb, nt),
        in_specs=[
            pl.BlockSpec(memory_space=pltpu.SMEM),
            pl.BlockSpec((tq, SWA_Q), lambda b, i: (b * nt + i, COL_QA // 8)),
            pl.BlockSpec((tq, LANE), lambda b, i: (b * nt + i, COL_KA)),
            pl.BlockSpec((tq, LANE), lambda b, i: (b * nt + i, COL_VA)),
            hist_k_spec,
            hist_v_spec,
            pl.BlockSpec((tq, LANE), lambda b, i: (i, 0)),
            pl.BlockSpec((tq, LANE), lambda b, i: (i, 0)),
            tab_spec_h,
            tab_spec_h,
        ],
        out_specs=[
            pl.BlockSpec((tq, SWA_Q), lambda b, i: (b * nt + i, 0)),
            pl.BlockSpec((tq, LANE), lambda b, i: (b * nt + i, 0)),
        ],
        out_shape=[
            jax.ShapeDtypeStruct((nb * t, SWA_Q), F32),
            jax.ShapeDtypeStruct((nb * t, LANE), F32),
        ],
        scratch_shapes=[
            pltpu.VMEM((tq, SWA_Q), BF16),
            pltpu.VMEM((SWA_KV_HEADS, kv_rows, LANE), BF16),
            pltpu.VMEM((SWA_KV_HEADS, kv_rows, LANE), BF16),
            pltpu.VMEM((SWA_KV_HEADS, kv_rows, LANE), BF16),
            pltpu.VMEM((SWA_KV_HEADS, kv_rows, LANE), BF16),
        ],
        compiler_params=pltpu.CompilerParams(dimension_semantics=("parallel", "arbitrary"),
                                             vmem_limit_bytes=VMEM_LIMIT),
        name="swa",
    )(sinks, proj, proj, proj, hk, hv, cos, sin, cos_h, sin_h)


INV_BASE = 16


def _inverse_minus_identity(pm, row, col, chunk):
    d = jnp.where((row // INV_BASE) == (col // INV_BASE), pm, 0.0)
    r = d
    m = d
    for _ in range(INV_BASE.bit_length() - 2):
        mb = m.astype(BF16)
        m = _dot(mb, mb)
        r = r + m + _dot(m.astype(BF16), r.astype(BF16))
    bs = INV_BASE
    while bs < chunk:
        inner = (row // bs) == (col // bs)
        outer = (row // (2 * bs)) == (col // (2 * bs))
        off = jnp.where(outer & jnp.logical_not(inner), pm, 0.0)
        x = off + _dot(off.astype(BF16), r.astype(BF16))
        r = r + x + _dot(r.astype(BF16), x.astype(BF16))
        bs *= 2
    return r


def _gdn_kernel(q_ref, k_ref, v_ref, z_ref, ab_ref, hq_ref, hk_ref, hv_ref, cwq_ref, cwk_ref, cwv_ref,
                alog_ref, dt_ref, ng_ref, s0_ref, o_ref, sout_ref, s_scr, carry_scr, ext_scr, conv_scr,
                *, tt, seg, chunk, carry):
    h = pl.program_id(0)
    i = pl.program_id(1)
    nseg = tt // seg
    nblk = LANE // chunk
    ngroups = tt // LANE

    def init():
        for s in range(nseg):
            s_scr[s] = s0_ref[s, 0]
            for cg, hist in enumerate((hq_ref, hk_ref, hv_ref)):
                carry_scr[s, cg] = jnp.zeros((8, LANE), F32)
                carry_scr[s, cg, 8 - (CONV_W - 1):8, :] = hist[s]

    if carry:
        pl.when(i == 0)(init)
    else:
        init()

    for s in range(nseg):
        for cg, (x_ref, cw_ref) in enumerate(((q_ref, cwq_ref), (k_ref, cwk_ref), (v_ref, cwv_ref))):
            ext_scr[0:8, :] = carry_scr[s, cg]
            ext_scr[8:8 + seg, :] = x_ref[s * seg:(s + 1) * seg, :]
            base = 8 - (CONV_W - 1)
            y = ext_scr[base:base + seg, :] * cw_ref[0:1, :]
            for tap in range(1, CONV_W):
                y = y + ext_scr[base + tap:base + tap + seg, :] * cw_ref[tap:tap + 1, :]
            if carry:
                carry_scr[s, cg] = ext_scr[seg:seg + 8, :]
            conv_scr[cg, s * seg:(s + 1) * seg, :] = y * _sigmoid(y)

    row = lax.broadcasted_iota(jnp.int32, (LANE, LANE), 0)
    col = lax.broadcasted_iota(jnp.int32, (LANE, LANE), 1)
    same = (row // chunk) == (col // chunk)
    incl = same & (row >= col)
    strict = same & (row > col)
    lmask = jnp.where(incl, 1.0, 0.0).astype(BF16)
    sel_a = jnp.where(row == h, 1.0, 0.0).astype(BF16)
    sel_b = jnp.where(row == h + GDN_HEADS, 1.0, 0.0).astype(BF16)

    for gi in range(ngroups):
        rows = slice(gi * LANE, (gi + 1) * LANE)
        q = conv_scr[0, rows, :]
        k = conv_scr[1, rows, :]
        v = conv_scr[2, rows, :]
        qn = q * lax.rsqrt(jnp.sum(q * q, axis=-1, keepdims=True) + EPS) * (GDN_DK ** -0.5)
        kn = k * lax.rsqrt(jnp.sum(k * k, axis=-1, keepdims=True) + EPS)
        ab = ab_ref[rows, :]
        xa = ab + dt_ref[...]
        softplus = jnp.maximum(xa, 0.0) + jnp.log1p(jnp.exp(-jnp.abs(xa)))
        la = -jnp.exp(alog_ref[...]) * softplus
        beta = _sigmoid(ab)
        la_b = sum(_dot(piece, sel_a) for piece in _split3(la))
        bb = sum(_dot(piece, sel_b) for piece in _split3(beta))
        gb = sum(_dot(lmask, piece) for piece in _split3(la_b))
        grow = gb.T
        decay = jnp.exp(jnp.where(incl, gb - grow, -jnp.inf))
        knb = kn.astype(BF16)
        cat = _dot_nt(jnp.concatenate([qn.astype(BF16), knb], axis=0), knb)
        qk = cat[:LANE] * decay
        pm = jnp.where(strict, -(bb * cat[LANE:] * decay), 0.0)
        r = _inverse_minus_identity(pm, row, col, chunk)
        eg = jnp.exp(gb)
        rhs = jnp.concatenate([v * bb, kn * (bb * eg)], axis=1)
        sol = rhs + _dot(r.astype(BF16), rhs.astype(BF16))
        u = sol[:, :GDN_DV]
        w = sol[:, GDN_DV:]
        qd = qn * eg
        glast = gb[chunk - 1:chunk, :]
        for blk in range(1, nblk):
            glast = jnp.where(row >= blk * chunk, gb[(blk + 1) * chunk - 1:(blk + 1) * chunk, :], glast)
        kd = kn * jnp.exp(glast - gb)
        wq = jnp.concatenate([w, qd], axis=0).astype(BF16)
        ws = None
        for blk in range(nblk):
            sidx = blk if not carry else 0
            part = _dot(wq, s_scr[sidx].astype(BF16))
            ws = part if ws is None else jnp.where(jnp.concatenate([row, row], axis=0) >= blk * chunk, part, ws)
        vnew = u - ws[:LANE]
        vnb = vnew.astype(BF16)
        o = ws[LANE:] + _dot(qk.astype(BF16), vnb)
        for blk in range(nblk):
            sidx = blk if not carry else 0
            kdb = kd if nblk == 1 else jnp.where((row // chunk) == blk, kd, 0.0)
            gtot = jnp.exp(gb[(blk + 1) * chunk - 1:(blk + 1) * chunk, :])
            s_scr[sidx] = s_scr[sidx] * gtot + _dot_tn(kdb.astype(BF16), vnb)
        z = z_ref[rows, :]
        o_ref[rows, :] = _rms(o, ng_ref[...]) * (z * _sigmoid(z))

    def fin():
        for s in range(nseg):
            sout_ref[s, 0] = s_scr[s]

    if carry:
        pl.when(i == pl.num_programs(1) - 1)(fin)
    else:
        fin()


def _gdn(proj, hist, s0, conv_w, alog_row, dt_row, norm_g, t_rows, tt, seg, chunk, carry):
    nt = t_rows // tt
    nseg = tt // seg
    if carry:
        seq_idx = lambda h, i: 0
    else:
        seq_idx = lambda h, i: i
    nseq = s0.shape[0]

    def col(c0):
        return pl.BlockSpec((tt, LANE), lambda h, i: (i, c0 + h))

    def hspec(c0):
        return pl.BlockSpec((nseg, CONV_W - 1, LANE), lambda h, i: (seq_idx(h, i), 0, c0 + h))

    def wspec(c0):
        return pl.BlockSpec((CONV_W, LANE), lambda h, i: (0, c0 + h))

    row_spec = pl.BlockSpec((1, LANE), lambda h, i: (0, 0))
    sspec = pl.BlockSpec((nseg, 1, GDN_DK, GDN_DV), lambda h, i: (seq_idx(h, i), h, 0, 0))
    return pl.pallas_call(
        functools.partial(_gdn_kernel, tt=tt, seg=seg, chunk=chunk, carry=carry),
        grid=(GDN_HEADS, nt),
        in_specs=[
            col(COL_QB), col(COL_KB), col(COL_VB), col(COL_ZB),
            pl.BlockSpec((tt, LANE), lambda h, i: (i, COL_AB)),
            hspec(0), hspec(GDN_HEADS), hspec(2 * GDN_HEADS),
            wspec(0), wspec(GDN_HEADS), wspec(2 * GDN_HEADS),
            row_spec, row_spec, row_spec,
            sspec,
        ],
        out_specs=[
            pl.BlockSpec((tt, LANE), lambda h, i: (i, h)),
            sspec,
        ],
        out_shape=[
            jax.ShapeDtypeStruct((t_rows, GDN_V), F32),
            jax.ShapeDtypeStruct((nseq, GDN_HEADS, GDN_DK, GDN_DV), F32),
        ],
        scratch_shapes=[
            pltpu.VMEM((nseg, GDN_DK, GDN_DV), F32),
            pltpu.VMEM((nseg, 3, 8, LANE), F32),
            pltpu.VMEM((8 + seg, LANE), F32),
            pltpu.VMEM((3, tt, LANE), F32),
        ],
        compiler_params=pltpu.CompilerParams(dimension_semantics=("parallel", "arbitrary"),
                                             vmem_limit_bytes=VMEM_LIMIT),
        name="gdn",
    )(proj, proj, proj, proj, proj, hist, hist, hist, conv_w, conv_w, conv_w, alog_row, dt_row, norm_g, s0)


def _mem_attn_kernel(q_ref, mk_ref, mv_ref, o_ref):
    scale = MEM_HEAD_DIM ** -0.5
    for hh in range(MEM_HEADS):
        sl = slice(hh * MEM_HEAD_DIM, (hh + 1) * MEM_HEAD_DIM)
        q = (q_ref[:, sl] * scale).astype(BF16)
        s = _dot_nt(q, mk_ref[:, sl].astype(BF16))
        m = jnp.max(s, axis=-1, keepdims=True)
        e = jnp.exp(s - m)
        p = e * (1.0 / jnp.sum(e, axis=-1, keepdims=True))
        o_ref[:, sl] = _dot(p.astype(BF16), mv_ref[:, sl].astype(BF16))


def _mem_attn(proj, mk, mv, nb, t, tt, batched):
    nt = t // tt
    if batched:
        kspec = pl.BlockSpec((None, N_MEM, MEM_Q), lambda b, i: (b, 0, 0))
        vspec = kspec
    else:
        kspec = pl.BlockSpec((N_MEM, MEM_Q), lambda b, i: (0, 0))
        vspec = pl.BlockSpec((N_MEM, MEM_Q), lambda b, i: (0, 1))
    return pl.pallas_call(
        _mem_attn_kernel,
        grid=(nb, nt),
        in_specs=[pl.BlockSpec((tt, MEM_Q), lambda b, i: (b * nt + i, COL_QC // 8)), kspec, vspec],
        out_specs=pl.BlockSpec((tt, MEM_Q), lambda b, i: (b * nt + i, 0)),
        out_shape=jax.ShapeDtypeStruct((nb * t, MEM_Q), F32),
        compiler_params=pltpu.CompilerParams(dimension_semantics=("parallel", "arbitrary"),
                                             vmem_limit_bytes=VMEM_LIMIT),
        name="mem_attn",
    )(proj, mk, mv)


def _merge_kernel(x_ref, oa_ref, ob_ref, oc_ref, g0_ref, g1_ref, g2_ref, wb_ref, wo_ref, gp_ref, o_ref):
    merged = None
    for n, (br_ref, gl_ref) in enumerate(((oa_ref, g0_ref), (ob_ref, g1_ref), (oc_ref, g2_ref))):
        br = _dot(br_ref[...].astype(BF16), wb_ref[n])
        term = _sigmoid(gl_ref[...]) * br
        merged = term if merged is None else merged + term
    y = _dot(merged.astype(BF16), wo_ref[...])
    o_ref[...] = x_ref[...] + _rms(y, gp_ref[...])


def _merge(x, oa, ob, oc, proj, wb, wo, gp, tm):
    m = x.shape[0]
    tile = pl.BlockSpec((tm, D_MODEL), lambda i: (i, 0))

    def gspec(n):
        return pl.BlockSpec((tm, D_MODEL), lambda i: (i, COL_GL // 8 + n))

    return pl.pallas_call(
        _merge_kernel,
        grid=(m // tm,),
        in_specs=[
            tile, tile, tile, tile, gspec(0), gspec(1), gspec(2),
            pl.BlockSpec((N_BRANCH, D_MODEL, D_MODEL), lambda i: (0, 0, 0)),
            pl.BlockSpec((D_MODEL, D_MODEL), lambda i: (0, 0)),
            pl.BlockSpec((1, D_MODEL), lambda i: (0, 0)),
        ],
        out_specs=tile,
        out_shape=jax.ShapeDtypeStruct((m, D_MODEL), F32),
        compiler_params=pltpu.CompilerParams(dimension_semantics=("parallel",), vmem_limit_bytes=VMEM_LIMIT),
        name="merge",
    )(x, oa, ob, oc, proj, proj, proj, wb, wo, gp)


FF_CHUNK = 1024


def _ffn_kernel(x_ref, g1_ref, wu_ref, wd_ref, g2_ref, o_ref):
    x = x_ref[...]
    hf = _rms(x, g1_ref[...]).astype(BF16)
    f = None
    for c in range(D_FF // FF_CHUNK):
        sl = slice(c * FF_CHUNK, (c + 1) * FF_CHUNK)
        up = jnp.maximum(_dot(hf, wu_ref[:, sl]), 0.0)
        part = _dot((up * up).astype(BF16), wd_ref[sl, :])
        f = part if f is None else f + part
    o_ref[...] = x + _rms(f, g2_ref[...])


def _ffn(x, g1, wu, wd, g2, tm):
    m = x.shape[0]
    tile = pl.BlockSpec((tm, D_MODEL), lambda i: (i, 0))
    vec = pl.BlockSpec((1, D_MODEL), lambda i: (0, 0))
    return pl.pallas_call(
        _ffn_kernel,
        grid=(m // tm,),
        in_specs=[tile, vec, pl.BlockSpec((D_MODEL, D_FF), lambda i: (0, 0)),
                  pl.BlockSpec((D_FF, D_MODEL), lambda i: (0, 0)), vec],
        out_specs=tile,
        out_shape=jax.ShapeDtypeStruct((m, D_MODEL), F32),
        compiler_params=pltpu.CompilerParams(dimension_semantics=("parallel",), vmem_limit_bytes=VMEM_LIMIT),
        name="ffn",
    )(x, g1, wu, wd, g2)


def _regroup_cols(w):
    o_ka = SWA_Q
    o_va = o_ka + SWA_KV
    o_qb = o_va + SWA_KV
    o_ab = o_qb + 2 * GDN_QK + 2 * GDN_V
    o_qc = o_ab + 2 * GDN_HEADS
    o_gl = o_qc + MEM_Q
    o_end = o_gl + N_BRANCH * D_MODEL
    parts = [w[..., :o_ka], w[..., o_qb:o_ab], w[..., o_qc:o_end], w[..., o_ka:o_qb], w[..., o_ab:o_qc]]
    out = jnp.concatenate(parts, axis=-1)
    pad = N_PROJ - out.shape[-1]
    return jnp.pad(out, [(0, 0)] * (out.ndim - 1) + [(0, pad)])


def _rope_tables(pos):
    half = SWA_HEAD_DIM // 2
    inv = ROPE_THETA ** (-jnp.arange(half, dtype=F32) / half)
    ang = pos.astype(F32)[:, None] * inv[None, :]
    cos = jnp.cos(ang)
    sin = jnp.sin(ang)
    cos_t = jnp.concatenate([cos, cos, cos, cos], axis=1)
    sin_t = jnp.concatenate([-sin, sin, -sin, sin], axis=1)
    return cos_t, sin_t


def _pad_row(v, offset=0):
    return jnp.zeros((1, LANE), F32).at[0, offset:offset + v.shape[0]].set(v)


def _layer(x, nb, t, pos, k_hist, v_hist, s0, conv_hist, mk, mv, wts, cfg):
    proj = _in_proj(x, wts['g_pre_mix'], wts['w_in'], wts['b_in'], cfg['tm_proj'], cfg['tn_proj'])
    cos, sin = _rope_tables(pos)
    oa, k_rot = _swa(proj, k_hist, v_hist, cos, sin, wts['sinks'], nb, t, cfg['tq'], cfg['cache'])
    ob, s_new = _gdn(proj, conv_hist, s0, wts['conv_w'], wts['alog'], wts['dt'], wts['norm_g'],
                     nb * t, cfg['tt_gdn'], cfg['seg'], cfg['chunk'], cfg['carry'])
    oc = _mem_attn(proj, mk, mv, nb, t, cfg['tt_mem'], cfg['cache'])
    x1 = _merge(x, oa, ob, oc, proj, wts['w_branch'], wts['w_out'], wts['g_post_mix'], cfg['tm'])
    y = _ffn(x1, wts['g_pre_ffn'], wts['w_up'], wts['w_down'], wts['g_post_ffn'], cfg['tm'])
    return y, k_rot, s_new, proj


def kernel(x_prompt, x_sample, mem_prompt, cache_swa_k, cache_swa_v, state_gdn, state_conv, cache_mem_k, cache_mem_v, g_pre_mix, w_in, b_in, swa_sinks, conv_w, gdn_a_log, gdn_dt_bias, gdn_norm_g, g_mem, w_mem_kv, w_branch, w_out, g_post_mix, g_pre_ffn, w_up, w_down, g_post_ffn):
    depth = w_in.shape[0]
    assert depth == 1
    l = 0
    bp, tp, _ = x_prompt.shape
    bs, ts, _ = x_sample.shape
    assert bp == 1

    wts = {
        'g_pre_mix': g_pre_mix[l][None, :],
        'w_in': _regroup_cols(w_in[l]).astype(BF16),
        'b_in': _regroup_cols(b_in[l])[None, :],
        'sinks': swa_sinks[l],
        'conv_w': conv_w[l],
        'alog': _pad_row(gdn_a_log[l]),
        'dt': _pad_row(gdn_dt_bias[l]),
        'norm_g': gdn_norm_g[l][None, :],
        'w_branch': w_branch[l].astype(BF16),
        'w_out': w_out[l].astype(BF16),
        'g_post_mix': g_post_mix[l][None, :],
        'g_pre_ffn': g_pre_ffn[l][None, :],
        'w_up': w_up[l].astype(BF16),
        'w_down': w_down[l].astype(BF16),
        'g_post_ffn': g_post_ffn[l][None, :],
    }

    mkv = _mem_kv(mem_prompt[0], g_mem[l][None, :], w_mem_kv[l].astype(BF16))
    cfg_p = dict(tm_proj=512, tn_proj=1920, tq=256, cache=False, tt_gdn=512, seg=512, chunk=128, carry=True,
                 tt_mem=512, tm=256)
    zero_hist = jnp.zeros((1, CONV_W - 1, 3 * GDN_QK), F32)
    zero_s = jnp.zeros((1, GDN_HEADS, GDN_DK, GDN_DV), F32)
    yp, kp_rot, sp_new, proj_p = _layer(
        x_prompt.reshape(bp * tp, D_MODEL), bp, tp, jnp.arange(tp, dtype=jnp.int32), None, None,
        zero_s, zero_hist, mkv, mkv, wts, cfg_p)

    cfg_s = dict(tm_proj=512, tn_proj=1920, tq=ts, cache=True, tt_gdn=128, seg=ts, chunk=ts, carry=False,
                 tt_mem=ts, tm=256)
    ys, ks_rot, ss_new, proj_s = _layer(
        x_sample.reshape(bs * ts, D_MODEL), bs, ts, PAST_LEN + jnp.arange(ts, dtype=jnp.int32),
        cache_swa_k[l].reshape(bs, WINDOW, SWA_KV), cache_swa_v[l].reshape(bs, WINDOW, SWA_KV),
        state_gdn[l], state_conv[l],
        cache_mem_k[l].reshape(bs, N_MEM, MEM_Q), cache_mem_v[l].reshape(bs, N_MEM, MEM_Q), wts, cfg_s)

    kv_shape = (SWA_KV_HEADS, SWA_HEAD_DIM)
    conv_cols = slice(COL_QB * LANE, (COL_ZB) * LANE)
    va_cols = slice(COL_VA * LANE, (COL_VA + 1) * LANE)

    p_k = kp_rot[tp - WINDOW:].reshape((1, bp, WINDOW) + kv_shape)
    p_v = proj_p[tp - WINDOW:, va_cols].reshape((1, bp, WINDOW) + kv_shape)
    p_c = proj_p[tp - (CONV_W - 1):, conv_cols].reshape(1, bp, CONV_W - 1, 3 * GDN_QK)
    p_mk = mkv[:, :MEM_Q].reshape(1, bp, N_MEM, MEM_HEADS, MEM_HEAD_DIM)
    p_mv = mkv[:, MEM_Q:].reshape(1, bp, N_MEM, MEM_HEADS, MEM_HEAD_DIM)

    ks_new = ks_rot.reshape(bs, ts, SWA_KV)
    vs_new = proj_s[:, va_cols].reshape(bs, ts, SWA_KV)
    s_k = jnp.concatenate([cache_swa_k[l].reshape(bs, WINDOW, SWA_KV), ks_new], axis=1)[:, -WINDOW:]
    s_v = jnp.concatenate([cache_swa_v[l].reshape(bs, WINDOW, SWA_KV), vs_new], axis=1)[:, -WINDOW:]
    s_c = proj_s[:, conv_cols].reshape(bs, ts, 3 * GDN_QK)[:, ts - (CONV_W - 1):]

    return (yp.reshape(bp, tp, D_MODEL), ys.reshape(bs, ts, D_MODEL),
            p_k, p_v, sp_new[None], p_c, p_mk, p_mv,
            s_k.reshape((1, bs, WINDOW) + kv_shape), s_v.reshape((1, bs, WINDOW) + kv_shape),
            ss_new[None], s_c[None])
```

```python
import functools

import jax
import jax.numpy as jnp
from jax import lax
from jax.experimental import pallas as pl
from jax.experimental.pallas import tpu as pltpu

F32 = jnp.float32
BF16 = jnp.bfloat16

D_MODEL = 1024
PAST_LEN = 2048
CHUNK = 64
N_MEM = 256
EPS = 1e-6
SWA_HEADS = 16
SWA_KV_HEADS = 2
SWA_HEAD_DIM = 64
WINDOW = 128
ROPE_THETA = 10000.0
GDN_HEADS = 8
GDN_DK = 128
GDN_DV = 128
CONV_W = 4
MEM_HEADS = 4
MEM_HEAD_DIM = 256
N_BRANCH = 3
D_FF = 4 * D_MODEL

LANE = 128
SWA_Q = SWA_HEADS * SWA_HEAD_DIM
SWA_KV = SWA_KV_HEADS * SWA_HEAD_DIM
GDN_QK = GDN_HEADS * GDN_DK
GDN_V = GDN_HEADS * GDN_DV
MEM_Q = MEM_HEADS * MEM_HEAD_DIM

COL_QA = 0
COL_QB = 8
COL_KB = 16
COL_VB = 24
COL_ZB = 32
COL_QC = 40
COL_GL = 48
COL_KA = 72
COL_VA = 73
COL_AB = 74
N_PROJ = 75 * LANE

VMEM_LIMIT = 48 * 1024 * 1024


def _dot(a, b):
    return jnp.dot(a, b, preferred_element_type=F32)


def _dot_nt(a, b):
    return lax.dot_general(a, b, (((1,), (1,)), ((), ())), preferred_element_type=F32)


def _dot_tn(a, b):
    return lax.dot_general(a, b, (((0,), (0,)), ((), ())), preferred_element_type=F32)


def _rms(x, g):
    return x * lax.rsqrt(jnp.mean(x * x, axis=-1, keepdims=True) + EPS) * g


def _sigmoid(x):
    return 1.0 / (1.0 + jnp.exp(-x))


def _split3(x):
    hi = x.astype(BF16)
    r = x - hi.astype(F32)
    mid = r.astype(BF16)
    lo = (r - mid.astype(F32)).astype(BF16)
    return hi, mid, lo


def _mem_kv_kernel(m_ref, g_ref, w_ref, o_ref):
    h = _rms(m_ref[...], g_ref[...]).astype(BF16)
    o_ref[...] = _dot(h, w_ref[...])


def _mem_kv(mem, g_mem, w_bf16):
    n = w_bf16.shape[1]
    tn = 1024
    return pl.pallas_call(
        _mem_kv_kernel,
        grid=(n // tn,),
        in_specs=[
            pl.BlockSpec((N_MEM, D_MODEL), lambda j: (0, 0)),
            pl.BlockSpec((1, D_MODEL), lambda j: (0, 0)),
            pl.BlockSpec((D_MODEL, tn), lambda j: (0, j)),
        ],
        out_specs=pl.BlockSpec((N_MEM, tn), lambda j: (0, j)),
        out_shape=jax.ShapeDtypeStruct((N_MEM, n), F32),
        compiler_params=pltpu.CompilerParams(dimension_semantics=("arbitrary",), vmem_limit_bytes=VMEM_LIMIT),
        name="mem_kv",
    )(mem, g_mem, w_bf16)


def _in_proj_kernel(x_ref, g_ref, w_ref, b_ref, o_ref, h_ref):
    @pl.when(pl.program_id(1) == 0)
    def _():
        h_ref[...] = _rms(x_ref[...], g_ref[...]).astype(BF16)

    o_ref[...] = _dot(h_ref[...], w_ref[...]) + b_ref[...]


def _in_proj(x, g, w_bf16, b, tm, tn):
    m = x.shape[0]
    return pl.pallas_call(
        _in_proj_kernel,
        grid=(m // tm, N_PROJ // tn),
        in_specs=[
            pl.BlockSpec((tm, D_MODEL), lambda i, j: (i, 0)),
            pl.BlockSpec((1, D_MODEL), lambda i, j: (0, 0)),
            pl.BlockSpec((D_MODEL, tn), lambda i, j: (0, j)),
            pl.BlockSpec((1, tn), lambda i, j: (0, j)),
        ],
        out_specs=pl.BlockSpec((tm, tn), lambda i, j: (i, j)),
        out_shape=jax.ShapeDtypeStruct((m, N_PROJ), F32),
        scratch_shapes=[pltpu.VMEM((tm, D_MODEL), BF16)],
        compiler_params=pltpu.CompilerParams(dimension_semantics=("parallel", "arbitrary"),
                                             vmem_limit_bytes=VMEM_LIMIT),
        name="in_proj",
    )(x, g, w_bf16, b)


def _rope(x, cos, sin_signed):
    lane = lax.broadcasted_iota(jnp.int32, x.shape, 1)
    first = (lane % SWA_HEAD_DIM) < (SWA_HEAD_DIM // 2)
    rot = jnp.where(first, pltpu.roll(x, LANE - SWA_HEAD_DIM // 2, 1), pltpu.roll(x, SWA_HEAD_DIM // 2, 1))
    return x * cos + rot * sin_signed


KEYS = WINDOW + CHUNK
KPAD = 256


def _swa_kernel(sink_ref, q_ref, k_ref, v_ref, kh_ref, vh_ref, cos_ref, sin_ref, cosh_ref, sinh_ref,
                o_ref, kout_ref, qs_ref, ka_ref, kb_ref, va_ref, vb_ref, *, tq, cache):
    i = pl.program_id(1)
    cos = cos_ref[...]
    sin = sin_ref[...]
    kr = _rope(k_ref[...], cos, sin)
    kout_ref[...] = kr
    if cache:
        khr = kh_ref[...]
    else:
        khr = _rope(kh_ref[...], cosh_ref[...], sinh_ref[...])
    zpad = jnp.zeros((KPAD - KEYS, LANE), F32)
    kall = jnp.concatenate([khr, kr, zpad], axis=0)
    vall = jnp.concatenate([vh_ref[...], v_ref[...], zpad], axis=0)
    lo = lax.broadcasted_iota(jnp.int32, kall.shape, 1) < SWA_HEAD_DIM
    zero = jnp.zeros_like(kall)
    for src, a_ref, b_ref in ((kall, ka_ref, kb_ref), (vall, va_ref, vb_ref)):
        rolled = pltpu.roll(src, SWA_HEAD_DIM, 1)
        a_ref[0] = jnp.where(lo, src, zero).astype(BF16)
        b_ref[0] = jnp.where(lo, zero, rolled).astype(BF16)
        a_ref[1] = jnp.where(lo, rolled, zero).astype(BF16)
        b_ref[1] = jnp.where(lo, zero, src).astype(BF16)

    scale = SWA_HEAD_DIM ** -0.5
    for p in range(SWA_HEADS // 2):
        sl = slice(p * LANE, (p + 1) * LANE)
        qs_ref[:, sl] = (_rope(q_ref[:, sl], cos, sin) * scale).astype(BF16)

    jj = lax.broadcasted_iota(jnp.int32, (CHUNK, KPAD), 1)
    pairs_per_kv = SWA_HEADS // SWA_KV_HEADS // 2
    for c in range(tq // CHUNK):
        rows = slice(c * CHUNK, (c + 1) * CHUNK)
        if cache:
            valid = jj < KEYS
        else:
            valid = (jj < KEYS) & ((i > 0) | (jj + c * CHUNK >= WINDOW))
        krows = slice(c * CHUNK, c * CHUNK + KPAD)
        for g in range(SWA_KV_HEADS):
            kbd = jnp.concatenate([ka_ref[g, krows, :], kb_ref[g, krows, :]], axis=0)
            vbd = jnp.concatenate([va_ref[g, krows, :], vb_ref[g, krows, :]], axis=0)
            qst = jnp.concatenate(
                [qs_ref[rows, (g * pairs_per_kv + j) * LANE:(g * pairs_per_kv + j + 1) * LANE]
                 for j in range(pairs_per_kv)], axis=0)
            s = _dot_nt(qst, kbd)
            pblocks = []
            for j in range(pairs_per_kv):
                halves = []
                for half in range(2):
                    sh = s[j * CHUNK:(j + 1) * CHUNK, half * KPAD:(half + 1) * KPAD]
                    sh = jnp.where(valid, sh, -jnp.inf)
                    sink = sink_ref[2 * (g * pairs_per_kv + j) + half]
                    m = jnp.maximum(jnp.max(sh, axis=-1, keepdims=True), sink)
                    e = jnp.exp(sh - m)
                    den = jnp.sum(e, axis=-1, keepdims=True) + jnp.exp(sink - m)
                    halves.append((e * (1.0 / den)).astype(BF16))
                pblocks.append(jnp.concatenate(halves, axis=1))
            pmat = jnp.concatenate(pblocks, axis=0)
            o = _dot(pmat, vbd)
            for j in range(pairs_per_kv):
                p = g * pairs_per_kv + j
                o_ref[rows, p * LANE:(p + 1) * LANE] = o[j * CHUNK:(j + 1) * CHUNK]


def _swa(proj, k_hist, v_hist, cos, sin, sinks, nb, t, tq, cache):
    nt = t // tq
    kv_rows = WINDOW + tq + (KPAD - KEYS)
    if cache:
        hist_k_spec = pl.BlockSpec((None, WINDOW, LANE), lambda b, i: (b, 0, 0))
        hist_v_spec = pl.BlockSpec((None, WINDOW, LANE), lambda b, i: (b, 0, 0))
        hk, hv = k_hist, v_hist
        tab_spec_h = pl.BlockSpec((WINDOW, LANE), lambda b, i: (0, 0))
        cos_h = jnp.zeros((WINDOW, LANE), F32)
        sin_h = cos_h
    else:
        r = tq // WINDOW

        def hidx(b, i):
            return jnp.maximum((b * nt + i) * r - 1, 0)

        hist_k_spec = pl.BlockSpec((WINDOW, LANE), lambda b, i: (hidx(b, i), COL_KA))
        hist_v_spec = pl.BlockSpec((WINDOW, LANE), lambda b, i: (hidx(b, i), COL_VA))
        hk, hv = proj, proj
        tab_spec_h = pl.BlockSpec((WINDOW, LANE), lambda b, i: (jnp.maximum(i * r - 1, 0), 0))
        cos_h, sin_h = cos, sin
    return pl.pallas_call(
        functools.partial(_swa_kernel, tq=tq, cache=cache),
        grid=(nb, nt),
        in_specs=[
            pl.BlockSpec(memory_space=pltpu.SMEM),
            pl.BlockSpec((tq, SWA_Q), lambda b, i: (b * nt + i, COL_QA // 8)),
            pl.BlockSpec((tq, LANE), lambda b, i: (b * nt + i, COL_KA)),
            pl.BlockSpec((tq, LANE), lambda b, i: (b * nt + i, COL_VA)),
            hist_k_spec,
            hist_v_spec,
            pl.BlockSpec((tq, LANE), lambda b, i: (i, 0)),
            pl.BlockSpec((tq, LANE), lambda b, i: (i, 0)),
            tab_spec_h,
            tab_spec_h,
        ],
        out_specs=[
            pl.BlockSpec((tq, SWA_Q), lambda b, i: (b * nt + i, 0)),
            pl.BlockSpec((tq, LANE), lambda b, i: (b * nt + i, 0)),
        ],
        out_shape=[
            jax.ShapeDtypeStruct((nb * t, SWA_Q), F32),
            jax.ShapeDtypeStruct((nb * t, LANE), F32),
        ],
        scratch_shapes=[
            pltpu.VMEM((tq, SWA_Q), BF16),
            pltpu.VMEM((SWA_KV_HEADS, kv_rows, LANE), BF16),
            pltpu.VMEM((SWA_KV_HEADS, kv_rows, LANE), BF16),
            pltpu.VMEM((SWA_KV_HEADS, kv_rows, LANE), BF16),
            pltpu.VMEM((SWA_KV_HEADS, kv_rows, LANE), BF16),
        ],
        compiler_params=pltpu.CompilerParams(dimension_semantics=("parallel", "arbitrary"),
                                             vmem_limit_bytes=VMEM_LIMIT),
        name="swa",
    )(sinks, proj, proj, proj, hk, hv, cos, sin, cos_h, sin_h)


INV_BASE = 16


def _inverse_minus_identity(pms, row, col, chunk):
    base = (row // INV_BASE) == (col // INV_BASE)
    rs = [jnp.where(base, pm, 0.0) for pm in pms]
    ms = rs
    for _ in range(INV_BASE.bit_length() - 2):
        mbs = [m.astype(BF16) for m in ms]
        ms = [_dot(mb, mb) for mb in mbs]
        prods = [_dot(m.astype(BF16), r.astype(BF16)) for m, r in zip(ms, rs)]
        rs = [r + m + p for r, m, p in zip(rs, ms, prods)]
    bs = INV_BASE
    while bs < chunk:
        inner = (row // bs) == (col // bs)
        outer = (row // (2 * bs)) == (col // (2 * bs))
        sib = outer & jnp.logical_not(inner)
        offs = [jnp.where(sib, pm, 0.0) for pm in pms]
        xs = [off + _dot(off.astype(BF16), r.astype(BF16)) for off, r in zip(offs, rs)]
        prods = [_dot(r.astype(BF16), x.astype(BF16)) for r, x in zip(rs, xs)]
        rs = [r + x + p for r, x, p in zip(rs, xs, prods)]
        bs *= 2
    return rs


def _gdn_kernel(q_ref, k_ref, v_ref, z_ref, ab_ref, hist_ref, cw_ref, alog_ref, dt_ref, ng_ref, s0_ref,
                o_ref, sout_ref, s_scr, carry_scr, ext_scr, conv_scr, *, tt, seg, chunk, carry):
    i = pl.program_id(0)
    nseg = tt // seg
    nblk = LANE // chunk
    ngroups = tt // LANE
    hist_rows = CONV_W - 1

    def init():
        for s in range(nseg):
            s_scr[s] = s0_ref[s]
            carry_scr[s] = jnp.zeros((8, 3 * GDN_QK), F32)
            carry_scr[s, 8 - hist_rows:8, :] = hist_ref[s]

    if carry:
        pl.when(i == 0)(init)
    else:
        init()

    for s in range(nseg):
        ext_scr[0:8, :] = carry_scr[s]
        for cg, x_ref in enumerate((q_ref, k_ref, v_ref)):
            ext_scr[8:8 + seg, cg * GDN_QK:(cg + 1) * GDN_QK] = x_ref[s * seg:(s + 1) * seg, :]
        if carry:
            carry_scr[s] = ext_scr[seg:seg + 8, :]
        base = 8 - hist_rows
        for cb in range(3 * GDN_HEADS):
            lanes = slice(cb * LANE, (cb + 1) * LANE)
            y = ext_scr[base:base + seg, lanes] * cw_ref[0:1, lanes]
            for tap in range(1, CONV_W):
                y = y + ext_scr[base + tap:base + tap + seg, lanes] * cw_ref[tap:tap + 1, lanes]
            conv_scr[s * seg:(s + 1) * seg, lanes] = y * _sigmoid(y)

    row = lax.broadcasted_iota(jnp.int32, (LANE, LANE), 0)
    col = lax.broadcasted_iota(jnp.int32, (LANE, LANE), 1)
    same = (row // chunk) == (col // chunk)
    incl = same & (row >= col)
    strict = same & (row > col)
    lmask = jnp.where(incl, 1.0, 0.0).astype(BF16)
    row2 = jnp.concatenate([row, row], axis=0)

    inst = [(gi, hh) for gi in range(ngroups) for hh in range(GDN_HEADS)]
    heads = range(GDN_HEADS)

    gates = []
    for gi in range(ngroups):
        ab = ab_ref[gi * LANE:(gi + 1) * LANE, :]
        xa = ab + dt_ref[...]
        softplus = jnp.maximum(xa, 0.0) + jnp.log1p(jnp.exp(-jnp.abs(xa)))
        la = -jnp.exp(alog_ref[...]) * softplus
        beta = _sigmoid(ab)
        g_all = sum(_dot(lmask, piece) for piece in _split3(la))
        gates.append((g_all, g_all.T, beta))

    qn_l, kn_l, v_l, gb_l, bb_l, decay_l = [], [], [], [], [], []
    for gi, hh in inst:
        rows = slice(gi * LANE, (gi + 1) * LANE)
        q = conv_scr[rows, hh * LANE:(hh + 1) * LANE]
        k = conv_scr[rows, GDN_QK + hh * LANE:GDN_QK + (hh + 1) * LANE]
        v_l.append(conv_scr[rows, 2 * GDN_QK + hh * LANE:2 * GDN_QK + (hh + 1) * LANE])
        qn_l.append(q * lax.rsqrt(jnp.sum(q * q, axis=-1, keepdims=True) + EPS) * (GDN_DK ** -0.5))
        kn_l.append(k * lax.rsqrt(jnp.sum(k * k, axis=-1, keepdims=True) + EPS))
        g_all, g_all_t, beta = gates[gi]
        gb = jnp.broadcast_to(g_all[:, hh:hh + 1], (LANE, LANE))
        grow = jnp.broadcast_to(g_all_t[hh:hh + 1, :], (LANE, LANE))
        gb_l.append(gb)
        bb_l.append(jnp.broadcast_to(beta[:, GDN_HEADS + hh:GDN_HEADS + hh + 1], (LANE, LANE)))
        decay_l.append(jnp.exp(jnp.where(incl, gb - grow, -jnp.inf)))

    knb_l = [kn.astype(BF16) for kn in kn_l]
    cat_l = [_dot_nt(jnp.concatenate([qn.astype(BF16), knb], axis=0), knb)
             for qn, knb in zip(qn_l, knb_l)]
    qk_l = [cat[:LANE] * decay for cat, decay in zip(cat_l, decay_l)]
    pm_l = [jnp.where(strict, -(bb * cat[LANE:] * decay), 0.0)
            for bb, cat, decay in zip(bb_l, cat_l, decay_l)]
    r_l = _inverse_minus_identity(pm_l, row, col, chunk)
    eg_l = [jnp.exp(gb) for gb in gb_l]
    rhs_l = [jnp.concatenate([v * bb, kn * (bb * eg)], axis=1)
             for v, bb, kn, eg in zip(v_l, bb_l, kn_l, eg_l)]
    sol_l = [rhs + _dot(r.astype(BF16), rhs.astype(BF16)) for r, rhs in zip(r_l, rhs_l)]
    wq_l = [jnp.concatenate([sol[:, GDN_DV:], qn * eg], axis=0).astype(BF16)
            for sol, qn, eg in zip(sol_l, qn_l, eg_l)]
    kd_l = []
    for kn, gb in zip(kn_l, gb_l):
        glast = gb[chunk - 1:chunk, :]
        for blk in range(1, nblk):
            glast = jnp.where(row >= blk * chunk, gb[(blk + 1) * chunk - 1:(blk + 1) * chunk, :], glast)
        kd_l.append(kn * jnp.exp(glast - gb))

    nstate = 1 if carry else nblk
    state = [[s_scr[sidx, hh] for hh in heads] for sidx in range(nstate)]
    for gi in range(ngroups):
        ix = [gi * GDN_HEADS + hh for hh in heads]
        ws_l = []
        for hh in heads:
            ws = None
            for blk in range(nblk):
                part = _dot(wq_l[ix[hh]], state[0 if carry else blk][hh].astype(BF16))
                ws = part if ws is None else jnp.where(row2 >= blk * chunk, part, ws)
            ws_l.append(ws)
        vnb_l = [(sol_l[ix[hh]][:, :GDN_DV] - ws_l[hh][:LANE]).astype(BF16) for hh in heads]
        o_l = [ws_l[hh][LANE:] + _dot(qk_l[ix[hh]].astype(BF16), vnb_l[hh]) for hh in heads]
        for blk in range(nblk):
            sidx = 0 if carry else blk
            new = []
            for hh in heads:
                kd = kd_l[ix[hh]]
                gb = gb_l[ix[hh]]
                kdb = kd if nblk == 1 else jnp.where((row // chunk) == blk, kd, 0.0)
                gtot = jnp.exp(gb[(blk + 1) * chunk - 1:(blk + 1) * chunk, :])
                new.append(state[sidx][hh] * gtot + _dot_tn(kdb.astype(BF16), vnb_l[hh]))
            state[sidx] = new
        for hh in heads:
            rows = slice(gi * LANE, (gi + 1) * LANE)
            lanes = slice(hh * LANE, (hh + 1) * LANE)
            z = z_ref[rows, lanes]
            o_ref[rows, lanes] = _rms(o_l[hh], ng_ref[...]) * (z * _sigmoid(z))
    for sidx in range(nstate):
        for hh in heads:
            s_scr[sidx, hh] = state[sidx][hh]

    def fin():
        for s in range(nseg):
            sout_ref[s] = s_scr[s]

    if carry:
        pl.when(i == pl.num_programs(0) - 1)(fin)
    else:
        fin()


def _gdn(proj, hist, s0, conv_w, alog_row, dt_row, norm_g, t_rows, tt, seg, chunk, carry):
    nt = t_rows // tt
    nseg = tt // seg
    nseq = s0.shape[0]
    if carry:
        seq_idx = lambda i: 0
    else:
        seq_idx = lambda i: i

    def col(c0):
        return pl.BlockSpec((tt, GDN_QK), lambda i: (i, c0 // GDN_HEADS))

    row_spec = pl.BlockSpec((1, LANE), lambda i: (0, 0))
    sspec = pl.BlockSpec((nseg, GDN_HEADS, GDN_DK, GDN_DV), lambda i: (seq_idx(i), 0, 0, 0))
    return pl.pallas_call(
        functools.partial(_gdn_kernel, tt=tt, seg=seg, chunk=chunk, carry=carry),
        grid=(nt,),
        in_specs=[
            col(COL_QB), col(COL_KB), col(COL_VB), col(COL_ZB),
            pl.BlockSpec((tt, LANE), lambda i: (i, COL_AB)),
            pl.BlockSpec((nseg, CONV_W - 1, 3 * GDN_QK), lambda i: (seq_idx(i), 0, 0)),
            pl.BlockSpec((CONV_W, 3 * GDN_QK), lambda i: (0, 0)),
            row_spec, row_spec, row_spec,
            sspec,
        ],
        out_specs=[
            pl.BlockSpec((tt, GDN_V), lambda i: (i, 0)),
            sspec,
        ],
        out_shape=[
            jax.ShapeDtypeStruct((t_rows, GDN_V), F32),
            jax.ShapeDtypeStruct((nseq, GDN_HEADS, GDN_DK, GDN_DV), F32),
        ],
        scratch_shapes=[
            pltpu.VMEM((nseg, GDN_HEADS, GDN_DK, GDN_DV), F32),
            pltpu.VMEM((nseg, 8, 3 * GDN_QK), F32),
            pltpu.VMEM((8 + seg, 3 * GDN_QK), F32),
            pltpu.VMEM((tt, 3 * GDN_QK), F32),
        ],
        compiler_params=pltpu.CompilerParams(dimension_semantics=("arbitrary",), vmem_limit_bytes=VMEM_LIMIT),
        name="gdn",
    )(proj, proj, proj, proj, proj, hist, conv_w, alog_row, dt_row, norm_g, s0)


def _mem_attn_kernel(q_ref, mk_ref, mv_ref, o_ref):
    scale = MEM_HEAD_DIM ** -0.5
    for hh in range(MEM_HEADS):
        sl = slice(hh * MEM_HEAD_DIM, (hh + 1) * MEM_HEAD_DIM)
        q = (q_ref[:, sl] * scale).astype(BF16)
        s = _dot_nt(q, mk_ref[:, sl].astype(BF16))
        m = jnp.max(s, axis=-1, keepdims=True)
        e = jnp.exp(s - m)
        p = e * (1.0 / jnp.sum(e, axis=-1, keepdims=True))
        o_ref[:, sl] = _dot(p.astype(BF16), mv_ref[:, sl].astype(BF16))


def _mem_attn(proj, mk, mv, nb, t, tt, batched):
    nt = t // tt
    if batched:
        kspec = pl.BlockSpec((None, N_MEM, MEM_Q), lambda b, i: (b, 0, 0))
        vspec = kspec
    else:
        kspec = pl.BlockSpec((N_MEM, MEM_Q), lambda b, i: (0, 0))
        vspec = pl.BlockSpec((N_MEM, MEM_Q), lambda b, i: (0, 1))
    return pl.pallas_call(
        _mem_attn_kernel,
        grid=(nb, nt),
        in_specs=[pl.BlockSpec((tt, MEM_Q), lambda b, i: (b * nt + i, COL_QC // 8)), kspec, vspec],
        out_specs=pl.BlockSpec((tt, MEM_Q), lambda b, i: (b * nt + i, 0)),
        out_shape=jax.ShapeDtypeStruct((nb * t, MEM_Q), F32),
        compiler_params=pltpu.CompilerParams(dimension_semantics=("parallel", "arbitrary"),
                                             vmem_limit_bytes=VMEM_LIMIT),
        name="mem_attn",
    )(proj, mk, mv)


def _merge_kernel(x_ref, oa_ref, ob_ref, oc_ref, g0_ref, g1_ref, g2_ref, wb_ref, wo_ref, gp_ref, o_ref):
    merged = None
    for n, (br_ref, gl_ref) in enumerate(((oa_ref, g0_ref), (ob_ref, g1_ref), (oc_ref, g2_ref))):
        br = _dot(br_ref[...].astype(BF16), wb_ref[n])
        term = _sigmoid(gl_ref[...]) * br
        merged = term if merged is None else merged + term
    y = _dot(merged.astype(BF16), wo_ref[...])
    o_ref[...] = x_ref[...] + _rms(y, gp_ref[...])


def _merge(x, oa, ob, oc, proj, wb, wo, gp, tm):
    m = x.shape[0]
    tile = pl.BlockSpec((tm, D_MODEL), lambda i: (i, 0))

    def gspec(n):
        return pl.BlockSpec((tm, D_MODEL), lambda i: (i, COL_GL // 8 + n))

    return pl.pallas_call(
        _merge_kernel,
        grid=(m // tm,),
        in_specs=[
            tile, tile, tile, tile, gspec(0), gspec(1), gspec(2),
            pl.BlockSpec((N_BRANCH, D_MODEL, D_MODEL), lambda i: (0, 0, 0)),
            pl.BlockSpec((D_MODEL, D_MODEL), lambda i: (0, 0)),
            pl.BlockSpec((1, D_MODEL), lambda i: (0, 0)),
        ],
        out_specs=tile,
        out_shape=jax.ShapeDtypeStruct((m, D_MODEL), F32),
        compiler_params=pltpu.CompilerParams(dimension_semantics=("parallel",), vmem_limit_bytes=VMEM_LIMIT),
        name="merge",
    )(x, oa, ob, oc, proj, proj, proj, wb, wo, gp)


FF_CHUNK = 1024


def _ffn_kernel(x_ref, g1_ref, wu_ref, wd_ref, g2_ref, o_ref):
    x = x_ref[...]
    hf = _rms(x, g1_ref[...]).astype(BF16)
    f = None
    for c in range(D_FF // FF_CHUNK):
        sl = slice(c * FF_CHUNK, (c + 1) * FF_CHUNK)
        up = jnp.maximum(_dot(hf, wu_ref[:, sl]), 0.0)
        part = _dot((up * up).astype(BF16), wd_ref[sl, :])
        f = part if f is None else f + part
    o_ref[...] = x + _rms(f, g2_ref[...])


def _ffn(x, g1, wu, wd, g2, tm):
    m = x.shape[0]
    tile = pl.BlockSpec((tm, D_MODEL), lambda i: (i, 0))
    vec = pl.BlockSpec((1, D_MODEL), lambda i: (0, 0))
    return pl.pallas_call(
        _ffn_kernel,
        grid=(m // tm,),
        in_specs=[tile, vec, pl.BlockSpec((D_MODEL, D_FF), lambda i: (0, 0)),
                  pl.BlockSpec((D_FF, D_MODEL), lambda i: (0, 0)), vec],
        out_specs=tile,
        out_shape=jax.ShapeDtypeStruct((m, D_MODEL), F32),
        compiler_params=pltpu.CompilerParams(dimension_semantics=("parallel",), vmem_limit_bytes=VMEM_LIMIT),
        name="ffn",
    )(x, g1, wu, wd, g2)


def _regroup_cols(w):
    o_ka = SWA_Q
    o_va = o_ka + SWA_KV
    o_qb = o_va + SWA_KV
    o_ab = o_qb + 2 * GDN_QK + 2 * GDN_V
    o_qc = o_ab + 2 * GDN_HEADS
    o_gl = o_qc + MEM_Q
    o_end = o_gl + N_BRANCH * D_MODEL
    parts = [w[..., :o_ka], w[..., o_qb:o_ab], w[..., o_qc:o_end], w[..., o_ka:o_qb], w[..., o_ab:o_qc]]
    out = jnp.concatenate(parts, axis=-1)
    pad = N_PROJ - out.shape[-1]
    return jnp.pad(out, [(0, 0)] * (out.ndim - 1) + [(0, pad)])


def _rope_tables(pos):
    half = SWA_HEAD_DIM // 2
    inv = ROPE_THETA ** (-jnp.arange(half, dtype=F32) / half)
    ang = pos.astype(F32)[:, None] * inv[None, :]
    cos = jnp.cos(ang)
    sin = jnp.sin(ang)
    cos_t = jnp.concatenate([cos, cos, cos, cos], axis=1)
    sin_t = jnp.concatenate([-sin, sin, -sin, sin], axis=1)
    return cos_t, sin_t


def _pad_row(v, offset=0):
    return jnp.zeros((1, LANE), F32).at[0, offset:offset + v.shape[0]].set(v)


def _layer(x, nb, t, pos, k_hist, v_hist, s0, conv_hist, mk, mv, wts, cfg):
    proj = _in_proj(x, wts['g_pre_mix'], wts['w_in'], wts['b_in'], cfg['tm_proj'], cfg['tn_proj'])
    cos, sin = _rope_tables(pos)
    oa, k_rot = _swa(proj, k_hist, v_hist, cos, sin, wts['sinks'], nb, t, cfg['tq'], cfg['cache'])
    ob, s_new = _gdn(proj, conv_hist, s0, wts['conv_w'], wts['alog'], wts['dt'], wts['norm_g'],
                     nb * t, cfg['tt_gdn'], cfg['seg'], cfg['chunk'], cfg['carry'])
    oc = _mem_attn(proj, mk, mv, nb, t, cfg['tt_mem'], cfg['cache'])
    x1 = _merge(x, oa, ob, oc, proj, wts['w_branch'], wts['w_out'], wts['g_post_mix'], cfg['tm'])
    y = _ffn(x1, wts['g_pre_ffn'], wts['w_up'], wts['w_down'], wts['g_post_ffn'], cfg['tm'])
    return y, k_rot, s_new, proj


def kernel(x_prompt, x_sample, mem_prompt, cache_swa_k, cache_swa_v, state_gdn, state_conv, cache_mem_k, cache_mem_v, g_pre_mix, w_in, b_in, swa_sinks, conv_w, gdn_a_log, gdn_dt_bias, gdn_norm_g, g_mem, w_mem_kv, w_branch, w_out, g_post_mix, g_pre_ffn, w_up, w_down, g_post_ffn):
    depth = w_in.shape[0]
    assert depth == 1
    l = 0
    bp, tp, _ = x_prompt.shape
    bs, ts, _ = x_sample.shape
    assert bp == 1

    wts = {
        'g_pre_mix': g_pre_mix[l][None, :],
        'w_in': _regroup_cols(w_in[l]).astype(BF16),
        'b_in': _regroup_cols(b_in[l])[None, :],
        'sinks': swa_sinks[l],
        'conv_w': conv_w[l],
        'alog': _pad_row(gdn_a_log[l]),
        'dt': _pad_row(gdn_dt_bias[l]),
        'norm_g': gdn_norm_g[l][None, :],
        'w_branch': w_branch[l].astype(BF16),
        'w_out': w_out[l].astype(BF16),
        'g_post_mix': g_post_mix[l][None, :],
        'g_pre_ffn': g_pre_ffn[l][None, :],
        'w_up': w_up[l].astype(BF16),
        'w_down': w_down[l].astype(BF16),
        'g_post_ffn': g_post_ffn[l][None, :],
    }

    mkv = _mem_kv(mem_prompt[0], g_mem[l][None, :], w_mem_kv[l].astype(BF16))
    cfg_p = dict(tm_proj=512, tn_proj=1920, tq=256, cache=False, tt_gdn=256, seg=256, chunk=128, carry=True,
                 tt_mem=512, tm=256)
    zero_hist = jnp.zeros((1, CONV_W - 1, 3 * GDN_QK), F32)
    zero_s = jnp.zeros((1, GDN_HEADS, GDN_DK, GDN_DV), F32)
    yp, kp_rot, sp_new, proj_p = _layer(
        x_prompt.reshape(bp * tp, D_MODEL), bp, tp, jnp.arange(tp, dtype=jnp.int32), None, None,
        zero_s, zero_hist, mkv, mkv, wts, cfg_p)

    cfg_s = dict(tm_proj=512, tn_proj=1920, tq=ts, cache=True, tt_gdn=128, seg=ts, chunk=ts, carry=False,
                 tt_mem=ts, tm=256)
    ys, ks_rot, ss_new, proj_s = _layer(
        x_sample.reshape(bs * ts, D_MODEL), bs, ts, PAST_LEN + jnp.arange(ts, dtype=jnp.int32),
        cache_swa_k[l].reshape(bs, WINDOW, SWA_KV), cache_swa_v[l].reshape(bs, WINDOW, SWA_KV),
        state_gdn[l], state_conv[l],
        cache_mem_k[l].reshape(bs, N_MEM, MEM_Q), cache_mem_v[l].reshape(bs, N_MEM, MEM_Q), wts, cfg_s)

    kv_shape = (SWA_KV_HEADS, SWA_HEAD_DIM)
    conv_cols = slice(COL_QB * LANE, (COL_ZB) * LANE)
    va_cols = slice(COL_VA * LANE, (COL_VA + 1) * LANE)

    p_k = kp_rot[tp - WINDOW:].reshape((1, bp, WINDOW) + kv_shape)
    p_v = proj_p[tp - WINDOW:, va_cols].reshape((1, bp, WINDOW) + kv_shape)
    p_c = proj_p[tp - (CONV_W - 1):, conv_cols].reshape(1, bp, CONV_W - 1, 3 * GDN_QK)
    p_mk = mkv[:, :MEM_Q].reshape(1, bp, N_MEM, MEM_HEADS, MEM_HEAD_DIM)
    p_mv = mkv[:, MEM_Q:].reshape(1, bp, N_MEM, MEM_HEADS, MEM_HEAD_DIM)

    ks_new = ks_rot.reshape(bs, ts, SWA_KV)
    vs_new = proj_s[:, va_cols].reshape(bs, ts, SWA_KV)
    s_k = jnp.concatenate([cache_swa_k[l].reshape(bs, WINDOW, SWA_KV), ks_new], axis=1)[:, -WINDOW:]
    s_v = jnp.concatenate([cache_swa_v[l].reshape(bs, WINDOW, SWA_KV), vs_new], axis=1)[:, -WINDOW:]
    s_c = proj_s[:, conv_cols].reshape(bs, ts, 3 * GDN_QK)[:, ts - (CONV_W - 1):]

    return (yp.reshape(bp, tp, D_MODEL), ys.reshape(bs, ts, D_MODEL),
            p_k, p_v, sp_new[None], p_c, p_mk, p_mv,
            s_k.reshape((1, bs, WINDOW) + kv_shape), s_v.reshape((1, bs, WINDOW) + kv_shape),
            ss_new[None], s_c[None])
```

```python
import functools

import jax
import jax.numpy as jnp
from jax import lax
from jax.experimental import pallas as pl
from jax.experimental.pallas import tpu as pltpu

F32 = jnp.float32
BF16 = jnp.bfloat16

D_MODEL = 1024
PAST_LEN = 2048
CHUNK = 64
N_MEM = 256
EPS = 1e-6
SWA_HEADS = 16
SWA_KV_HEADS = 2
SWA_HEAD_DIM = 64
WINDOW = 128
ROPE_THETA = 10000.0
GDN_HEADS = 8
GDN_DK = 128
GDN_DV = 128
CONV_W = 4
MEM_HEADS = 4
MEM_HEAD_DIM = 256
N_BRANCH = 3
D_FF = 4 * D_MODEL

LANE = 128
SWA_Q = SWA_HEADS * SWA_HEAD_DIM
SWA_KV = SWA_KV_HEADS * SWA_HEAD_DIM
GDN_QK = GDN_HEADS * GDN_DK
GDN_V = GDN_HEADS * GDN_DV
MEM_Q = MEM_HEADS * MEM_HEAD_DIM

COL_QA = 0
COL_QB = 8
COL_KB = 16
COL_VB = 24
COL_ZB = 32
COL_QC = 40
COL_GL = 48
COL_KA = 72
COL_VA = 73
COL_AB = 74
N_PROJ = 76 * LANE

VMEM_LIMIT = 48 * 1024 * 1024


def _dot(a, b):
    return jnp.dot(a, b, preferred_element_type=F32)


def _dot_nt(a, b):
    return lax.dot_general(a, b, (((1,), (1,)), ((), ())), preferred_element_type=F32)


def _dot_tn(a, b):
    return lax.dot_general(a, b, (((0,), (0,)), ((), ())), preferred_element_type=F32)


def _rms(x, g):
    return x * lax.rsqrt(jnp.mean(x * x, axis=-1, keepdims=True) + EPS) * g


def _sigmoid(x):
    return 1.0 / (1.0 + jnp.exp(-x))


def _resident(shape):
    return pl.BlockSpec(shape, lambda *_: (0,) * len(shape), pipeline_mode=pl.Buffered(1))


def _split3(x):
    hi = x.astype(BF16)
    r = x - hi.astype(F32)
    mid = r.astype(BF16)
    lo = (r - mid.astype(F32)).astype(BF16)
    return hi, mid, lo


def _mem_kv_kernel(m_ref, g_ref, w_ref, o_ref):
    h = _rms(m_ref[...], g_ref[...]).astype(BF16)
    o_ref[...] = _dot(h, w_ref[...])


def _mem_kv(mem, g_mem, w_bf16):
    n = w_bf16.shape[1]
    tn = 1024
    return pl.pallas_call(
        _mem_kv_kernel,
        grid=(n // tn,),
        in_specs=[
            pl.BlockSpec((N_MEM, D_MODEL), lambda j: (0, 0)),
            pl.BlockSpec((1, D_MODEL), lambda j: (0, 0)),
            pl.BlockSpec((D_MODEL, tn), lambda j: (0, j)),
        ],
        out_specs=pl.BlockSpec((N_MEM, tn), lambda j: (0, j)),
        out_shape=jax.ShapeDtypeStruct((N_MEM, n), F32),
        compiler_params=pltpu.CompilerParams(dimension_semantics=("arbitrary",), vmem_limit_bytes=VMEM_LIMIT),
        name="mem_kv",
    )(mem, g_mem, w_bf16)


def _in_proj_kernel(x_ref, g_ref, w_ref, b_ref, o_ref, h_ref):
    @pl.when(pl.program_id(1) == 0)
    def _():
        h_ref[...] = _rms(x_ref[...], g_ref[...]).astype(BF16)

    o_ref[...] = _dot(h_ref[...], w_ref[...]) + b_ref[...]


def _in_proj(x, g, w_bf16, b, tm, tn):
    m = x.shape[0]
    return pl.pallas_call(
        _in_proj_kernel,
        grid=(m // tm, N_PROJ // tn),
        in_specs=[
            pl.BlockSpec((tm, D_MODEL), lambda i, j: (i, 0)),
            pl.BlockSpec((1, D_MODEL), lambda i, j: (0, 0)),
            pl.BlockSpec((D_MODEL, tn), lambda i, j: (0, j)),
            pl.BlockSpec((1, tn), lambda i, j: (0, j)),
        ],
        out_specs=pl.BlockSpec((tm, tn), lambda i, j: (i, j)),
        out_shape=jax.ShapeDtypeStruct((m, N_PROJ), F32),
        scratch_shapes=[pltpu.VMEM((tm, D_MODEL), BF16)],
        compiler_params=pltpu.CompilerParams(dimension_semantics=("parallel", "arbitrary"),
                                             vmem_limit_bytes=VMEM_LIMIT),
        name="in_proj",
    )(x, g, w_bf16, b)


def _rope(x, cos, sin_signed):
    lane = lax.broadcasted_iota(jnp.int32, x.shape, 1)
    first = (lane % SWA_HEAD_DIM) < (SWA_HEAD_DIM // 2)
    rot = jnp.where(first, pltpu.roll(x, LANE - SWA_HEAD_DIM // 2, 1), pltpu.roll(x, SWA_HEAD_DIM // 2, 1))
    return x * cos + rot * sin_signed


KEYS = WINDOW + CHUNK
KPAD = 256


def _swa_kernel(sink_ref, q_ref, k_ref, v_ref, kh_ref, vh_ref, cos_ref, sin_ref, cosh_ref, sinh_ref,
                o_ref, kout_ref, qs_ref, ka_ref, kb_ref, va_ref, vb_ref, *, tq, cache):
    i = pl.program_id(1)
    cos = cos_ref[...]
    sin = sin_ref[...]
    kr = _rope(k_ref[...], cos, sin)
    kout_ref[...] = kr
    if cache:
        khr = kh_ref[...]
    else:
        khr = _rope(kh_ref[...], cosh_ref[...], sinh_ref[...])
    zpad = jnp.zeros((KPAD - KEYS, LANE), F32)
    kall = jnp.concatenate([khr, kr, zpad], axis=0)
    vall = jnp.concatenate([vh_ref[...], v_ref[...], zpad], axis=0)
    lo = lax.broadcasted_iota(jnp.int32, kall.shape, 1) < SWA_HEAD_DIM
    zero = jnp.zeros_like(kall)
    for src, a_ref, b_ref in ((kall, ka_ref, kb_ref), (vall, va_ref, vb_ref)):
        rolled = pltpu.roll(src, SWA_HEAD_DIM, 1)
        a_ref[0] = jnp.where(lo, src, zero).astype(BF16)
        b_ref[0] = jnp.where(lo, zero, rolled).astype(BF16)
        a_ref[1] = jnp.where(lo, rolled, zero).astype(BF16)
        b_ref[1] = jnp.where(lo, zero, src).astype(BF16)

    scale = SWA_HEAD_DIM ** -0.5
    for p in range(SWA_HEADS // 2):
        sl = slice(p * LANE, (p + 1) * LANE)
        qs_ref[:, sl] = (_rope(q_ref[:, sl], cos, sin) * scale).astype(BF16)

    jj = lax.broadcasted_iota(jnp.int32, (CHUNK, KPAD), 1)
    pairs_per_kv = SWA_HEADS // SWA_KV_HEADS // 2
    inst = [(c, g) for c in range(tq // CHUNK) for g in range(SWA_KV_HEADS)]

    def scores(c, g):
        rows = slice(c * CHUNK, (c + 1) * CHUNK)
        krows = slice(c * CHUNK, c * CHUNK + KPAD)
        kbd = jnp.concatenate([ka_ref[g, krows, :], kb_ref[g, krows, :]], axis=0)
        qst = jnp.concatenate(
            [qs_ref[rows, (g * pairs_per_kv + j) * LANE:(g * pairs_per_kv + j + 1) * LANE]
             for j in range(pairs_per_kv)], axis=0)
        return _dot_nt(qst, kbd)

    def probs(c, g, s):
        if cache:
            valid = jj < KEYS
        else:
            valid = (jj < KEYS) & ((i > 0) | (jj + c * CHUNK >= WINDOW))
        pblocks = []
        for j in range(pairs_per_kv):
            halves = []
            for half in range(2):
                sh = s[j * CHUNK:(j + 1) * CHUNK, half * KPAD:(half + 1) * KPAD]
                sh = jnp.where(valid, sh, -jnp.inf)
                sink = sink_ref[2 * (g * pairs_per_kv + j) + half]
                m = jnp.maximum(jnp.max(sh, axis=-1, keepdims=True), sink)
                e = jnp.exp(sh - m)
                den = jnp.sum(e, axis=-1, keepdims=True) + jnp.exp(sink - m)
                halves.append((e * (1.0 / den)).astype(BF16))
            pblocks.append(jnp.concatenate(halves, axis=1))
        return jnp.concatenate(pblocks, axis=0)

    def outputs(c, g, pmat):
        rows = slice(c * CHUNK, (c + 1) * CHUNK)
        krows = slice(c * CHUNK, c * CHUNK + KPAD)
        vbd = jnp.concatenate([va_ref[g, krows, :], vb_ref[g, krows, :]], axis=0)
        o = _dot(pmat, vbd)
        for j in range(pairs_per_kv):
            p = g * pairs_per_kv + j
            o_ref[rows, p * LANE:(p + 1) * LANE] = o[j * CHUNK:(j + 1) * CHUNK].astype(BF16)

    s_l, p_l = {}, {}
    for step in range(len(inst) + 2):
        if step < len(inst):
            s_l[step] = scores(*inst[step])
        if 0 <= step - 1 < len(inst):
            p_l[step - 1] = probs(*inst[step - 1], s_l.pop(step - 1))
        if 0 <= step - 2 < len(inst):
            outputs(*inst[step - 2], p_l.pop(step - 2))


def _swa(proj, k_hist, v_hist, cos, sin, sinks, nb, t, tq, cache):
    nt = t // tq
    kv_rows = WINDOW + tq + (KPAD - KEYS)
    if cache:
        hist_k_spec = pl.BlockSpec((None, WINDOW, LANE), lambda b, i: (b, 0, 0))
        hist_v_spec = pl.BlockSpec((None, WINDOW, LANE), lambda b, i: (b, 0, 0))
        hk, hv = k_hist, v_hist
        tab_spec_h = pl.BlockSpec((WINDOW, LANE), lambda b, i: (0, 0))
        cos_h = jnp.zeros((WINDOW, LANE), F32)
        sin_h = cos_h
    else:
        r = tq // WINDOW

        def hidx(b, i):
            return jnp.maximum((b * nt + i) * r - 1, 0)

        hist_k_spec = pl.BlockSpec((WINDOW, LANE), lambda b, i: (hidx(b, i), COL_KA))
        hist_v_spec = pl.BlockSpec((WINDOW, LANE), lambda b, i: (hidx(b, i), COL_VA))
        hk, hv = proj, proj
        tab_spec_h = pl.BlockSpec((WINDOW, LANE), lambda b, i: (jnp.maximum(i * r - 1, 0), 0))
        cos_h, sin_h = cos, sin
    return pl.pallas_call(
        functools.partial(_swa_kernel, tq=tq, cache=cache),
        grid=(nb, nt),
        in_specs=[
            pl.BlockSpec(memory_space=pltpu.SMEM),
            pl.BlockSpec((tq, SWA_Q), lambda b, i: (b * nt + i, COL_QA // 8)),
            pl.BlockSpec((tq, LANE), lambda b, i: (b * nt + i, COL_KA)),
            pl.BlockSpec((tq, LANE), lambda b, i: (b * nt + i, COL_VA)),
            hist_k_spec,
            hist_v_spec,
            pl.BlockSpec((tq, LANE), lambda b, i: (i, 0)),
            pl.BlockSpec((tq, LANE), lambda b, i: (i, 0)),
            tab_spec_h,
            tab_spec_h,
        ],
        out_specs=[
            pl.BlockSpec((tq, SWA_Q), lambda b, i: (b * nt + i, 0)),
            pl.BlockSpec((tq, LANE), lambda b, i: (b * nt + i, 0)),
        ],
        out_shape=[
            jax.ShapeDtypeStruct((nb * t, SWA_Q), BF16),
            jax.ShapeDtypeStruct((nb * t, LANE), F32),
        ],
        scratch_shapes=[
            pltpu.VMEM((tq, SWA_Q), BF16),
            pltpu.VMEM((SWA_KV_HEADS, kv_rows, LANE), BF16),
            pltpu.VMEM((SWA_KV_HEADS, kv_rows, LANE), BF16),
            pltpu.VMEM((SWA_KV_HEADS, kv_rows, LANE), BF16),
            pltpu.VMEM((SWA_KV_HEADS, kv_rows, LANE), BF16),
        ],
        compiler_params=pltpu.CompilerParams(dimension_semantics=("parallel", "arbitrary"),
                                             vmem_limit_bytes=VMEM_LIMIT),
        name="swa",
    )(sinks, proj, proj, proj, hk, hv, cos, sin, cos_h, sin_h)


INV_BASE = 16


def _inverse_minus_identity(pms, row, col, chunk):
    base = (row // INV_BASE) == (col // INV_BASE)
    rs = [jnp.where(base, pm, 0.0) for pm in pms]
    ms = rs
    for _ in range(INV_BASE.bit_length() - 2):
        mbs = [m.astype(BF16) for m in ms]
        ms = [_dot(mb, mb) for mb in mbs]
        prods = [_dot(m.astype(BF16), r.astype(BF16)) for m, r in zip(ms, rs)]
        rs = [r + m + p for r, m, p in zip(rs, ms, prods)]
    bs = INV_BASE
    while bs < chunk:
        inner = (row // bs) == (col // bs)
        outer = (row // (2 * bs)) == (col // (2 * bs))
        sib = outer & jnp.logical_not(inner)
        offs = [jnp.where(sib, pm, 0.0) for pm in pms]
        xs = [off + _dot(off.astype(BF16), r.astype(BF16)) for off, r in zip(offs, rs)]
        prods = [_dot(r.astype(BF16), x.astype(BF16)) for r, x in zip(rs, xs)]
        rs = [r + x + p for r, x, p in zip(rs, xs, prods)]
        bs *= 2
    return rs


def _gdn_kernel(q_ref, k_ref, v_ref, z_ref, ab_ref, hist_ref, cw_ref, alog_ref, dt_ref, ng_ref, s0_ref,
                o_ref, sout_ref, s_scr, carry_scr, ext_scr, conv_scr, *, tt, seg, chunk, carry):
    i = pl.program_id(0)
    nseg = tt // seg
    nblk = LANE // chunk
    ngroups = tt // LANE
    hist_rows = CONV_W - 1

    def init():
        for s in range(nseg):
            s_scr[s] = s0_ref[s]
            carry_scr[s] = jnp.zeros((8, 3 * GDN_QK), F32)
            carry_scr[s, 8 - hist_rows:8, :] = hist_ref[s]

    if carry:
        pl.when(i == 0)(init)
    else:
        init()

    for s in range(nseg):
        ext_scr[0:8, :] = carry_scr[s]
        for cg, x_ref in enumerate((q_ref, k_ref, v_ref)):
            ext_scr[8:8 + seg, cg * GDN_QK:(cg + 1) * GDN_QK] = x_ref[s * seg:(s + 1) * seg, :]
        if carry:
            carry_scr[s] = ext_scr[seg:seg + 8, :]
        base = 8 - hist_rows
        for cb in range(3 * GDN_HEADS):
            lanes = slice(cb * LANE, (cb + 1) * LANE)
            y = ext_scr[base:base + seg, lanes] * cw_ref[0:1, lanes]
            for tap in range(1, CONV_W):
                y = y + ext_scr[base + tap:base + tap + seg, lanes] * cw_ref[tap:tap + 1, lanes]
            conv_scr[s * seg:(s + 1) * seg, lanes] = y * _sigmoid(y)

    row = lax.broadcasted_iota(jnp.int32, (LANE, LANE), 0)
    col = lax.broadcasted_iota(jnp.int32, (LANE, LANE), 1)
    same = (row // chunk) == (col // chunk)
    incl = same & (row >= col)
    strict = same & (row > col)
    lmask = jnp.where(incl, 1.0, 0.0).astype(BF16)
    row2 = jnp.concatenate([row, row], axis=0)

    inst = [(gi, hh) for gi in range(ngroups) for hh in range(GDN_HEADS)]
    heads = range(GDN_HEADS)

    gates = []
    for gi in range(ngroups):
        ab = ab_ref[gi * LANE:(gi + 1) * LANE, :]
        xa = ab + dt_ref[...]
        softplus = jnp.maximum(xa, 0.0) + jnp.log1p(jnp.exp(-jnp.abs(xa)))
        la = -jnp.exp(alog_ref[...]) * softplus
        beta = _sigmoid(ab)
        g_all = sum(_dot(lmask, piece) for piece in _split3(la))
        gates.append((g_all, g_all.T, beta))

    qn_l, kn_l, v_l, gb_l, bb_l, decay_l = [], [], [], [], [], []
    for gi, hh in inst:
        rows = slice(gi * LANE, (gi + 1) * LANE)
        q = conv_scr[rows, hh * LANE:(hh + 1) * LANE]
        k = conv_scr[rows, GDN_QK + hh * LANE:GDN_QK + (hh + 1) * LANE]
        v_l.append(conv_scr[rows, 2 * GDN_QK + hh * LANE:2 * GDN_QK + (hh + 1) * LANE])
        qn_l.append(q * lax.rsqrt(jnp.sum(q * q, axis=-1, keepdims=True) + EPS) * (GDN_DK ** -0.5))
        kn_l.append(k * lax.rsqrt(jnp.sum(k * k, axis=-1, keepdims=True) + EPS))
        g_all, g_all_t, beta = gates[gi]
        gb = jnp.broadcast_to(g_all[:, hh:hh + 1], (LANE, LANE))
        grow = jnp.broadcast_to(g_all_t[hh:hh + 1, :], (LANE, LANE))
        gb_l.append(gb)
        bb_l.append(jnp.broadcast_to(beta[:, GDN_HEADS + hh:GDN_HEADS + hh + 1], (LANE, LANE)))
        decay_l.append(jnp.exp(jnp.where(incl, gb - grow, -jnp.inf)))

    knb_l = [kn.astype(BF16) for kn in kn_l]
    cat_l = [_dot_nt(jnp.concatenate([qn.astype(BF16), knb], axis=0), knb)
             for qn, knb in zip(qn_l, knb_l)]
    qk_l = [cat[:LANE] * decay for cat, decay in zip(cat_l, decay_l)]
    pm_l = [jnp.where(strict, -(bb * cat[LANE:] * decay), 0.0)
            for bb, cat, decay in zip(bb_l, cat_l, decay_l)]
    r_l = _inverse_minus_identity(pm_l, row, col, chunk)
    eg_l = [jnp.exp(gb) for gb in gb_l]
    rhs_l = [jnp.concatenate([v * bb, kn * (bb * eg)], axis=1)
             for v, bb, kn, eg in zip(v_l, bb_l, kn_l, eg_l)]
    sol_l = [rhs + _dot(r.astype(BF16), rhs.astype(BF16)) for r, rhs in zip(r_l, rhs_l)]
    wq_l = [jnp.concatenate([sol[:, GDN_DV:], qn * eg], axis=0).astype(BF16)
            for sol, qn, eg in zip(sol_l, qn_l, eg_l)]
    kd_l = []
    for kn, gb in zip(kn_l, gb_l):
        glast = gb[chunk - 1:chunk, :]
        for blk in range(1, nblk):
            glast = jnp.where(row >= blk * chunk, gb[(blk + 1) * chunk - 1:(blk + 1) * chunk, :], glast)
        kd_l.append(kn * jnp.exp(glast - gb))

    nstate = 1 if carry else nblk
    state = [[s_scr[sidx, hh] for hh in heads] for sidx in range(nstate)]
    for gi in range(ngroups):
        ix = [gi * GDN_HEADS + hh for hh in heads]
        ws_l = []
        for hh in heads:
            ws = None
            for blk in range(nblk):
                part = _dot(wq_l[ix[hh]], state[0 if carry else blk][hh].astype(BF16))
                ws = part if ws is None else jnp.where(row2 >= blk * chunk, part, ws)
            ws_l.append(ws)
        vnb_l = [(sol_l[ix[hh]][:, :GDN_DV] - ws_l[hh][:LANE]).astype(BF16) for hh in heads]
        o_l = [ws_l[hh][LANE:] + _dot(qk_l[ix[hh]].astype(BF16), vnb_l[hh]) for hh in heads]
        for blk in range(nblk):
            sidx = 0 if carry else blk
            new = []
            for hh in heads:
                kd = kd_l[ix[hh]]
                gb = gb_l[ix[hh]]
                kdb = kd if nblk == 1 else jnp.where((row // chunk) == blk, kd, 0.0)
                gtot = jnp.exp(gb[(blk + 1) * chunk - 1:(blk + 1) * chunk, :])
                new.append(state[sidx][hh] * gtot + _dot_tn(kdb.astype(BF16), vnb_l[hh]))
            state[sidx] = new
        for hh in heads:
            rows = slice(gi * LANE, (gi + 1) * LANE)
            lanes = slice(hh * LANE, (hh + 1) * LANE)
            z = z_ref[rows, lanes]
            o_ref[rows, lanes] = (_rms(o_l[hh], ng_ref[...]) * (z * _sigmoid(z))).astype(BF16)
    for sidx in range(nstate):
        for hh in heads:
            s_scr[sidx, hh] = state[sidx][hh]

    def fin():
        for s in range(nseg):
            sout_ref[s] = s_scr[s]

    if carry:
        pl.when(i == pl.num_programs(0) - 1)(fin)
    else:
        fin()


def _gdn(proj, hist, s0, conv_w, alog_row, dt_row, norm_g, t_rows, tt, seg, chunk, carry):
    nt = t_rows // tt
    nseg = tt // seg
    nseq = s0.shape[0]
    if carry:
        seq_idx = lambda i: 0
    else:
        seq_idx = lambda i: i

    def col(c0):
        return pl.BlockSpec((tt, GDN_QK), lambda i: (i, c0 // GDN_HEADS))

    row_spec = pl.BlockSpec((1, LANE), lambda i: (0, 0))
    sspec = pl.BlockSpec((nseg, GDN_HEADS, GDN_DK, GDN_DV), lambda i: (seq_idx(i), 0, 0, 0))
    return pl.pallas_call(
        functools.partial(_gdn_kernel, tt=tt, seg=seg, chunk=chunk, carry=carry),
        grid=(nt,),
        in_specs=[
            col(COL_QB), col(COL_KB), col(COL_VB), col(COL_ZB),
            pl.BlockSpec((tt, LANE), lambda i: (i, COL_AB)),
            pl.BlockSpec((nseg, CONV_W - 1, 3 * GDN_QK), lambda i: (seq_idx(i), 0, 0)),
            pl.BlockSpec((CONV_W, 3 * GDN_QK), lambda i: (0, 0)),
            row_spec, row_spec, row_spec,
            sspec,
        ],
        out_specs=[
            pl.BlockSpec((tt, GDN_V), lambda i: (i, 0)),
            sspec,
        ],
        out_shape=[
            jax.ShapeDtypeStruct((t_rows, GDN_V), BF16),
            jax.ShapeDtypeStruct((nseq, GDN_HEADS, GDN_DK, GDN_DV), F32),
        ],
        scratch_shapes=[
            pltpu.VMEM((nseg, GDN_HEADS, GDN_DK, GDN_DV), F32),
            pltpu.VMEM((nseg, 8, 3 * GDN_QK), F32),
            pltpu.VMEM((8 + seg, 3 * GDN_QK), F32),
            pltpu.VMEM((tt, 3 * GDN_QK), F32),
        ],
        compiler_params=pltpu.CompilerParams(dimension_semantics=("arbitrary",), vmem_limit_bytes=VMEM_LIMIT),
        name="gdn",
    )(proj, proj, proj, proj, proj, hist, conv_w, alog_row, dt_row, norm_g, s0)


def _mem_attn_kernel(q_ref, mk_ref, mv_ref, o_ref):
    scale = MEM_HEAD_DIM ** -0.5
    for hh in range(MEM_HEADS):
        sl = slice(hh * MEM_HEAD_DIM, (hh + 1) * MEM_HEAD_DIM)
        q = (q_ref[:, sl] * scale).astype(BF16)
        s = _dot_nt(q, mk_ref[:, sl].astype(BF16))
        m = jnp.max(s, axis=-1, keepdims=True)
        e = jnp.exp(s - m)
        p = e * (1.0 / jnp.sum(e, axis=-1, keepdims=True))
        o_ref[:, sl] = _dot(p.astype(BF16), mv_ref[:, sl].astype(BF16)).astype(BF16)


def _mem_attn(proj, mk, mv, nb, t, tt, batched):
    nt = t // tt
    if batched:
        kspec = pl.BlockSpec((None, N_MEM, MEM_Q), lambda b, i: (b, 0, 0))
        vspec = kspec
    else:
        kspec = pl.BlockSpec((N_MEM, MEM_Q), lambda b, i: (0, 0))
        vspec = pl.BlockSpec((N_MEM, MEM_Q), lambda b, i: (0, 1))
    return pl.pallas_call(
        _mem_attn_kernel,
        grid=(nb, nt),
        in_specs=[pl.BlockSpec((tt, MEM_Q), lambda b, i: (b * nt + i, COL_QC // 8)), kspec, vspec],
        out_specs=pl.BlockSpec((tt, MEM_Q), lambda b, i: (b * nt + i, 0)),
        out_shape=jax.ShapeDtypeStruct((nb * t, MEM_Q), BF16),
        compiler_params=pltpu.CompilerParams(dimension_semantics=("parallel", "arbitrary"),
                                             vmem_limit_bytes=VMEM_LIMIT),
        name="mem_attn",
    )(proj, mk, mv)


def _merge_kernel(x_ref, oa_ref, ob_ref, oc_ref, g0_ref, g1_ref, g2_ref, wb_ref, wo_ref, gp_ref, o_ref):
    merged = None
    for n, (br_ref, gl_ref) in enumerate(((oa_ref, g0_ref), (ob_ref, g1_ref), (oc_ref, g2_ref))):
        br = _dot(br_ref[...], wb_ref[n])
        term = _sigmoid(gl_ref[...]) * br
        merged = term if merged is None else merged + term
    y = _dot(merged.astype(BF16), wo_ref[...])
    o_ref[...] = x_ref[...] + _rms(y, gp_ref[...])


def _merge(x, oa, ob, oc, proj, wb, wo, gp, tm):
    m = x.shape[0]
    tile = pl.BlockSpec((tm, D_MODEL), lambda i: (i, 0))

    def gspec(n):
        return pl.BlockSpec((tm, D_MODEL), lambda i: (i, COL_GL // 8 + n))

    return pl.pallas_call(
        _merge_kernel,
        grid=(m // tm,),
        in_specs=[
            tile, tile, tile, tile, gspec(0), gspec(1), gspec(2),
            _resident((N_BRANCH, D_MODEL, D_MODEL)),
            _resident((D_MODEL, D_MODEL)),
            pl.BlockSpec((1, D_MODEL), lambda i: (0, 0)),
        ],
        out_specs=tile,
        out_shape=jax.ShapeDtypeStruct((m, D_MODEL), F32),
        compiler_params=pltpu.CompilerParams(dimension_semantics=("parallel",), vmem_limit_bytes=VMEM_LIMIT),
        name="merge",
    )(x, oa, ob, oc, proj, proj, proj, wb, wo, gp)


FF_CHUNK = 1024


def _ffn_kernel(x_ref, g1_ref, wu_ref, wd_ref, g2_ref, o_ref):
    x = x_ref[...]
    hf = _rms(x, g1_ref[...]).astype(BF16)
    f = None
    for c in range(D_FF // FF_CHUNK):
        sl = slice(c * FF_CHUNK, (c + 1) * FF_CHUNK)
        up = jnp.maximum(_dot(hf, wu_ref[:, sl]), 0.0)
        part = _dot((up * up).astype(BF16), wd_ref[sl, :])
        f = part if f is None else f + part
    o_ref[...] = x + _rms(f, g2_ref[...])


def _ffn(x, g1, wu, wd, g2, tm):
    m = x.shape[0]
    tile = pl.BlockSpec((tm, D_MODEL), lambda i: (i, 0))
    vec = pl.BlockSpec((1, D_MODEL), lambda i: (0, 0))
    return pl.pallas_call(
        _ffn_kernel,
        grid=(m // tm,),
        in_specs=[tile, vec, _resident((D_MODEL, D_FF)), _resident((D_FF, D_MODEL)), vec],
        out_specs=tile,
        out_shape=jax.ShapeDtypeStruct((m, D_MODEL), F32),
        compiler_params=pltpu.CompilerParams(dimension_semantics=("parallel",), vmem_limit_bytes=VMEM_LIMIT),
        name="ffn",
    )(x, g1, wu, wd, g2)


def _regroup_cols(w):
    o_ka = SWA_Q
    o_va = o_ka + SWA_KV
    o_qb = o_va + SWA_KV
    o_ab = o_qb + 2 * GDN_QK + 2 * GDN_V
    o_qc = o_ab + 2 * GDN_HEADS
    o_gl = o_qc + MEM_Q
    o_end = o_gl + N_BRANCH * D_MODEL
    parts = [w[..., :o_ka], w[..., o_qb:o_ab], w[..., o_qc:o_end], w[..., o_ka:o_qb], w[..., o_ab:o_qc]]
    out = jnp.concatenate(parts, axis=-1)
    pad = N_PROJ - out.shape[-1]
    return jnp.pad(out, [(0, 0)] * (out.ndim - 1) + [(0, pad)])


def _rope_tables(pos):
    half = SWA_HEAD_DIM // 2
    inv = ROPE_THETA ** (-jnp.arange(half, dtype=F32) / half)
    ang = pos.astype(F32)[:, None] * inv[None, :]
    cos = jnp.cos(ang)
    sin = jnp.sin(ang)
    cos_t = jnp.concatenate([cos, cos, cos, cos], axis=1)
    sin_t = jnp.concatenate([-sin, sin, -sin, sin], axis=1)
    return cos_t, sin_t


def _pad_row(v, offset=0):
    return jnp.zeros((1, LANE), F32).at[0, offset:offset + v.shape[0]].set(v)


def _layer(x, nb, t, pos, k_hist, v_hist, s0, conv_hist, mk, mv, wts, cfg):
    proj = _in_proj(x, wts['g_pre_mix'], wts['w_in'], wts['b_in'], cfg['tm_proj'], cfg['tn_proj'])
    cos, sin = _rope_tables(pos)
    oa, k_rot = _swa(proj, k_hist, v_hist, cos, sin, wts['sinks'], nb, t, cfg['tq'], cfg['cache'])
    ob, s_new = _gdn(proj, conv_hist, s0, wts['conv_w'], wts['alog'], wts['dt'], wts['norm_g'],
                     nb * t, cfg['tt_gdn'], cfg['seg'], cfg['chunk'], cfg['carry'])
    oc = _mem_attn(proj, mk, mv, nb, t, cfg['tt_mem'], cfg['cache'])
    x1 = _merge(x, oa, ob, oc, proj, wts['w_branch'], wts['w_out'], wts['g_post_mix'], cfg['tm'])
    y = _ffn(x1, wts['g_pre_ffn'], wts['w_up'], wts['w_down'], wts['g_post_ffn'], cfg['tm'])
    return y, k_rot, s_new, proj


def kernel(x_prompt, x_sample, mem_prompt, cache_swa_k, cache_swa_v, state_gdn, state_conv, cache_mem_k, cache_mem_v, g_pre_mix, w_in, b_in, swa_sinks, conv_w, gdn_a_log, gdn_dt_bias, gdn_norm_g, g_mem, w_mem_kv, w_branch, w_out, g_post_mix, g_pre_ffn, w_up, w_down, g_post_ffn):
    depth = w_in.shape[0]
    assert depth == 1
    l = 0
    bp, tp, _ = x_prompt.shape
    bs, ts, _ = x_sample.shape
    assert bp == 1

    wts = {
        'g_pre_mix': g_pre_mix[l][None, :],
        'w_in': _regroup_cols(w_in[l]).astype(BF16),
        'b_in': _regroup_cols(b_in[l])[None, :],
        'sinks': swa_sinks[l],
        'conv_w': conv_w[l],
        'alog': _pad_row(gdn_a_log[l]),
        'dt': _pad_row(gdn_dt_bias[l]),
        'norm_g': gdn_norm_g[l][None, :],
        'w_branch': w_branch[l].astype(BF16),
        'w_out': w_out[l].astype(BF16),
        'g_post_mix': g_post_mix[l][None, :],
        'g_pre_ffn': g_pre_ffn[l][None, :],
        'w_up': w_up[l].astype(BF16),
        'w_down': w_down[l].astype(BF16),
        'g_post_ffn': g_post_ffn[l][None, :],
    }

    mkv = _mem_kv(mem_prompt[0], g_mem[l][None, :], w_mem_kv[l].astype(BF16))
    cfg_p = dict(tm_proj=2048, tn_proj=512, tq=256, cache=False, tt_gdn=256, seg=256, chunk=128, carry=True,
                 tt_mem=512, tm=512)
    zero_hist = jnp.zeros((1, CONV_W - 1, 3 * GDN_QK), F32)
    zero_s = jnp.zeros((1, GDN_HEADS, GDN_DK, GDN_DV), F32)
    yp, kp_rot, sp_new, proj_p = _layer(
        x_prompt.reshape(bp * tp, D_MODEL), bp, tp, jnp.arange(tp, dtype=jnp.int32), None, None,
        zero_s, zero_hist, mkv, mkv, wts, cfg_p)

    cfg_s = dict(tm_proj=1024, tn_proj=512, tq=ts, cache=True, tt_gdn=128, seg=ts, chunk=ts, carry=False,
                 tt_mem=ts, tm=512)
    ys, ks_rot, ss_new, proj_s = _layer(
        x_sample.reshape(bs * ts, D_MODEL), bs, ts, PAST_LEN + jnp.arange(ts, dtype=jnp.int32),
        cache_swa_k[l].reshape(bs, WINDOW, SWA_KV), cache_swa_v[l].reshape(bs, WINDOW, SWA_KV),
        state_gdn[l], state_conv[l],
        cache_mem_k[l].reshape(bs, N_MEM, MEM_Q), cache_mem_v[l].reshape(bs, N_MEM, MEM_Q), wts, cfg_s)

    kv_shape = (SWA_KV_HEADS, SWA_HEAD_DIM)
    conv_cols = slice(COL_QB * LANE, (COL_ZB) * LANE)
    va_cols = slice(COL_VA * LANE, (COL_VA + 1) * LANE)

    p_k = kp_rot[tp - WINDOW:].reshape((1, bp, WINDOW) + kv_shape)
    p_v = proj_p[tp - WINDOW:, va_cols].reshape((1, bp, WINDOW) + kv_shape)
    p_c = proj_p[tp - (CONV_W - 1):, conv_cols].reshape(1, bp, CONV_W - 1, 3 * GDN_QK)
    p_mk = mkv[:, :MEM_Q].reshape(1, bp, N_MEM, MEM_HEADS, MEM_HEAD_DIM)
    p_mv = mkv[:, MEM_Q:].reshape(1, bp, N_MEM, MEM_HEADS, MEM_HEAD_DIM)

    ks_new = ks_rot.reshape(bs, ts, SWA_KV)
    vs_new = proj_s[:, va_cols].reshape(bs, ts, SWA_KV)
    s_k = jnp.concatenate([cache_swa_k[l].reshape(bs, WINDOW, SWA_KV), ks_new], axis=1)[:, -WINDOW:]
    s_v = jnp.concatenate([cache_swa_v[l].reshape(bs, WINDOW, SWA_KV), vs_new], axis=1)[:, -WINDOW:]
    s_c = proj_s[:, conv_cols].reshape(bs, ts, 3 * GDN_QK)[:, ts - (CONV_W - 1):]

    return (yp.reshape(bp, tp, D_MODEL), ys.reshape(bs, ts, D_MODEL),
            p_k, p_v, sp_new[None], p_c, p_mk, p_mv,
            s_k.reshape((1, bs, WINDOW) + kv_shape), s_v.reshape((1, bs, WINDOW) + kv_shape),
            ss_new[None], s_c[None])
```

```python
import functools

import jax
import jax.numpy as jnp
from jax import lax
from jax.experimental import pallas as pl
from jax.experimental.pallas import tpu as pltpu

F32 = jnp.float32
BF16 = jnp.bfloat16

D_MODEL = 1024
PAST_LEN = 2048
CHUNK = 64
N_MEM = 256
EPS = 1e-6
SWA_HEADS = 16
SWA_KV_HEADS = 2
SWA_HEAD_DIM = 64
WINDOW = 128
ROPE_THETA = 10000.0
GDN_HEADS = 8
GDN_DK = 128
GDN_DV = 128
CONV_W = 4
MEM_HEADS = 4
MEM_HEAD_DIM = 256
N_BRANCH = 3
D_FF = 4 * D_MODEL

LANE = 128
SWA_Q = SWA_HEADS * SWA_HEAD_DIM
SWA_KV = SWA_KV_HEADS * SWA_HEAD_DIM
GDN_QK = GDN_HEADS * GDN_DK
GDN_V = GDN_HEADS * GDN_DV
MEM_Q = MEM_HEADS * MEM_HEAD_DIM

COL_QB = 0
COL_KB = 8
COL_VB = 16
COL_ZB = 24
COL_QA = 32
COL_KA = 40
COL_VA = 41
COL_AB = 42
N_PROJ = 44 * LANE
O_KA = SWA_Q
O_QB = O_KA + 2 * SWA_KV
O_AB = O_QB + 2 * GDN_QK + 2 * GDN_V
O_QC = O_AB + 2 * GDN_HEADS
O_END = O_QC + MEM_Q + N_BRANCH * D_MODEL

VMEM_LIMIT = 48 * 1024 * 1024


def _dot(a, b):
    return jnp.dot(a, b, preferred_element_type=F32)


def _dot_nt(a, b):
    return lax.dot_general(a, b, (((1,), (1,)), ((), ())), preferred_element_type=F32)


def _dot_tn(a, b):
    return lax.dot_general(a, b, (((0,), (0,)), ((), ())), preferred_element_type=F32)


def _rms(x, g):
    return x * lax.rsqrt(jnp.mean(x * x, axis=-1, keepdims=True) + EPS) * g


def _sigmoid(x):
    return 1.0 / (1.0 + jnp.exp(-x))


def _resident(shape):
    return pl.BlockSpec(shape, lambda *_: (0,) * len(shape), pipeline_mode=pl.Buffered(1))


def _split3(x):
    hi = x.astype(BF16)
    r = x - hi.astype(F32)
    mid = r.astype(BF16)
    lo = (r - mid.astype(F32)).astype(BF16)
    return hi, mid, lo


def _mem_kv_kernel(m_ref, g_ref, w_ref, o_ref, ob_ref):
    h = _rms(m_ref[...], g_ref[...]).astype(BF16)
    kv = _dot(h, w_ref[...])
    o_ref[...] = kv
    ob_ref[...] = kv.astype(BF16)


def _mem_kv(mem, g_mem, w_bf16):
    n = w_bf16.shape[1]
    tn = 1024
    return pl.pallas_call(
        _mem_kv_kernel,
        grid=(n // tn,),
        in_specs=[
            pl.BlockSpec((N_MEM, D_MODEL), lambda j: (0, 0)),
            pl.BlockSpec((1, D_MODEL), lambda j: (0, 0)),
            pl.BlockSpec((D_MODEL, tn), lambda j: (0, j)),
        ],
        out_specs=[pl.BlockSpec((N_MEM, tn), lambda j: (0, j))] * 2,
        out_shape=[jax.ShapeDtypeStruct((N_MEM, n), F32), jax.ShapeDtypeStruct((N_MEM, n), BF16)],
        compiler_params=pltpu.CompilerParams(dimension_semantics=("arbitrary",), vmem_limit_bytes=VMEM_LIMIT),
        name="mem_kv",
    )(mem, g_mem, w_bf16)


def _in_proj_kernel(x_ref, g_ref, w_ref, b_ref, o_ref, h_ref):
    @pl.when(pl.program_id(1) == 0)
    def _():
        h_ref[...] = _rms(x_ref[...], g_ref[...]).astype(BF16)

    o_ref[...] = _dot(h_ref[...], w_ref[...]) + b_ref[...]


def _in_proj(x, g, w_bf16, b, tm, tn):
    m = x.shape[0]
    return pl.pallas_call(
        _in_proj_kernel,
        grid=(m // tm, N_PROJ // tn),
        in_specs=[
            pl.BlockSpec((tm, D_MODEL), lambda i, j: (i, 0)),
            pl.BlockSpec((1, D_MODEL), lambda i, j: (0, 0)),
            pl.BlockSpec((D_MODEL, tn), lambda i, j: (0, j)),
            pl.BlockSpec((1, tn), lambda i, j: (0, j)),
        ],
        out_specs=pl.BlockSpec((tm, tn), lambda i, j: (i, j)),
        out_shape=jax.ShapeDtypeStruct((m, N_PROJ), F32),
        scratch_shapes=[pltpu.VMEM((tm, D_MODEL), BF16)],
        compiler_params=pltpu.CompilerParams(dimension_semantics=("parallel", "arbitrary"),
                                             vmem_limit_bytes=VMEM_LIMIT),
        name="in_proj",
    )(x, g, w_bf16, b)


def _rope(x, cos, sin_signed):
    lane = lax.broadcasted_iota(jnp.int32, x.shape, 1)
    first = (lane % SWA_HEAD_DIM) < (SWA_HEAD_DIM // 2)
    rot = jnp.where(first, pltpu.roll(x, LANE - SWA_HEAD_DIM // 2, 1), pltpu.roll(x, SWA_HEAD_DIM // 2, 1))
    return x * cos + rot * sin_signed


KEYS = WINDOW + CHUNK
KPAD = 256


def _swa_kernel(sink_ref, q_ref, k_ref, v_ref, kh_ref, vh_ref, base_ref, off_ref,
                o_ref, kout_ref, qs_ref, ka_ref, kb_ref, va_ref, vb_ref, *, tq, cache):
    i = pl.program_id(1)
    cb, sb = base_ref[0:1, :], base_ref[1:2, :]
    co, so = off_ref[0], off_ref[1]
    cos_all = cb * co - sb * so
    sin_all = sb * co + cb * so
    cos, sin = cos_all[WINDOW:], sin_all[WINDOW:]
    kr = _rope(k_ref[...], cos, sin)
    kout_ref[...] = kr
    if cache:
        khr = kh_ref[...]
    else:
        khr = _rope(kh_ref[...], cos_all[:WINDOW], sin_all[:WINDOW])
    zpad = jnp.zeros((KPAD - KEYS, LANE), F32)
    kall = jnp.concatenate([khr, kr, zpad], axis=0)
    vall = jnp.concatenate([vh_ref[...], v_ref[...], zpad], axis=0)
    lo = lax.broadcasted_iota(jnp.int32, kall.shape, 1) < SWA_HEAD_DIM
    zero = jnp.zeros_like(kall)
    for src, a_ref, b_ref in ((kall, ka_ref, kb_ref), (vall, va_ref, vb_ref)):
        rolled = pltpu.roll(src, SWA_HEAD_DIM, 1)
        a_ref[0] = jnp.where(lo, src, zero).astype(BF16)
        b_ref[0] = jnp.where(lo, zero, rolled).astype(BF16)
        a_ref[1] = jnp.where(lo, rolled, zero).astype(BF16)
        b_ref[1] = jnp.where(lo, zero, src).astype(BF16)

    scale = SWA_HEAD_DIM ** -0.5
    for p in range(SWA_HEADS // 2):
        sl = slice(p * LANE, (p + 1) * LANE)
        qs_ref[:, sl] = (_rope(q_ref[:, sl], cos, sin) * scale).astype(BF16)

    jj = lax.broadcasted_iota(jnp.int32, (CHUNK, KPAD), 1)
    pairs_per_kv = SWA_HEADS // SWA_KV_HEADS // 2
    inst = [(c, g) for c in range(tq // CHUNK) for g in range(SWA_KV_HEADS)]

    def scores(c, g):
        rows = slice(c * CHUNK, (c + 1) * CHUNK)
        krows = slice(c * CHUNK, c * CHUNK + KPAD)
        kbd = jnp.concatenate([ka_ref[g, krows, :], kb_ref[g, krows, :]], axis=0)
        qst = jnp.concatenate(
            [qs_ref[rows, (g * pairs_per_kv + j) * LANE:(g * pairs_per_kv + j + 1) * LANE]
             for j in range(pairs_per_kv)], axis=0)
        return _dot_nt(qst, kbd)

    def probs(c, g, s):
        if cache:
            valid = jj < KEYS
        else:
            valid = (jj < KEYS) & ((i > 0) | (jj + c * CHUNK >= WINDOW))
        pblocks = []
        for j in range(pairs_per_kv):
            halves = []
            for half in range(2):
                sh = s[j * CHUNK:(j + 1) * CHUNK, half * KPAD:(half + 1) * KPAD]
                sh = jnp.where(valid, sh, -jnp.inf)
                sink = sink_ref[2 * (g * pairs_per_kv + j) + half]
                m = jnp.maximum(jnp.max(sh, axis=-1, keepdims=True), sink)
                e = jnp.exp(sh - m)
                den = jnp.sum(e, axis=-1, keepdims=True) + jnp.exp(sink - m)
                halves.append((e * (1.0 / den)).astype(BF16))
            pblocks.append(jnp.concatenate(halves, axis=1))
        return jnp.concatenate(pblocks, axis=0)

    def outputs(c, g, pmat):
        rows = slice(c * CHUNK, (c + 1) * CHUNK)
        krows = slice(c * CHUNK, c * CHUNK + KPAD)
        vbd = jnp.concatenate([va_ref[g, krows, :], vb_ref[g, krows, :]], axis=0)
        o = _dot(pmat, vbd)
        for j in range(pairs_per_kv):
            p = g * pairs_per_kv + j
            o_ref[rows, p * LANE:(p + 1) * LANE] = o[j * CHUNK:(j + 1) * CHUNK].astype(BF16)

    s_l, p_l = {}, {}
    for step in range(len(inst) + 2):
        if step < len(inst):
            s_l[step] = scores(*inst[step])
        if 0 <= step - 1 < len(inst):
            p_l[step - 1] = probs(*inst[step - 1], s_l.pop(step - 1))
        if 0 <= step - 2 < len(inst):
            outputs(*inst[step - 2], p_l.pop(step - 2))


def _swa(proj, k_hist, v_hist, rope_base, rope_off, sinks, nb, t, tq, cache):
    nt = t // tq
    kv_rows = WINDOW + tq + (KPAD - KEYS)
    if cache:
        hist_k_spec = pl.BlockSpec((None, WINDOW, LANE), lambda b, i: (b, 0, 0))
        hist_v_spec = pl.BlockSpec((None, WINDOW, LANE), lambda b, i: (b, 0, 0))
        hk, hv = k_hist, v_hist
    else:
        r = tq // WINDOW

        def hidx(b, i):
            return jnp.maximum((b * nt + i) * r - 1, 0)

        hist_k_spec = pl.BlockSpec((WINDOW, LANE), lambda b, i: (hidx(b, i), COL_KA))
        hist_v_spec = pl.BlockSpec((WINDOW, LANE), lambda b, i: (hidx(b, i), COL_VA))
        hk, hv = proj, proj
    return pl.pallas_call(
        functools.partial(_swa_kernel, tq=tq, cache=cache),
        grid=(nb, nt),
        in_specs=[
            pl.BlockSpec(memory_space=pltpu.SMEM),
            pl.BlockSpec((tq, SWA_Q), lambda b, i: (b * nt + i, COL_QA // 8)),
            pl.BlockSpec((tq, LANE), lambda b, i: (b * nt + i, COL_KA)),
            pl.BlockSpec((tq, LANE), lambda b, i: (b * nt + i, COL_VA)),
            hist_k_spec,
            hist_v_spec,
            pl.BlockSpec((None, 2, LANE), lambda b, i: (i, 0, 0)),
            pl.BlockSpec((2, WINDOW + tq, LANE), lambda b, i: (0, 0, 0)),
        ],
        out_specs=[
            pl.BlockSpec((tq, SWA_Q), lambda b, i: (b * nt + i, 0)),
            pl.BlockSpec((tq, LANE), lambda b, i: (b * nt + i, 0)),
        ],
        out_shape=[
            jax.ShapeDtypeStruct((nb * t, SWA_Q), BF16),
            jax.ShapeDtypeStruct((nb * t, LANE), F32),
        ],
        scratch_shapes=[
            pltpu.VMEM((tq, SWA_Q), BF16),
            pltpu.VMEM((SWA_KV_HEADS, kv_rows, LANE), BF16),
            pltpu.VMEM((SWA_KV_HEADS, kv_rows, LANE), BF16),
            pltpu.VMEM((SWA_KV_HEADS, kv_rows, LANE), BF16),
            pltpu.VMEM((SWA_KV_HEADS, kv_rows, LANE), BF16),
        ],
        compiler_params=pltpu.CompilerParams(dimension_semantics=("parallel", "arbitrary"),
                                             vmem_limit_bytes=VMEM_LIMIT),
        name="swa",
    )(sinks, proj, proj, proj, hk, hv, rope_base, rope_off)


INV_BASE = 16


def _inverse_minus_identity(pms, row, col, chunk):
    base = (row // INV_BASE) == (col // INV_BASE)
    diag = row == col
    ms = [jnp.where(base, pm, 0.0) for pm in pms]
    ts = [jnp.where(diag, 1.0, m) for m in ms]
    for _ in range(INV_BASE.bit_length() - 2):
        mbs = [m.astype(BF16) for m in ms]
        ms = [_dot(mb, mb) for mb in mbs]
        ts = [t + _dot(m.astype(BF16), t.astype(BF16)) for m, t in zip(ms, ts)]
    bs = INV_BASE
    while bs < chunk:
        inner = (row // bs) == (col // bs)
        outer = (row // (2 * bs)) == (col // (2 * bs))
        sib = outer & jnp.logical_not(inner)
        xs = [_dot(jnp.where(sib, pm, 0.0).astype(BF16), t.astype(BF16)) for pm, t in zip(pms, ts)]
        ts = [t + _dot(t.astype(BF16), x.astype(BF16)) for t, x in zip(ts, xs)]
        bs *= 2
    rs = [jnp.where(diag, 0.0, t) for t in ts]
    return rs


def _gdn_kernel(q_ref, k_ref, v_ref, z_ref, ab_ref, hist_ref, cw_ref, alog_ref, dt_ref, ng_ref, s0_ref,
                o_ref, sout_ref, s_scr, carry_scr, ext_scr, conv_scr, *, tt, seg, chunk, carry):
    i = pl.program_id(0)
    nseg = tt // seg
    nblk = LANE // chunk
    ngroups = tt // LANE
    hist_rows = CONV_W - 1

    def init():
        for s in range(nseg):
            s_scr[s] = s0_ref[s]
            carry_scr[s] = jnp.zeros((8, 3 * GDN_QK), F32)
            carry_scr[s, 8 - hist_rows:8, :] = hist_ref[s]

    if carry:
        pl.when(i == 0)(init)
    else:
        init()

    for s in range(nseg):
        ext_scr[0:8, :] = carry_scr[s]
        for cg, x_ref in enumerate((q_ref, k_ref, v_ref)):
            ext_scr[8:8 + seg, cg * GDN_QK:(cg + 1) * GDN_QK] = x_ref[s * seg:(s + 1) * seg, :]
        if carry:
            carry_scr[s] = ext_scr[seg:seg + 8, :]
        base = 8 - hist_rows
        for cb in range(3 * GDN_HEADS):
            lanes = slice(cb * LANE, (cb + 1) * LANE)
            y = ext_scr[base:base + seg, lanes] * cw_ref[0:1, lanes]
            for tap in range(1, CONV_W):
                y = y + ext_scr[base + tap:base + tap + seg, lanes] * cw_ref[tap:tap + 1, lanes]
            conv_scr[s * seg:(s + 1) * seg, lanes] = y * _sigmoid(y)

    row = lax.broadcasted_iota(jnp.int32, (LANE, LANE), 0)
    col = lax.broadcasted_iota(jnp.int32, (LANE, LANE), 1)
    same = (row // chunk) == (col // chunk)
    incl = same & (row >= col)
    strict = same & (row > col)
    lmask = jnp.where(incl, 1.0, 0.0).astype(BF16)
    row2 = jnp.concatenate([row, row], axis=0)

    inst = [(gi, hh) for gi in range(ngroups) for hh in range(GDN_HEADS)]
    heads = range(GDN_HEADS)

    gates = []
    for gi in range(ngroups):
        ab = ab_ref[gi * LANE:(gi + 1) * LANE, :]
        xa = ab + dt_ref[...]
        softplus = jnp.maximum(xa, 0.0) + jnp.log1p(jnp.exp(-jnp.abs(xa)))
        la = -jnp.exp(alog_ref[...]) * softplus
        beta = _sigmoid(ab)
        g_all = sum(_dot(lmask, piece) for piece in _split3(la))
        gates.append((g_all, g_all.T, beta))

    qn_l, kn_l, v_l, gb_l, bb_l, decay_l = [], [], [], [], [], []
    for gi, hh in inst:
        rows = slice(gi * LANE, (gi + 1) * LANE)
        q = conv_scr[rows, hh * LANE:(hh + 1) * LANE]
        k = conv_scr[rows, GDN_QK + hh * LANE:GDN_QK + (hh + 1) * LANE]
        v_l.append(conv_scr[rows, 2 * GDN_QK + hh * LANE:2 * GDN_QK + (hh + 1) * LANE])
        qn_l.append(q * (lax.rsqrt(jnp.sum(q * q, axis=-1, keepdims=True) + EPS) * (GDN_DK ** -0.5)))
        kn_l.append(k * lax.rsqrt(jnp.sum(k * k, axis=-1, keepdims=True) + EPS))
        g_all, g_all_t, beta = gates[gi]
        gb = jnp.broadcast_to(g_all[:, hh:hh + 1], (LANE, LANE))
        grow = jnp.broadcast_to(g_all_t[hh:hh + 1, :], (LANE, LANE))
        gb_l.append(gb)
        bb_l.append(jnp.broadcast_to(beta[:, GDN_HEADS + hh:GDN_HEADS + hh + 1], (LANE, LANE)))
        decay_l.append(jnp.exp(jnp.where(incl, gb - grow, -jnp.inf)))

    knb_l = [kn.astype(BF16) for kn in kn_l]
    cat_l = [_dot_nt(jnp.concatenate([qn.astype(BF16), knb], axis=0), knb)
             for qn, knb in zip(qn_l, knb_l)]
    qk_l = [cat[:LANE] * decay for cat, decay in zip(cat_l, decay_l)]
    pm_l = [jnp.where(strict, -(bb * cat[LANE:] * decay), 0.0)
            for bb, cat, decay in zip(bb_l, cat_l, decay_l)]
    r_l = _inverse_minus_identity(pm_l, row, col, chunk)
    eg_l = [jnp.exp(gb) for gb in gb_l]
    rhs_l = [jnp.concatenate([v * bb, kn * (bb * eg)], axis=1)
             for v, bb, kn, eg in zip(v_l, bb_l, kn_l, eg_l)]
    sol_l = [rhs + _dot(r.astype(BF16), rhs.astype(BF16)) for r, rhs in zip(r_l, rhs_l)]
    wq_l = [jnp.concatenate([sol[:, GDN_DV:], qn * eg], axis=0).astype(BF16)
            for sol, qn, eg in zip(sol_l, qn_l, eg_l)]
    kd_l = []
    for kn, gb in zip(kn_l, gb_l):
        glast = gb[chunk - 1:chunk, :]
        for blk in range(1, nblk):
            glast = jnp.where(row >= blk * chunk, gb[(blk + 1) * chunk - 1:(blk + 1) * chunk, :], glast)
        kd_l.append(kn * jnp.exp(glast - gb))

    nstate = 1 if carry else nblk
    state = [[s_scr[sidx, hh] for hh in heads] for sidx in range(nstate)]
    for gi in range(ngroups):
        ix = [gi * GDN_HEADS + hh for hh in heads]
        ws_l = []
        for hh in heads:
            ws = None
            for blk in range(nblk):
                part = _dot(wq_l[ix[hh]], state[0 if carry else blk][hh].astype(BF16))
                ws = part if ws is None else jnp.where(row2 >= blk * chunk, part, ws)
            ws_l.append(ws)
        vnb_l = [(sol_l[ix[hh]][:, :GDN_DV] - ws_l[hh][:LANE]).astype(BF16) for hh in heads]
        o_l = [ws_l[hh][LANE:] + _dot(qk_l[ix[hh]].astype(BF16), vnb_l[hh]) for hh in heads]
        for blk in range(nblk):
            sidx = 0 if carry else blk
            new = []
            for hh in heads:
                kd = kd_l[ix[hh]]
                gb = gb_l[ix[hh]]
                kdb = kd if nblk == 1 else jnp.where((row // chunk) == blk, kd, 0.0)
                gtot = jnp.exp(gb[(blk + 1) * chunk - 1:(blk + 1) * chunk, :])
                new.append(state[sidx][hh] * gtot + _dot_tn(kdb.astype(BF16), vnb_l[hh]))
            state[sidx] = new
        for hh in heads:
            rows = slice(gi * LANE, (gi + 1) * LANE)
            lanes = slice(hh * LANE, (hh + 1) * LANE)
            z = z_ref[rows, lanes]
            o_ref[rows, lanes] = (_rms(o_l[hh], ng_ref[...]) * (z * _sigmoid(z))).astype(BF16)
    for sidx in range(nstate):
        for hh in heads:
            s_scr[sidx, hh] = state[sidx][hh]

    def fin():
        for s in range(nseg):
            sout_ref[s] = s_scr[s]

    if carry:
        pl.when(i == pl.num_programs(0) - 1)(fin)
    else:
        fin()


def _gdn(proj, hist, s0, conv_w, alog_row, dt_row, norm_g, t_rows, tt, seg, chunk, carry):
    nt = t_rows // tt
    nseg = tt // seg
    nseq = s0.shape[0]
    if carry:
        seq_idx = lambda i: 0
    else:
        seq_idx = lambda i: i

    def col(c0):
        return pl.BlockSpec((tt, GDN_QK), lambda i: (i, c0 // GDN_HEADS))

    row_spec = pl.BlockSpec((1, LANE), lambda i: (0, 0))
    sspec = pl.BlockSpec((nseg, GDN_HEADS, GDN_DK, GDN_DV), lambda i: (seq_idx(i), 0, 0, 0))
    return pl.pallas_call(
        functools.partial(_gdn_kernel, tt=tt, seg=seg, chunk=chunk, carry=carry),
        grid=(nt,),
        in_specs=[
            col(COL_QB), col(COL_KB), col(COL_VB), col(COL_ZB),
            pl.BlockSpec((tt, LANE), lambda i: (i, COL_AB)),
            pl.BlockSpec((nseg, CONV_W - 1, 3 * GDN_QK), lambda i: (seq_idx(i), 0, 0)),
            pl.BlockSpec((CONV_W, 3 * GDN_QK), lambda i: (0, 0)),
            row_spec, row_spec, row_spec,
            sspec,
        ],
        out_specs=[
            pl.BlockSpec((tt, GDN_V), lambda i: (i, 0)),
            sspec,
        ],
        out_shape=[
            jax.ShapeDtypeStruct((t_rows, GDN_V), BF16),
            jax.ShapeDtypeStruct((nseq, GDN_HEADS, GDN_DK, GDN_DV), F32),
        ],
        scratch_shapes=[
            pltpu.VMEM((nseg, GDN_HEADS, GDN_DK, GDN_DV), F32),
            pltpu.VMEM((nseg, 8, 3 * GDN_QK), F32),
            pltpu.VMEM((8 + seg, 3 * GDN_QK), F32),
            pltpu.VMEM((tt, 3 * GDN_QK), F32),
        ],
        compiler_params=pltpu.CompilerParams(dimension_semantics=("arbitrary",), vmem_limit_bytes=VMEM_LIMIT),
        name="gdn",
    )(proj, proj, proj, proj, proj, hist, conv_w, alog_row, dt_row, norm_g, s0)


def _mix_kernel(x_ref, oa_ref, ob_ref, mk_ref, mv_ref, gpre_ref, wq_ref, bq_ref, wb_ref, wo_ref, gp_ref, o_ref):
    x = x_ref[...]
    h = _rms(x, gpre_ref[...]).astype(BF16)
    qc = _dot(h, wq_ref[:, :MEM_Q]) + bq_ref[:, :MEM_Q]
    scale = MEM_HEAD_DIM ** -0.5
    heads = []
    for hh in range(MEM_HEADS):
        sl = slice(hh * MEM_HEAD_DIM, (hh + 1) * MEM_HEAD_DIM)
        s = _dot_nt((qc[:, sl] * scale).astype(BF16), mk_ref[:, sl].astype(BF16))
        m = jnp.max(s, axis=-1, keepdims=True)
        e = jnp.exp(s - m)
        p = e * (1.0 / jnp.sum(e, axis=-1, keepdims=True))
        heads.append(_dot(p.astype(BF16), mv_ref[:, sl].astype(BF16)).astype(BF16))
    oc = jnp.concatenate(heads, axis=1)
    merged = None
    for n, br_in in enumerate((oa_ref[...], ob_ref[...], oc)):
        cols = slice(MEM_Q + n * D_MODEL, MEM_Q + (n + 1) * D_MODEL)
        gate = _sigmoid(_dot(h, wq_ref[:, cols]) + bq_ref[:, cols])
        term = gate * _dot(br_in, wb_ref[n])
        merged = term if merged is None else merged + term
    y = _dot(merged.astype(BF16), wo_ref[...])
    o_ref[...] = x + _rms(y, gp_ref[...])


def _mix(x, oa, ob, mk, mv, gpre, wq, bq, wb, wo, gp, tm, batched):
    m = x.shape[0]
    tile = pl.BlockSpec((tm, D_MODEL), lambda i: (i, 0))
    vec = pl.BlockSpec((1, D_MODEL), lambda i: (0, 0))
    if batched:
        mspec = pl.BlockSpec((None, N_MEM, MEM_Q), lambda i: (i, 0, 0))
    else:
        mspec = _resident((N_MEM, MEM_Q))
    nq = MEM_Q + N_BRANCH * D_MODEL
    return pl.pallas_call(
        _mix_kernel,
        grid=(m // tm,),
        in_specs=[
            tile, tile, tile, mspec, mspec, vec,
            _resident((D_MODEL, nq)),
            _resident((1, nq)),
            _resident((N_BRANCH, D_MODEL, D_MODEL)),
            _resident((D_MODEL, D_MODEL)),
            vec,
        ],
        out_specs=tile,
        out_shape=jax.ShapeDtypeStruct((m, D_MODEL), F32),
        compiler_params=pltpu.CompilerParams(dimension_semantics=("parallel",), vmem_limit_bytes=VMEM_LIMIT),
        name="mix",
    )(x, oa, ob, mk, mv, gpre, wq, bq, wb, wo, gp)


FF_CHUNK = 1024


def _ffn_kernel(x_ref, g1_ref, wu_ref, wd_ref, g2_ref, o_ref):
    x = x_ref[...]
    hf = _rms(x, g1_ref[...]).astype(BF16)
    f = None
    for c in range(D_FF // FF_CHUNK):
        sl = slice(c * FF_CHUNK, (c + 1) * FF_CHUNK)
        up = jnp.maximum(_dot(hf, wu_ref[:, sl]), 0.0)
        part = _dot((up * up).astype(BF16), wd_ref[sl, :])
        f = part if f is None else f + part
    o_ref[...] = x + _rms(f, g2_ref[...])


def _ffn(x, g1, wu, wd, g2, tm):
    m = x.shape[0]
    tile = pl.BlockSpec((tm, D_MODEL), lambda i: (i, 0))
    vec = pl.BlockSpec((1, D_MODEL), lambda i: (0, 0))
    return pl.pallas_call(
        _ffn_kernel,
        grid=(m // tm,),
        in_specs=[tile, vec, _resident((D_MODEL, D_FF)), _resident((D_FF, D_MODEL)), vec],
        out_specs=tile,
        out_shape=jax.ShapeDtypeStruct((m, D_MODEL), F32),
        compiler_params=pltpu.CompilerParams(dimension_semantics=("parallel",), vmem_limit_bytes=VMEM_LIMIT),
        name="ffn",
    )(x, g1, wu, wd, g2)


def _regroup_cols(w):
    parts = [w[..., O_QB:O_AB], w[..., :O_KA], w[..., O_KA:O_QB], w[..., O_AB:O_QC]]
    out = jnp.concatenate(parts, axis=-1)
    pad = N_PROJ - out.shape[-1]
    return jnp.pad(out, [(0, 0)] * (out.ndim - 1) + [(0, pad)])


def _rope_tables(pos0, nt, tq):
    half = SWA_HEAD_DIM // 2
    inv = ROPE_THETA ** (-jnp.arange(half, dtype=F32) / half)
    inv = jnp.tile(inv, LANE // half)
    sgn = jnp.tile(jnp.concatenate([-jnp.ones((half,), F32), jnp.ones((half,), F32)]), LANE // SWA_HEAD_DIM)
    base = (pos0 + tq * jnp.arange(nt, dtype=jnp.int32)).astype(F32)[:, None] * inv[None, :]
    off = jnp.arange(-WINDOW, tq, dtype=jnp.int32).astype(F32)[:, None] * inv[None, :]
    rope_base = jnp.stack([jnp.cos(base), sgn * jnp.sin(base)], axis=1)
    rope_off = jnp.stack([jnp.cos(off), sgn * jnp.sin(off)], axis=0)
    return rope_base, rope_off


def _pad_row(v, offset=0):
    return jnp.zeros((1, LANE), F32).at[0, offset:offset + v.shape[0]].set(v)


def _layer(x, nb, t, pos0, k_hist, v_hist, s0, conv_hist, mk, mv, wts, cfg):
    proj = _in_proj(x, wts['g_pre_mix'], wts['w_in'], wts['b_in'], cfg['tm_proj'], cfg['tn_proj'])
    rope_base, rope_off = _rope_tables(pos0, t // cfg['tq'], cfg['tq'])
    oa, k_rot = _swa(proj, k_hist, v_hist, rope_base, rope_off, wts['sinks'], nb, t, cfg['tq'], cfg['cache'])
    ob, s_new = _gdn(proj, conv_hist, s0, wts['conv_w'], wts['alog'], wts['dt'], wts['norm_g'],
                     nb * t, cfg['tt_gdn'], cfg['seg'], cfg['chunk'], cfg['carry'])
    x1 = _mix(x, oa, ob, mk, mv, wts['g_pre_mix'], wts['w_q'], wts['b_q'], wts['w_branch'], wts['w_out'],
              wts['g_post_mix'], cfg['tm_mix'], cfg['cache'])
    y = _ffn(x1, wts['g_pre_ffn'], wts['w_up'], wts['w_down'], wts['g_post_ffn'], cfg['tm'])
    return y, k_rot, s_new, proj


def kernel(x_prompt, x_sample, mem_prompt, cache_swa_k, cache_swa_v, state_gdn, state_conv, cache_mem_k, cache_mem_v, g_pre_mix, w_in, b_in, swa_sinks, conv_w, gdn_a_log, gdn_dt_bias, gdn_norm_g, g_mem, w_mem_kv, w_branch, w_out, g_post_mix, g_pre_ffn, w_up, w_down, g_post_ffn):
    depth = w_in.shape[0]
    assert depth == 1
    l = 0
    bp, tp, _ = x_prompt.shape
    bs, ts, _ = x_sample.shape
    assert bp == 1

    wts = {
        'g_pre_mix': g_pre_mix[l][None, :],
        'w_in': _regroup_cols(w_in[l]).astype(BF16),
        'b_in': _regroup_cols(b_in[l])[None, :],
        'w_q': w_in[l][:, O_QC:O_END].astype(BF16),
        'b_q': b_in[l][None, O_QC:O_END],
        'sinks': swa_sinks[l],
        'conv_w': conv_w[l],
        'alog': _pad_row(gdn_a_log[l]),
        'dt': _pad_row(gdn_dt_bias[l]),
        'norm_g': gdn_norm_g[l][None, :],
        'w_branch': w_branch[l].astype(BF16),
        'w_out': w_out[l].astype(BF16),
        'g_post_mix': g_post_mix[l][None, :],
        'g_pre_ffn': g_pre_ffn[l][None, :],
        'w_up': w_up[l].astype(BF16),
        'w_down': w_down[l].astype(BF16),
        'g_post_ffn': g_post_ffn[l][None, :],
    }

    mkv, mkv_bf = _mem_kv(mem_prompt[0], g_mem[l][None, :], w_mem_kv[l].astype(BF16))
    cfg_p = dict(tm_proj=2048, tn_proj=512, tq=256, cache=False, tt_gdn=256, seg=256, chunk=128, carry=True,
                 tm_mix=512, tm=512)
    zero_hist = jnp.zeros((1, CONV_W - 1, 3 * GDN_QK), F32)
    zero_s = jnp.zeros((1, GDN_HEADS, GDN_DK, GDN_DV), F32)
    yp, kp_rot, sp_new, proj_p = _layer(
        x_prompt.reshape(bp * tp, D_MODEL), bp, tp, 0, None, None,
        zero_s, zero_hist, mkv_bf[:, :MEM_Q], mkv_bf[:, MEM_Q:], wts, cfg_p)

    cfg_s = dict(tm_proj=1024, tn_proj=512, tq=ts, cache=True, tt_gdn=128, seg=ts, chunk=ts, carry=False,
                 tm_mix=ts, tm=512)
    ys, ks_rot, ss_new, proj_s = _layer(
        x_sample.reshape(bs * ts, D_MODEL), bs, ts, PAST_LEN,
        cache_swa_k[l].reshape(bs, WINDOW, SWA_KV), cache_swa_v[l].reshape(bs, WINDOW, SWA_KV),
        state_gdn[l], state_conv[l],
        cache_mem_k[l].reshape(bs, N_MEM, MEM_Q), cache_mem_v[l].reshape(bs, N_MEM, MEM_Q), wts, cfg_s)

    kv_shape = (SWA_KV_HEADS, SWA_HEAD_DIM)
    conv_cols = slice(COL_QB * LANE, COL_ZB * LANE)
    va_cols = slice(COL_VA * LANE, (COL_VA + 1) * LANE)

    p_k = kp_rot[tp - WINDOW:].reshape((1, bp, WINDOW) + kv_shape)
    p_v = proj_p[tp - WINDOW:, va_cols].reshape((1, bp, WINDOW) + kv_shape)
    p_c = proj_p[tp - (CONV_W - 1):, conv_cols].reshape(1, bp, CONV_W - 1, 3 * GDN_QK)
    p_mk = mkv[:, :MEM_Q].reshape(1, bp, N_MEM, MEM_HEADS, MEM_HEAD_DIM)
    p_mv = mkv[:, MEM_Q:].reshape(1, bp, N_MEM, MEM_HEADS, MEM_HEAD_DIM)

    ks_new = ks_rot.reshape(bs, ts, SWA_KV)
    vs_new = proj_s[:, va_cols].reshape(bs, ts, SWA_KV)
    s_k = jnp.concatenate([cache_swa_k[l].reshape(bs, WINDOW, SWA_KV), ks_new], axis=1)[:, -WINDOW:]
    s_v = jnp.concatenate([cache_swa_v[l].reshape(bs, WINDOW, SWA_KV), vs_new], axis=1)[:, -WINDOW:]
    s_c = proj_s.reshape(bs, ts, N_PROJ)[:, ts - (CONV_W - 1):, conv_cols]

    return (yp.reshape(bp, tp, D_MODEL), ys.reshape(bs, ts, D_MODEL),
            p_k, p_v, sp_new[None], p_c, p_mk, p_mv,
            s_k.reshape((1, bs, WINDOW) + kv_shape), s_v.reshape((1, bs, WINDOW) + kv_shape),
            ss_new[None], s_c[None])
```

```python
import functools

import jax
import jax.numpy as jnp
from jax import lax
from jax.experimental import pallas as pl
from jax.experimental.pallas import tpu as pltpu

F32 = jnp.float32
BF16 = jnp.bfloat16

D_MODEL = 1024
PAST_LEN = 2048
CHUNK = 64
N_MEM = 256
EPS = 1e-6
SWA_HEADS = 16
SWA_KV_HEADS = 2
SWA_HEAD_DIM = 64
WINDOW = 128
ROPE_THETA = 10000.0
GDN_HEADS = 8
GDN_DK = 128
GDN_DV = 128
CONV_W = 4
MEM_HEADS = 4
MEM_HEAD_DIM = 256
N_BRANCH = 3
D_FF = 4 * D_MODEL

LANE = 128
SWA_Q = SWA_HEADS * SWA_HEAD_DIM
SWA_KV = SWA_KV_HEADS * SWA_HEAD_DIM
GDN_QK = GDN_HEADS * GDN_DK
GDN_V = GDN_HEADS * GDN_DV
MEM_Q = MEM_HEADS * MEM_HEAD_DIM

COL_QB = 0
COL_KB = 8
COL_VB = 16
COL_ZB = 24
COL_QA = 32
COL_KA = 40
COL_VA = 41
COL_AB = 42
N_PROJ = 44 * LANE
O_KA = SWA_Q
O_QB = O_KA + 2 * SWA_KV
O_AB = O_QB + 2 * GDN_QK + 2 * GDN_V
O_QC = O_AB + 2 * GDN_HEADS
O_END = O_QC + MEM_Q + N_BRANCH * D_MODEL

VMEM_LIMIT = 48 * 1024 * 1024


def _dot(a, b):
    return jnp.dot(a, b, preferred_element_type=F32)


def _dot_nt(a, b):
    return lax.dot_general(a, b, (((1,), (1,)), ((), ())), preferred_element_type=F32)


def _dot_tn(a, b):
    return lax.dot_general(a, b, (((0,), (0,)), ((), ())), preferred_element_type=F32)


def _rms(x, g):
    return x * lax.rsqrt(jnp.mean(x * x, axis=-1, keepdims=True) + EPS) * g


def _sigmoid(x):
    return 1.0 / (1.0 + jnp.exp(-x))


def _resident(shape):
    return pl.BlockSpec(shape, lambda *_: (0,) * len(shape), pipeline_mode=pl.Buffered(1))


def _split3(x):
    hi = x.astype(BF16)
    r = x - hi.astype(F32)
    mid = r.astype(BF16)
    lo = (r - mid.astype(F32)).astype(BF16)
    return hi, mid, lo


def _mem_kv_kernel(m_ref, g_ref, w_ref, o_ref, ob_ref):
    h = _rms(m_ref[...], g_ref[...]).astype(BF16)
    kv = _dot(h, w_ref[...])
    o_ref[...] = kv
    ob_ref[...] = kv.astype(BF16)


def _mem_kv(mem, g_mem, w_bf16):
    n = w_bf16.shape[1]
    tn = 1024
    return pl.pallas_call(
        _mem_kv_kernel,
        grid=(n // tn,),
        in_specs=[
            pl.BlockSpec((N_MEM, D_MODEL), lambda j: (0, 0)),
            pl.BlockSpec((1, D_MODEL), lambda j: (0, 0)),
            pl.BlockSpec((D_MODEL, tn), lambda j: (0, j)),
        ],
        out_specs=[pl.BlockSpec((N_MEM, tn), lambda j: (0, j))] * 2,
        out_shape=[jax.ShapeDtypeStruct((N_MEM, n), F32), jax.ShapeDtypeStruct((N_MEM, n), BF16)],
        compiler_params=pltpu.CompilerParams(dimension_semantics=("arbitrary",), vmem_limit_bytes=VMEM_LIMIT),
        name="mem_kv",
    )(mem, g_mem, w_bf16)


def _in_proj_kernel(x_ref, g_ref, w_ref, b_ref, o_ref, h_ref):
    @pl.when(pl.program_id(1) == 0)
    def _():
        h_ref[...] = _rms(x_ref[...], g_ref[...]).astype(BF16)

    o_ref[...] = _dot(h_ref[...], w_ref[...]) + b_ref[...]


def _in_proj(x, g, w_bf16, b, tm, tn):
    m = x.shape[0]
    return pl.pallas_call(
        _in_proj_kernel,
        grid=(m // tm, N_PROJ // tn),
        in_specs=[
            pl.BlockSpec((tm, D_MODEL), lambda i, j: (i, 0)),
            pl.BlockSpec((1, D_MODEL), lambda i, j: (0, 0)),
            pl.BlockSpec((D_MODEL, tn), lambda i, j: (0, j)),
            pl.BlockSpec((1, tn), lambda i, j: (0, j)),
        ],
        out_specs=pl.BlockSpec((tm, tn), lambda i, j: (i, j)),
        out_shape=jax.ShapeDtypeStruct((m, N_PROJ), F32),
        scratch_shapes=[pltpu.VMEM((tm, D_MODEL), BF16)],
        compiler_params=pltpu.CompilerParams(dimension_semantics=("parallel", "arbitrary"),
                                             vmem_limit_bytes=VMEM_LIMIT),
        name="in_proj",
    )(x, g, w_bf16, b)


def _rope(x, cos, sin_signed):
    lane = lax.broadcasted_iota(jnp.int32, x.shape, 1)
    first = (lane % SWA_HEAD_DIM) < (SWA_HEAD_DIM // 2)
    rot = jnp.where(first, pltpu.roll(x, LANE - SWA_HEAD_DIM // 2, 1), pltpu.roll(x, SWA_HEAD_DIM // 2, 1))
    return x * cos + rot * sin_signed


KEYS = WINDOW + CHUNK
KPAD = 256


def _swa_kernel(sink_ref, q_ref, k_ref, v_ref, kh_ref, vh_ref, base_ref, off_ref,
                o_ref, kout_ref, qs_ref, ka_ref, kb_ref, va_ref, vb_ref, *, tq, cache):
    i = pl.program_id(1)
    cb, sb = base_ref[0:1, :], base_ref[1:2, :]
    co, so = off_ref[0], off_ref[1]
    cos_all = cb * co - sb * so
    sin_all = sb * co + cb * so
    cos, sin = cos_all[WINDOW:], sin_all[WINDOW:]
    kr = _rope(k_ref[...], cos, sin)
    kout_ref[...] = kr
    if cache:
        khr = kh_ref[...]
    else:
        khr = _rope(kh_ref[...], cos_all[:WINDOW], sin_all[:WINDOW])
    zpad = jnp.zeros((KPAD - KEYS, LANE), F32)
    kall = jnp.concatenate([khr, kr, zpad], axis=0)
    vall = jnp.concatenate([vh_ref[...], v_ref[...], zpad], axis=0)
    lo = lax.broadcasted_iota(jnp.int32, kall.shape, 1) < SWA_HEAD_DIM
    zero = jnp.zeros_like(kall)
    for src, a_ref, b_ref in ((kall, ka_ref, kb_ref), (vall, va_ref, vb_ref)):
        rolled = pltpu.roll(src, SWA_HEAD_DIM, 1)
        a_ref[0] = jnp.where(lo, src, zero).astype(BF16)
        b_ref[0] = jnp.where(lo, zero, rolled).astype(BF16)
        a_ref[1] = jnp.where(lo, rolled, zero).astype(BF16)
        b_ref[1] = jnp.where(lo, zero, src).astype(BF16)

    scale = SWA_HEAD_DIM ** -0.5
    for p in range(SWA_HEADS // 2):
        sl = slice(p * LANE, (p + 1) * LANE)
        qs_ref[:, sl] = (_rope(q_ref[:, sl], cos, sin) * scale).astype(BF16)

    jj = lax.broadcasted_iota(jnp.int32, (CHUNK, KPAD), 1)
    pairs_per_kv = SWA_HEADS // SWA_KV_HEADS // 2
    inst = [(c, g) for c in range(tq // CHUNK) for g in range(SWA_KV_HEADS)]

    def scores(c, g):
        rows = slice(c * CHUNK, (c + 1) * CHUNK)
        krows = slice(c * CHUNK, c * CHUNK + KPAD)
        kbd = jnp.concatenate([ka_ref[g, krows, :], kb_ref[g, krows, :]], axis=0)
        qst = jnp.concatenate(
            [qs_ref[rows, (g * pairs_per_kv + j) * LANE:(g * pairs_per_kv + j + 1) * LANE]
             for j in range(pairs_per_kv)], axis=0)
        return _dot_nt(qst, kbd)

    def probs(c, g, s):
        if cache:
            valid = jj < KEYS
        else:
            valid = (jj < KEYS) & ((i > 0) | (jj + c * CHUNK >= WINDOW))
        pblocks = []
        for j in range(pairs_per_kv):
            halves = []
            for half in range(2):
                sh = s[j * CHUNK:(j + 1) * CHUNK, half * KPAD:(half + 1) * KPAD]
                sh = jnp.where(valid, sh, -jnp.inf)
                sink = sink_ref[2 * (g * pairs_per_kv + j) + half]
                m = jnp.maximum(jnp.max(sh, axis=-1, keepdims=True), sink)
                e = jnp.exp(sh - m)
                den = jnp.sum(e, axis=-1, keepdims=True) + jnp.exp(sink - m)
                halves.append((e * (1.0 / den)).astype(BF16))
            pblocks.append(jnp.concatenate(halves, axis=1))
        return jnp.concatenate(pblocks, axis=0)

    def outputs(c, g, pmat):
        rows = slice(c * CHUNK, (c + 1) * CHUNK)
        krows = slice(c * CHUNK, c * CHUNK + KPAD)
        vbd = jnp.concatenate([va_ref[g, krows, :], vb_ref[g, krows, :]], axis=0)
        o = _dot(pmat, vbd)
        for j in range(pairs_per_kv):
            p = g * pairs_per_kv + j
            o_ref[rows, p * LANE:(p + 1) * LANE] = o[j * CHUNK:(j + 1) * CHUNK].astype(BF16)

    s_l, p_l = {}, {}
    for step in range(len(inst) + 2):
        if step < len(inst):
            s_l[step] = scores(*inst[step])
        if 0 <= step - 1 < len(inst):
            p_l[step - 1] = probs(*inst[step - 1], s_l.pop(step - 1))
        if 0 <= step - 2 < len(inst):
            outputs(*inst[step - 2], p_l.pop(step - 2))


def _swa(proj, k_hist, v_hist, rope_base, rope_off, sinks, nb, t, tq, cache):
    nt = t // tq
    kv_rows = WINDOW + tq + (KPAD - KEYS)
    if cache:
        hist_k_spec = pl.BlockSpec((None, WINDOW, LANE), lambda b, i: (b, 0, 0))
        hist_v_spec = pl.BlockSpec((None, WINDOW, LANE), lambda b, i: (b, 0, 0))
        hk, hv = k_hist, v_hist
    else:
        r = tq // WINDOW

        def hidx(b, i):
            return jnp.maximum((b * nt + i) * r - 1, 0)

        hist_k_spec = pl.BlockSpec((WINDOW, LANE), lambda b, i: (hidx(b, i), COL_KA))
        hist_v_spec = pl.BlockSpec((WINDOW, LANE), lambda b, i: (hidx(b, i), COL_VA))
        hk, hv = proj, proj
    return pl.pallas_call(
        functools.partial(_swa_kernel, tq=tq, cache=cache),
        grid=(nb, nt),
        in_specs=[
            pl.BlockSpec(memory_space=pltpu.SMEM),
            pl.BlockSpec((tq, SWA_Q), lambda b, i: (b * nt + i, COL_QA // 8)),
            pl.BlockSpec((tq, LANE), lambda b, i: (b * nt + i, COL_KA)),
            pl.BlockSpec((tq, LANE), lambda b, i: (b * nt + i, COL_VA)),
            hist_k_spec,
            hist_v_spec,
            pl.BlockSpec((None, 2, LANE), lambda b, i: (i, 0, 0)),
            pl.BlockSpec((2, WINDOW + tq, LANE), lambda b, i: (0, 0, 0)),
        ],
        out_specs=[
            pl.BlockSpec((tq, SWA_Q), lambda b, i: (b * nt + i, 0)),
            pl.BlockSpec((tq, LANE), lambda b, i: (b * nt + i, 0)),
        ],
        out_shape=[
            jax.ShapeDtypeStruct((nb * t, SWA_Q), BF16),
            jax.ShapeDtypeStruct((nb * t, LANE), F32),
        ],
        scratch_shapes=[
            pltpu.VMEM((tq, SWA_Q), BF16),
            pltpu.VMEM((SWA_KV_HEADS, kv_rows, LANE), BF16),
            pltpu.VMEM((SWA_KV_HEADS, kv_rows, LANE), BF16),
            pltpu.VMEM((SWA_KV_HEADS, kv_rows, LANE), BF16),
            pltpu.VMEM((SWA_KV_HEADS, kv_rows, LANE), BF16),
        ],
        compiler_params=pltpu.CompilerParams(dimension_semantics=("parallel", "arbitrary"),
                                             vmem_limit_bytes=VMEM_LIMIT),
        name="swa",
    )(sinks, proj, proj, proj, hk, hv, rope_base, rope_off)


INV_BASE = 16


def _inverse_minus_identity(pms, row, col, chunk):
    base = (row // INV_BASE) == (col // INV_BASE)
    diag = row == col
    ms = [jnp.where(base, pm, 0.0) for pm in pms]
    ts = [jnp.where(diag, 1.0, m) for m in ms]
    for _ in range(INV_BASE.bit_length() - 2):
        mbs = [m.astype(BF16) for m in ms]
        ms = [_dot(mb, mb) for mb in mbs]
        ts = [t + _dot(m.astype(BF16), t.astype(BF16)) for m, t in zip(ms, ts)]
    bs = INV_BASE
    while bs < chunk:
        inner = (row // bs) == (col // bs)
        outer = (row // (2 * bs)) == (col // (2 * bs))
        sib = outer & jnp.logical_not(inner)
        xs = [_dot(jnp.where(sib, pm, 0.0).astype(BF16), t.astype(BF16)) for pm, t in zip(pms, ts)]
        ts = [t + _dot(t.astype(BF16), x.astype(BF16)) for t, x in zip(ts, xs)]
        bs *= 2
    rs = [jnp.where(diag, 0.0, t) for t in ts]
    return rs


def _gdn_kernel(q_ref, k_ref, v_ref, z_ref, ab_ref, hist_ref, cw_ref, alog_ref, dt_ref, ng_ref, s0_ref,
                o_ref, sout_ref, s_scr, carry_scr, ext_scr, conv_scr, *, tt, seg, chunk, carry):
    i = pl.program_id(0)
    nseg = tt // seg
    nblk = LANE // chunk
    ngroups = tt // LANE
    hist_rows = CONV_W - 1

    def init():
        for s in range(nseg):
            s_scr[s] = s0_ref[s]
            carry_scr[s] = jnp.zeros((8, 3 * GDN_QK), F32)
            carry_scr[s, 8 - hist_rows:8, :] = hist_ref[s]

    if carry:
        pl.when(i == 0)(init)
    else:
        init()

    for s in range(nseg):
        ext_scr[0:8, :] = carry_scr[s]
        for cg, x_ref in enumerate((q_ref, k_ref, v_ref)):
            ext_scr[8:8 + seg, cg * GDN_QK:(cg + 1) * GDN_QK] = x_ref[s * seg:(s + 1) * seg, :]
        if carry:
            carry_scr[s] = ext_scr[seg:seg + 8, :]
        base = 8 - hist_rows
        for cb in range(3 * GDN_HEADS):
            lanes = slice(cb * LANE, (cb + 1) * LANE)
            y = ext_scr[base:base + seg, lanes] * cw_ref[0:1, lanes]
            for tap in range(1, CONV_W):
                y = y + ext_scr[base + tap:base + tap + seg, lanes] * cw_ref[tap:tap + 1, lanes]
            conv_scr[s * seg:(s + 1) * seg, lanes] = y * _sigmoid(y)

    row = lax.broadcasted_iota(jnp.int32, (LANE, LANE), 0)
    col = lax.broadcasted_iota(jnp.int32, (LANE, LANE), 1)
    same = (row // chunk) == (col // chunk)
    incl = same & (row >= col)
    strict = same & (row > col)
    lmask = jnp.where(incl, 1.0, 0.0).astype(BF16)
    row2 = jnp.concatenate([row, row], axis=0)

    inst = [(gi, hh) for gi in range(ngroups) for hh in range(GDN_HEADS)]
    heads = range(GDN_HEADS)

    gates = []
    for gi in range(ngroups):
        ab = ab_ref[gi * LANE:(gi + 1) * LANE, :]
        xa = ab + dt_ref[...]
        softplus = jnp.maximum(xa, 0.0) + jnp.log1p(jnp.exp(-jnp.abs(xa)))
        la = -jnp.exp(alog_ref[...]) * softplus
        beta = _sigmoid(ab)
        g_all = sum(_dot(lmask, piece) for piece in _split3(la))
        gates.append((g_all, g_all.T, beta))

    qn_l, kn_l, v_l, gb_l, bb_l, decay_l = [], [], [], [], [], []
    for gi, hh in inst:
        rows = slice(gi * LANE, (gi + 1) * LANE)
        q = conv_scr[rows, hh * LANE:(hh + 1) * LANE]
        k = conv_scr[rows, GDN_QK + hh * LANE:GDN_QK + (hh + 1) * LANE]
        v_l.append(conv_scr[rows, 2 * GDN_QK + hh * LANE:2 * GDN_QK + (hh + 1) * LANE])
        qn_l.append(q * (lax.rsqrt(jnp.sum(q * q, axis=-1, keepdims=True) + EPS) * (GDN_DK ** -0.5)))
        kn_l.append(k * lax.rsqrt(jnp.sum(k * k, axis=-1, keepdims=True) + EPS))
        g_all, g_all_t, beta = gates[gi]
        gb = jnp.broadcast_to(g_all[:, hh:hh + 1], (LANE, LANE))
        grow = jnp.broadcast_to(g_all_t[hh:hh + 1, :], (LANE, LANE))
        gb_l.append(gb)
        bb_l.append(jnp.broadcast_to(beta[:, GDN_HEADS + hh:GDN_HEADS + hh + 1], (LANE, LANE)))
        decay_l.append(jnp.exp(jnp.where(incl, gb - grow, -jnp.inf)))

    knb_l = [kn.astype(BF16) for kn in kn_l]
    cat_l = [_dot_nt(jnp.concatenate([qn.astype(BF16), knb], axis=0), knb)
             for qn, knb in zip(qn_l, knb_l)]
    qk_l = [cat[:LANE] * decay for cat, decay in zip(cat_l, decay_l)]
    pm_l = [jnp.where(strict, -(bb * cat[LANE:] * decay), 0.0)
            for bb, cat, decay in zip(bb_l, cat_l, decay_l)]
    r_l = _inverse_minus_identity(pm_l, row, col, chunk)
    eg_l = [jnp.exp(gb) for gb in gb_l]
    rhs_l = [jnp.concatenate([v * bb, kn * (bb * eg)], axis=1)
             for v, bb, kn, eg in zip(v_l, bb_l, kn_l, eg_l)]
    sol_l = [rhs + _dot(r.astype(BF16), rhs.astype(BF16)) for r, rhs in zip(r_l, rhs_l)]
    wq_l = [jnp.concatenate([sol[:, GDN_DV:], qn * eg], axis=0).astype(BF16)
            for sol, qn, eg in zip(sol_l, qn_l, eg_l)]
    kd_l = []
    for kn, gb in zip(kn_l, gb_l):
        glast = gb[chunk - 1:chunk, :]
        for blk in range(1, nblk):
            glast = jnp.where(row >= blk * chunk, gb[(blk + 1) * chunk - 1:(blk + 1) * chunk, :], glast)
        kd_l.append(kn * jnp.exp(glast - gb))

    nstate = 1 if carry else nblk
    state = [[s_scr[sidx, hh] for hh in heads] for sidx in range(nstate)]
    for gi in range(ngroups):
        ix = [gi * GDN_HEADS + hh for hh in heads]
        ws_l = []
        for hh in heads:
            ws = None
            for blk in range(nblk):
                part = _dot(wq_l[ix[hh]], state[0 if carry else blk][hh].astype(BF16))
                ws = part if ws is None else jnp.where(row2 >= blk * chunk, part, ws)
            ws_l.append(ws)
        vnb_l = [(sol_l[ix[hh]][:, :GDN_DV] - ws_l[hh][:LANE]).astype(BF16) for hh in heads]
        o_l = [ws_l[hh][LANE:] + _dot(qk_l[ix[hh]].astype(BF16), vnb_l[hh]) for hh in heads]
        for blk in range(nblk):
            sidx = 0 if carry else blk
            new = []
            for hh in heads:
                kd = kd_l[ix[hh]]
                gb = gb_l[ix[hh]]
                kdb = kd if nblk == 1 else jnp.where((row // chunk) == blk, kd, 0.0)
                gtot = jnp.exp(gb[(blk + 1) * chunk - 1:(blk + 1) * chunk, :])
                new.append(state[sidx][hh] * gtot + _dot_tn(kdb.astype(BF16), vnb_l[hh]))
            state[sidx] = new
        for hh in heads:
            rows = slice(gi * LANE, (gi + 1) * LANE)
            lanes = slice(hh * LANE, (hh + 1) * LANE)
            z = z_ref[rows, lanes]
            o_ref[rows, lanes] = (_rms(o_l[hh], ng_ref[...]) * (z * _sigmoid(z))).astype(BF16)
    for sidx in range(nstate):
        for hh in heads:
            s_scr[sidx, hh] = state[sidx][hh]

    def fin():
        for s in range(nseg):
            sout_ref[s] = s_scr[s]

    if carry:
        pl.when(i == pl.num_programs(0) - 1)(fin)
    else:
        fin()


def _gdn(proj, hist, s0, conv_w, alog_row, dt_row, norm_g, t_rows, tt, seg, chunk, carry):
    nt = t_rows // tt
    nseg = tt // seg
    nseq = s0.shape[0]
    if carry:
        seq_idx = lambda i: 0
    else:
        seq_idx = lambda i: i

    def col(c0):
        return pl.BlockSpec((tt, GDN_QK), lambda i: (i, c0 // GDN_HEADS))

    row_spec = pl.BlockSpec((1, LANE), lambda i: (0, 0))
    sspec = pl.BlockSpec((nseg, GDN_HEADS, GDN_DK, GDN_DV), lambda i: (seq_idx(i), 0, 0, 0))
    return pl.pallas_call(
        functools.partial(_gdn_kernel, tt=tt, seg=seg, chunk=chunk, carry=carry),
        grid=(nt,),
        in_specs=[
            col(COL_QB), col(COL_KB), col(COL_VB), col(COL_ZB),
            pl.BlockSpec((tt, LANE), lambda i: (i, COL_AB)),
            pl.BlockSpec((nseg, CONV_W - 1, 3 * GDN_QK), lambda i: (seq_idx(i), 0, 0)),
            pl.BlockSpec((CONV_W, 3 * GDN_QK), lambda i: (0, 0)),
            row_spec, row_spec, row_spec,
            sspec,
        ],
        out_specs=[
            pl.BlockSpec((tt, GDN_V), lambda i: (i, 0)),
            sspec,
        ],
        out_shape=[
            jax.ShapeDtypeStruct((t_rows, GDN_V), BF16),
            jax.ShapeDtypeStruct((nseq, GDN_HEADS, GDN_DK, GDN_DV), F32),
        ],
        scratch_shapes=[
            pltpu.VMEM((nseg, GDN_HEADS, GDN_DK, GDN_DV), F32),
            pltpu.VMEM((nseg, 8, 3 * GDN_QK), F32),
            pltpu.VMEM((8 + seg, 3 * GDN_QK), F32),
            pltpu.VMEM((tt, 3 * GDN_QK), F32),
        ],
        compiler_params=pltpu.CompilerParams(dimension_semantics=("arbitrary",), vmem_limit_bytes=VMEM_LIMIT),
        name="gdn",
    )(proj, proj, proj, proj, proj, hist, conv_w, alog_row, dt_row, norm_g, s0)


def _mix_kernel(x_ref, oa_ref, ob_ref, mk_ref, mv_ref, gpre_ref, wq_ref, bq_ref, wb_ref, wo_ref, gp_ref, o_ref):
    x = x_ref[...]
    tm = x.shape[0]
    nmem = mk_ref.shape[0]
    rpm = tm // nmem
    h = _rms(x, gpre_ref[...]).astype(BF16)
    qc = ((_dot(h, wq_ref[:, :MEM_Q]) + bq_ref[:, :MEM_Q]) * (MEM_HEAD_DIM ** -0.5)).astype(BF16)
    blocks = []
    for b in range(nmem):
        heads = []
        for hh in range(MEM_HEADS):
            sl = slice(hh * MEM_HEAD_DIM, (hh + 1) * MEM_HEAD_DIM)
            s = _dot_nt(qc[b * rpm:(b + 1) * rpm, sl], mk_ref[b, :, sl])
            m = jnp.max(s, axis=-1, keepdims=True)
            e = jnp.exp(s - m)
            p = e * (1.0 / jnp.sum(e, axis=-1, keepdims=True))
            heads.append(_dot(p.astype(BF16), mv_ref[b, :, sl]).astype(BF16))
        blocks.append(jnp.concatenate(heads, axis=1))
    oc = blocks[0] if nmem == 1 else jnp.concatenate(blocks, axis=0)
    merged = None
    for n, br_in in enumerate((oa_ref[...], ob_ref[...], oc)):
        cols = slice(MEM_Q + n * D_MODEL, MEM_Q + (n + 1) * D_MODEL)
        gate = _sigmoid(_dot(h, wq_ref[:, cols]) + bq_ref[:, cols])
        term = gate * _dot(br_in, wb_ref[n])
        merged = term if merged is None else merged + term
    y = _dot(merged.astype(BF16), wo_ref[...])
    o_ref[...] = x + _rms(y, gp_ref[...])


def _mix(x, oa, ob, mk, mv, gpre, wq, bq, wb, wo, gp, tm, rows_per_mem):
    m = x.shape[0]
    tile = pl.BlockSpec((tm, D_MODEL), lambda i: (i, 0))
    vec = pl.BlockSpec((1, D_MODEL), lambda i: (0, 0))
    if mk.shape[0] == 1:
        mspec = _resident((1, N_MEM, MEM_Q))
    else:
        mspec = pl.BlockSpec((tm // rows_per_mem, N_MEM, MEM_Q), lambda i: (i, 0, 0))
    nq = MEM_Q + N_BRANCH * D_MODEL
    return pl.pallas_call(
        _mix_kernel,
        grid=(m // tm,),
        in_specs=[
            tile, tile, tile, mspec, mspec, vec,
            _resident((D_MODEL, nq)),
            _resident((1, nq)),
            _resident((N_BRANCH, D_MODEL, D_MODEL)),
            _resident((D_MODEL, D_MODEL)),
            vec,
        ],
        out_specs=tile,
        out_shape=jax.ShapeDtypeStruct((m, D_MODEL), F32),
        compiler_params=pltpu.CompilerParams(dimension_semantics=("parallel",), vmem_limit_bytes=VMEM_LIMIT),
        name="mix",
    )(x, oa, ob, mk, mv, gpre, wq, bq, wb, wo, gp)


FF_CHUNK = 1024


def _ffn_kernel(x_ref, g1_ref, wu_ref, wd_ref, g2_ref, o_ref):
    x = x_ref[...]
    hf = _rms(x, g1_ref[...]).astype(BF16)
    f = None
    for c in range(D_FF // FF_CHUNK):
        sl = slice(c * FF_CHUNK, (c + 1) * FF_CHUNK)
        up = jnp.maximum(_dot(hf, wu_ref[:, sl]), 0.0)
        part = _dot((up * up).astype(BF16), wd_ref[sl, :])
        f = part if f is None else f + part
    o_ref[...] = x + _rms(f, g2_ref[...])


def _ffn(x, g1, wu, wd, g2, tm):
    m = x.shape[0]
    tile = pl.BlockSpec((tm, D_MODEL), lambda i: (i, 0))
    vec = pl.BlockSpec((1, D_MODEL), lambda i: (0, 0))
    return pl.pallas_call(
        _ffn_kernel,
        grid=(m // tm,),
        in_specs=[tile, vec, _resident((D_MODEL, D_FF)), _resident((D_FF, D_MODEL)), vec],
        out_specs=tile,
        out_shape=jax.ShapeDtypeStruct((m, D_MODEL), F32),
        compiler_params=pltpu.CompilerParams(dimension_semantics=("parallel",), vmem_limit_bytes=VMEM_LIMIT),
        name="ffn",
    )(x, g1, wu, wd, g2)


def _regroup_cols(w):
    parts = [w[..., O_QB:O_AB], w[..., :O_KA], w[..., O_KA:O_QB], w[..., O_AB:O_QC]]
    out = jnp.concatenate(parts, axis=-1)
    pad = N_PROJ - out.shape[-1]
    return jnp.pad(out, [(0, 0)] * (out.ndim - 1) + [(0, pad)])


def _rope_tables(pos0, nt, tq):
    half = SWA_HEAD_DIM // 2
    inv = ROPE_THETA ** (-jnp.arange(half, dtype=F32) / half)
    inv = jnp.tile(inv, LANE // half)
    sgn = jnp.tile(jnp.concatenate([-jnp.ones((half,), F32), jnp.ones((half,), F32)]), LANE // SWA_HEAD_DIM)
    base = (pos0 + tq * jnp.arange(nt, dtype=jnp.int32)).astype(F32)[:, None] * inv[None, :]
    off = jnp.arange(-WINDOW, tq, dtype=jnp.int32).astype(F32)[:, None] * inv[None, :]
    rope_base = jnp.stack([jnp.cos(base), sgn * jnp.sin(base)], axis=1)
    rope_off = jnp.stack([jnp.cos(off), sgn * jnp.sin(off)], axis=0)
    return rope_base, rope_off


def _pad_row(v, offset=0):
    return jnp.zeros((1, LANE), F32).at[0, offset:offset + v.shape[0]].set(v)


def _layer(x, nb, t, pos0, k_hist, v_hist, s0, conv_hist, mk, mv, wts, cfg):
    proj = _in_proj(x, wts['g_pre_mix'], wts['w_in'], wts['b_in'], cfg['tm_proj'], cfg['tn_proj'])
    rope_base, rope_off = _rope_tables(pos0, t // cfg['tq'], cfg['tq'])
    oa, k_rot = _swa(proj, k_hist, v_hist, rope_base, rope_off, wts['sinks'], nb, t, cfg['tq'], cfg['cache'])
    ob, s_new = _gdn(proj, conv_hist, s0, wts['conv_w'], wts['alog'], wts['dt'], wts['norm_g'],
                     nb * t, cfg['tt_gdn'], cfg['seg'], cfg['chunk'], cfg['carry'])
    x1 = _mix(x, oa, ob, mk, mv, wts['g_pre_mix'], wts['w_q'], wts['b_q'], wts['w_branch'], wts['w_out'],
              wts['g_post_mix'], cfg['tm_mix'], t)
    y = _ffn(x1, wts['g_pre_ffn'], wts['w_up'], wts['w_down'], wts['g_post_ffn'], cfg['tm'])
    return y, k_rot, s_new, proj


def kernel(x_prompt, x_sample, mem_prompt, cache_swa_k, cache_swa_v, state_gdn, state_conv, cache_mem_k, cache_mem_v, g_pre_mix, w_in, b_in, swa_sinks, conv_w, gdn_a_log, gdn_dt_bias, gdn_norm_g, g_mem, w_mem_kv, w_branch, w_out, g_post_mix, g_pre_ffn, w_up, w_down, g_post_ffn):
    depth = w_in.shape[0]
    assert depth == 1
    l = 0
    bp, tp, _ = x_prompt.shape
    bs, ts, _ = x_sample.shape
    assert bp == 1

    wts = {
        'g_pre_mix': g_pre_mix[l][None, :],
        'w_in': _regroup_cols(w_in[l]).astype(BF16),
        'b_in': _regroup_cols(b_in[l])[None, :],
        'w_q': w_in[l][:, O_QC:O_END].astype(BF16),
        'b_q': b_in[l][None, O_QC:O_END],
        'sinks': swa_sinks[l],
        'conv_w': conv_w[l],
        'alog': _pad_row(gdn_a_log[l]),
        'dt': _pad_row(gdn_dt_bias[l]),
        'norm_g': gdn_norm_g[l][None, :],
        'w_branch': w_branch[l].astype(BF16),
        'w_out': w_out[l].astype(BF16),
        'g_post_mix': g_post_mix[l][None, :],
        'g_pre_ffn': g_pre_ffn[l][None, :],
        'w_up': w_up[l].astype(BF16),
        'w_down': w_down[l].astype(BF16),
        'g_post_ffn': g_post_ffn[l][None, :],
    }

    mkv, mkv_bf = _mem_kv(mem_prompt[0], g_mem[l][None, :], w_mem_kv[l].astype(BF16))
    cfg_p = dict(tm_proj=2048, tn_proj=512, tq=512, cache=False, tt_gdn=256, seg=256, chunk=128, carry=True,
                 tm_mix=512, tm=512)
    zero_hist = jnp.zeros((1, CONV_W - 1, 3 * GDN_QK), F32)
    zero_s = jnp.zeros((1, GDN_HEADS, GDN_DK, GDN_DV), F32)
    yp, kp_rot, sp_new, proj_p = _layer(
        x_prompt.reshape(bp * tp, D_MODEL), bp, tp, 0, None, None,
        zero_s, zero_hist, mkv_bf[None, :, :MEM_Q], mkv_bf[None, :, MEM_Q:], wts, cfg_p)

    cfg_s = dict(tm_proj=1024, tn_proj=512, tq=ts, cache=True, tt_gdn=128, seg=ts, chunk=ts, carry=False,
                 tm_mix=4 * ts, tm=512)
    ys, ks_rot, ss_new, proj_s = _layer(
        x_sample.reshape(bs * ts, D_MODEL), bs, ts, PAST_LEN,
        cache_swa_k[l].reshape(bs, WINDOW, SWA_KV), cache_swa_v[l].reshape(bs, WINDOW, SWA_KV),
        state_gdn[l], state_conv[l],
        cache_mem_k[l].astype(BF16).reshape(bs, N_MEM, MEM_Q),
        cache_mem_v[l].astype(BF16).reshape(bs, N_MEM, MEM_Q), wts, cfg_s)

    kv_shape = (SWA_KV_HEADS, SWA_HEAD_DIM)
    conv_cols = slice(COL_QB * LANE, COL_ZB * LANE)
    va_cols = slice(COL_VA * LANE, (COL_VA + 1) * LANE)

    p_k = kp_rot[tp - WINDOW:].reshape((1, bp, WINDOW) + kv_shape)
    p_v = proj_p[tp - WINDOW:, va_cols].reshape((1, bp, WINDOW) + kv_shape)
    p_c = proj_p[tp - (CONV_W - 1):, conv_cols].reshape(1, bp, CONV_W - 1, 3 * GDN_QK)
    p_mk = mkv[:, :MEM_Q].reshape(1, bp, N_MEM, MEM_HEADS, MEM_HEAD_DIM)
    p_mv = mkv[:, MEM_Q:].reshape(1, bp, N_MEM, MEM_HEADS, MEM_HEAD_DIM)

    ks_new = ks_rot.reshape(bs, ts, SWA_KV)
    vs_new = proj_s[:, va_cols].reshape(bs, ts, SWA_KV)
    s_k = jnp.concatenate([cache_swa_k[l].reshape(bs, WINDOW, SWA_KV), ks_new], axis=1)[:, -WINDOW:]
    s_v = jnp.concatenate([cache_swa_v[l].reshape(bs, WINDOW, SWA_KV), vs_new], axis=1)[:, -WINDOW:]
    s_c = proj_s.reshape(bs, ts, N_PROJ)[:, ts - (CONV_W - 1):, conv_cols]

    return (yp.reshape(bp, tp, D_MODEL), ys.reshape(bs, ts, D_MODEL),
            p_k, p_v, sp_new[None], p_c, p_mk, p_mv,
            s_k.reshape((1, bs, WINDOW) + kv_shape), s_v.reshape((1, bs, WINDOW) + kv_shape),
            ss_new[None], s_c[None])
```

```python
import functools

import jax
import jax.numpy as jnp
from jax import lax
from jax.experimental import pallas as pl
from jax.experimental.pallas import tpu as pltpu

F32 = jnp.float32
BF16 = jnp.bfloat16

D_MODEL = 1024
PAST_LEN = 2048
CHUNK = 64
N_MEM = 256
EPS = 1e-6
SWA_HEADS = 16
SWA_KV_HEADS = 2
SWA_HEAD_DIM = 64
WINDOW = 128
ROPE_THETA = 10000.0
GDN_HEADS = 8
GDN_DK = 128
GDN_DV = 128
CONV_W = 4
MEM_HEADS = 4
MEM_HEAD_DIM = 256
N_BRANCH = 3
D_FF = 4 * D_MODEL

LANE = 128
SWA_Q = SWA_HEADS * SWA_HEAD_DIM
SWA_KV = SWA_KV_HEADS * SWA_HEAD_DIM
GDN_QK = GDN_HEADS * GDN_DK
GDN_V = GDN_HEADS * GDN_DV
MEM_Q = MEM_HEADS * MEM_HEAD_DIM

COL_QB = 0
COL_KB = 8
COL_VB = 16
COL_ZB = 24
COL_QA = 32
COL_KA = 40
COL_VA = 41
COL_AB = 42
N_PROJ = 44 * LANE
O_KA = SWA_Q
O_QB = O_KA + 2 * SWA_KV
O_AB = O_QB + 2 * GDN_QK + 2 * GDN_V
O_QC = O_AB + 2 * GDN_HEADS
O_END = O_QC + MEM_Q + N_BRANCH * D_MODEL

VMEM_LIMIT = 48 * 1024 * 1024


def _dot(a, b):
    return jnp.dot(a, b, preferred_element_type=F32)


def _dot_nt(a, b):
    return lax.dot_general(a, b, (((1,), (1,)), ((), ())), preferred_element_type=F32)


def _dot_tn(a, b):
    return lax.dot_general(a, b, (((0,), (0,)), ((), ())), preferred_element_type=F32)


def _rms(x, g):
    return x * lax.rsqrt(jnp.mean(x * x, axis=-1, keepdims=True) + EPS) * g


def _sigmoid(x):
    return 1.0 / (1.0 + jnp.exp(-x))


def _resident(shape):
    return pl.BlockSpec(shape, lambda *_: (0,) * len(shape), pipeline_mode=pl.Buffered(1))


def _split3(x):
    hi = x.astype(BF16)
    r = x - hi.astype(F32)
    mid = r.astype(BF16)
    lo = (r - mid.astype(F32)).astype(BF16)
    return hi, mid, lo


def _mem_kv_kernel(m_ref, g_ref, w_ref, o_ref, ob_ref):
    h = _rms(m_ref[...], g_ref[...]).astype(BF16)
    kv = _dot(h, w_ref[...])
    o_ref[...] = kv
    ob_ref[...] = kv.astype(BF16)


def _mem_kv(mem, g_mem, w_bf16):
    n = w_bf16.shape[1]
    tn = 1024
    return pl.pallas_call(
        _mem_kv_kernel,
        grid=(n // tn,),
        in_specs=[
            pl.BlockSpec((N_MEM, D_MODEL), lambda j: (0, 0)),
            pl.BlockSpec((1, D_MODEL), lambda j: (0, 0)),
            pl.BlockSpec((D_MODEL, tn), lambda j: (0, j)),
        ],
        out_specs=[pl.BlockSpec((N_MEM, tn), lambda j: (0, j))] * 2,
        out_shape=[jax.ShapeDtypeStruct((N_MEM, n), F32), jax.ShapeDtypeStruct((N_MEM, n), BF16)],
        compiler_params=pltpu.CompilerParams(dimension_semantics=("arbitrary",), vmem_limit_bytes=VMEM_LIMIT),
        name="mem_kv",
    )(mem, g_mem, w_bf16)


def _rope(x, cos, sin_signed):
    lane = lax.broadcasted_iota(jnp.int32, x.shape, 1)
    first = (lane % SWA_HEAD_DIM) < (SWA_HEAD_DIM // 2)
    rot = jnp.where(first, pltpu.roll(x, LANE - SWA_HEAD_DIM // 2, 1), pltpu.roll(x, SWA_HEAD_DIM // 2, 1))
    return x * cos + rot * sin_signed


SWA_KA = SWA_Q // LANE
SWA_VA = SWA_KA + 1
KEYS = WINDOW + CHUNK
KPAD = 256


def _swa_kernel(sink_ref, q_ref, k_ref, v_ref, kh_ref, vh_ref, base_ref, off_ref,
                o_ref, kout_ref, qs_ref, ka_ref, kb_ref, va_ref, vb_ref, *, tq, cache):
    i = pl.program_id(1)
    cb, sb = base_ref[0:1, :], base_ref[1:2, :]
    co, so = off_ref[0], off_ref[1]
    cos_all = cb * co - sb * so
    sin_all = sb * co + cb * so
    cos, sin = cos_all[WINDOW:], sin_all[WINDOW:]
    kr = _rope(k_ref[...], cos, sin)
    kout_ref[...] = kr
    if cache:
        khr = kh_ref[...]
    else:
        khr = _rope(kh_ref[...], cos_all[:WINDOW], sin_all[:WINDOW])
    zpad = jnp.zeros((KPAD - KEYS, LANE), F32)
    kall = jnp.concatenate([khr, kr, zpad], axis=0)
    vall = jnp.concatenate([vh_ref[...], v_ref[...], zpad], axis=0)
    lo = lax.broadcasted_iota(jnp.int32, kall.shape, 1) < SWA_HEAD_DIM
    zero = jnp.zeros_like(kall)
    for src, a_ref, b_ref in ((kall, ka_ref, kb_ref), (vall, va_ref, vb_ref)):
        rolled = pltpu.roll(src, SWA_HEAD_DIM, 1)
        a_ref[0] = jnp.where(lo, src, zero).astype(BF16)
        b_ref[0] = jnp.where(lo, zero, rolled).astype(BF16)
        a_ref[1] = jnp.where(lo, rolled, zero).astype(BF16)
        b_ref[1] = jnp.where(lo, zero, src).astype(BF16)

    scale = SWA_HEAD_DIM ** -0.5
    for p in range(SWA_HEADS // 2):
        sl = slice(p * LANE, (p + 1) * LANE)
        qs_ref[:, sl] = (_rope(q_ref[:, sl], cos, sin) * scale).astype(BF16)

    jj = lax.broadcasted_iota(jnp.int32, (CHUNK, KPAD), 1)
    pairs_per_kv = SWA_HEADS // SWA_KV_HEADS // 2
    inst = [(c, g) for c in range(tq // CHUNK) for g in range(SWA_KV_HEADS)]

    def scores(c, g):
        rows = slice(c * CHUNK, (c + 1) * CHUNK)
        krows = slice(c * CHUNK, c * CHUNK + KPAD)
        kbd = jnp.concatenate([ka_ref[g, krows, :], kb_ref[g, krows, :]], axis=0)
        qst = jnp.concatenate(
            [qs_ref[rows, (g * pairs_per_kv + j) * LANE:(g * pairs_per_kv + j + 1) * LANE]
             for j in range(pairs_per_kv)], axis=0)
        return _dot_nt(qst, kbd)

    def probs(c, g, s):
        if cache:
            valid = jj < KEYS
        else:
            valid = (jj < KEYS) & ((i > 0) | (jj + c * CHUNK >= WINDOW))
        pblocks = []
        for j in range(pairs_per_kv):
            halves = []
            for half in range(2):
                sh = s[j * CHUNK:(j + 1) * CHUNK, half * KPAD:(half + 1) * KPAD]
                sh = jnp.where(valid, sh, -jnp.inf)
                sink = sink_ref[2 * (g * pairs_per_kv + j) + half]
                m = jnp.maximum(jnp.max(sh, axis=-1, keepdims=True), sink)
                e = jnp.exp(sh - m)
                den = jnp.sum(e, axis=-1, keepdims=True) + jnp.exp(sink - m)
                halves.append((e * (1.0 / den)).astype(BF16))
            pblocks.append(jnp.concatenate(halves, axis=1))
        return jnp.concatenate(pblocks, axis=0)

    def outputs(c, g, pmat):
        rows = slice(c * CHUNK, (c + 1) * CHUNK)
        krows = slice(c * CHUNK, c * CHUNK + KPAD)
        vbd = jnp.concatenate([va_ref[g, krows, :], vb_ref[g, krows, :]], axis=0)
        o = _dot(pmat, vbd)
        for j in range(pairs_per_kv):
            p = g * pairs_per_kv + j
            o_ref[rows, p * LANE:(p + 1) * LANE] = o[j * CHUNK:(j + 1) * CHUNK].astype(BF16)

    s_l, p_l = {}, {}
    for step in range(len(inst) + 2):
        if step < len(inst):
            s_l[step] = scores(*inst[step])
        if 0 <= step - 1 < len(inst):
            p_l[step - 1] = probs(*inst[step - 1], s_l.pop(step - 1))
        if 0 <= step - 2 < len(inst):
            outputs(*inst[step - 2], p_l.pop(step - 2))


def _swa(proj, k_hist, v_hist, rope_base, rope_off, sinks, nb, t, tq, cache):
    nt = t // tq
    kv_rows = WINDOW + tq + (KPAD - KEYS)
    if cache:
        hist_k_spec = pl.BlockSpec((None, WINDOW, LANE), lambda b, i: (b, 0, 0))
        hist_v_spec = pl.BlockSpec((None, WINDOW, LANE), lambda b, i: (b, 0, 0))
        hk, hv = k_hist, v_hist
    else:
        r = tq // WINDOW

        def hidx(b, i):
            return jnp.maximum((b * nt + i) * r - 1, 0)

        hist_k_spec = pl.BlockSpec((WINDOW, LANE), lambda b, i: (hidx(b, i), SWA_KA))
        hist_v_spec = pl.BlockSpec((WINDOW, LANE), lambda b, i: (hidx(b, i), SWA_VA))
        hk, hv = proj, proj
    return pl.pallas_call(
        functools.partial(_swa_kernel, tq=tq, cache=cache),
        grid=(nb, nt),
        in_specs=[
            pl.BlockSpec(memory_space=pltpu.SMEM),
            pl.BlockSpec((tq, SWA_Q), lambda b, i: (b * nt + i, 0)),
            pl.BlockSpec((tq, LANE), lambda b, i: (b * nt + i, SWA_KA)),
            pl.BlockSpec((tq, LANE), lambda b, i: (b * nt + i, SWA_VA)),
            hist_k_spec,
            hist_v_spec,
            pl.BlockSpec((None, 2, LANE), lambda b, i: (i, 0, 0)),
            pl.BlockSpec((2, WINDOW + tq, LANE), lambda b, i: (0, 0, 0)),
        ],
        out_specs=[
            pl.BlockSpec((tq, SWA_Q), lambda b, i: (b * nt + i, 0)),
            pl.BlockSpec((tq, LANE), lambda b, i: (b * nt + i, 0)),
        ],
        out_shape=[
            jax.ShapeDtypeStruct((nb * t, SWA_Q), BF16),
            jax.ShapeDtypeStruct((nb * t, LANE), F32),
        ],
        scratch_shapes=[
            pltpu.VMEM((tq, SWA_Q), BF16),
            pltpu.VMEM((SWA_KV_HEADS, kv_rows, LANE), BF16),
            pltpu.VMEM((SWA_KV_HEADS, kv_rows, LANE), BF16),
            pltpu.VMEM((SWA_KV_HEADS, kv_rows, LANE), BF16),
            pltpu.VMEM((SWA_KV_HEADS, kv_rows, LANE), BF16),
        ],
        compiler_params=pltpu.CompilerParams(dimension_semantics=("parallel", "arbitrary"),
                                             vmem_limit_bytes=VMEM_LIMIT),
        name="swa",
    )(sinks, proj, proj, proj, hk, hv, rope_base, rope_off)


PROJ_CHUNK = 256
GDN_VMEM_LIMIT = 58 * 1024 * 1024
INV_BASE = 16


def _inverse_minus_identity(pms, row, col, chunk, between=lambda: None):
    base = (row // INV_BASE) == (col // INV_BASE)
    diag = row == col
    ms = [jnp.where(base, pm, 0.0) for pm in pms]
    ts = [jnp.where(diag, 1.0, m) for m in ms]
    for _ in range(INV_BASE.bit_length() - 2):
        mbs = [m.astype(BF16) for m in ms]
        ms = [_dot(mb, mb) for mb in mbs]
        between()
        ts = [t + _dot(m.astype(BF16), t.astype(BF16)) for m, t in zip(ms, ts)]
    bs = INV_BASE
    while bs < chunk:
        inner = (row // bs) == (col // bs)
        outer = (row // (2 * bs)) == (col // (2 * bs))
        sib = outer & jnp.logical_not(inner)
        xs = [_dot(jnp.where(sib, pm, 0.0).astype(BF16), t.astype(BF16)) for pm, t in zip(pms, ts)]
        ts = [t + _dot(t.astype(BF16), x.astype(BF16)) for t, x in zip(ts, xs)]
        between()
        bs *= 2
    rs = [jnp.where(diag, 0.0, t) for t in ts]
    return rs


def _gdn_kernel(x0_ref, xn_ref, gpre_ref, w_ref, b_ref, hist_ref, cw_ref, alog_ref, dt_ref, ng_ref, s0_ref,
                o_ref, sout_ref, swa_ref, clast_ref, proj_scr, h_scr, s_scr, carry_scr, ext_scr, conv_scr,
                *, tt, seg, chunk, carry):
    i = pl.program_id(0)
    nseg = tt // seg
    nblk = LANE // chunk
    ngroups = tt // LANE
    hist_rows = CONV_W - 1
    cur = proj_scr.at[i % 2]
    nxt = proj_scr.at[1 - i % 2]

    def project(dst, c):
        cols = slice(c * PROJ_CHUNK, (c + 1) * PROJ_CHUNK)
        dst[8:8 + tt, cols] = _dot(h_scr[...], w_ref[:, cols]) + b_ref[:, cols]

    @pl.when(i == 0)
    def _():
        h_scr[...] = _rms(x0_ref[...], gpre_ref[...]).astype(BF16)
        for c in range(N_PROJ // PROJ_CHUNK):
            project(proj_scr.at[0], c)

    h_scr[...] = _rms(xn_ref[...], gpre_ref[...]).astype(BF16)
    pending = list(range(N_PROJ // PROJ_CHUNK))

    def emit(n=1):
        for _ in range(n):
            if pending:
                project(nxt, pending.pop(0))

    def init():
        for s in range(nseg):
            s_scr[s] = s0_ref[s]
            carry_scr[s] = jnp.zeros((8, 3 * GDN_QK), F32)
            carry_scr[s, 8 - hist_rows:8, :] = hist_ref[s]

    if carry:
        pl.when(i == 0)(init)
    else:
        init()

    swa_ref[...] = cur[8:8 + tt, COL_QA * LANE:(COL_VA + 1) * LANE]

    qkv = slice(COL_QB * LANE, COL_ZB * LANE)
    base = 8 - hist_rows
    for s in range(nseg):
        if carry:
            cur[0:8, qkv] = carry_scr[s]
            src, off = cur, 0
            last = cur[tt:tt + 8, qkv]
            carry_scr[s] = last
        else:
            ext_scr[0:8, :] = carry_scr[s]
            ext_scr[8:8 + seg, :] = cur[8 + s * seg:8 + (s + 1) * seg, qkv]
            src, off = ext_scr, 0
            last = ext_scr[seg:seg + 8, :]
        clast_ref[s] = last
        for cb in range(3 * GDN_HEADS):
            lanes = slice(cb * LANE, (cb + 1) * LANE)
            y = src[off + base:off + base + seg, lanes] * cw_ref[0:1, lanes]
            for tap in range(1, CONV_W):
                y = y + src[off + base + tap:off + base + tap + seg, lanes] * cw_ref[tap:tap + 1, lanes]
            conv_scr[s * seg:(s + 1) * seg, lanes] = y * _sigmoid(y)
            if s == 0 and cb % 2 == 0:
                emit()

    row = lax.broadcasted_iota(jnp.int32, (LANE, LANE), 0)
    col = lax.broadcasted_iota(jnp.int32, (LANE, LANE), 1)
    same = (row // chunk) == (col // chunk)
    incl = same & (row >= col)
    strict = same & (row > col)
    lmask = jnp.where(incl, 1.0, 0.0).astype(BF16)
    row2 = jnp.concatenate([row, row], axis=0)

    inst = [(gi, hh) for gi in range(ngroups) for hh in range(GDN_HEADS)]
    heads = range(GDN_HEADS)

    gates = []
    for gi in range(ngroups):
        ab = cur[8 + gi * LANE:8 + (gi + 1) * LANE, COL_AB * LANE:(COL_AB + 1) * LANE]
        xa = ab + dt_ref[...]
        softplus = jnp.maximum(xa, 0.0) + jnp.log1p(jnp.exp(-jnp.abs(xa)))
        la = -jnp.exp(alog_ref[...]) * softplus
        beta = _sigmoid(ab)
        g_all = sum(_dot(lmask, piece) for piece in _split3(la))
        gates.append((g_all, g_all.T, beta))

    qn_l, kn_l, v_l, gb_l, bb_l, decay_l = [], [], [], [], [], []
    for gi, hh in inst:
        rows = slice(gi * LANE, (gi + 1) * LANE)
        q = conv_scr[rows, hh * LANE:(hh + 1) * LANE]
        k = conv_scr[rows, GDN_QK + hh * LANE:GDN_QK + (hh + 1) * LANE]
        v_l.append(conv_scr[rows, 2 * GDN_QK + hh * LANE:2 * GDN_QK + (hh + 1) * LANE])
        qn_l.append(q * (lax.rsqrt(jnp.sum(q * q, axis=-1, keepdims=True) + EPS) * (GDN_DK ** -0.5)))
        kn_l.append(k * lax.rsqrt(jnp.sum(k * k, axis=-1, keepdims=True) + EPS))
        g_all, g_all_t, beta = gates[gi]
        gb = jnp.broadcast_to(g_all[:, hh:hh + 1], (LANE, LANE))
        grow = jnp.broadcast_to(g_all_t[hh:hh + 1, :], (LANE, LANE))
        gb_l.append(gb)
        bb_l.append(jnp.broadcast_to(beta[:, GDN_HEADS + hh:GDN_HEADS + hh + 1], (LANE, LANE)))
        decay_l.append(jnp.exp(jnp.where(incl, gb - grow, -jnp.inf)))

    emit()
    knb_l = [kn.astype(BF16) for kn in kn_l]
    cat_l = [_dot_nt(jnp.concatenate([qn.astype(BF16), knb], axis=0), knb)
             for qn, knb in zip(qn_l, knb_l)]
    emit()
    qk_l = [cat[:LANE] * decay for cat, decay in zip(cat_l, decay_l)]
    pm_l = [jnp.where(strict, -(bb * cat[LANE:] * decay), 0.0)
            for bb, cat, decay in zip(bb_l, cat_l, decay_l)]
    r_l = _inverse_minus_identity(pm_l, row, col, chunk, emit)
    emit()
    eg_l = [jnp.exp(gb) for gb in gb_l]
    rhs_l = [jnp.concatenate([v * bb, kn * (bb * eg)], axis=1)
             for v, bb, kn, eg in zip(v_l, bb_l, kn_l, eg_l)]
    sol_l = [rhs + _dot(r.astype(BF16), rhs.astype(BF16)) for r, rhs in zip(r_l, rhs_l)]
    wq_l = [jnp.concatenate([sol[:, GDN_DV:], qn * eg], axis=0).astype(BF16)
            for sol, qn, eg in zip(sol_l, qn_l, eg_l)]
    kd_l = []
    for kn, gb in zip(kn_l, gb_l):
        glast = gb[chunk - 1:chunk, :]
        for blk in range(1, nblk):
            glast = jnp.where(row >= blk * chunk, gb[(blk + 1) * chunk - 1:(blk + 1) * chunk, :], glast)
        kd_l.append(kn * jnp.exp(glast - gb))

    emit()
    nstate = 1 if carry else nblk
    state = [[s_scr[sidx, hh] for hh in heads] for sidx in range(nstate)]
    for gi in range(ngroups):
        ix = [gi * GDN_HEADS + hh for hh in heads]
        ws_l = []
        for hh in heads:
            ws = None
            for blk in range(nblk):
                part = _dot(wq_l[ix[hh]], state[0 if carry else blk][hh].astype(BF16))
                ws = part if ws is None else jnp.where(row2 >= blk * chunk, part, ws)
            ws_l.append(ws)
        vnb_l = [(sol_l[ix[hh]][:, :GDN_DV] - ws_l[hh][:LANE]).astype(BF16) for hh in heads]
        o_l = [ws_l[hh][LANE:] + _dot(qk_l[ix[hh]].astype(BF16), vnb_l[hh]) for hh in heads]
        for blk in range(nblk):
            sidx = 0 if carry else blk
            new = []
            for hh in heads:
                kd = kd_l[ix[hh]]
                gb = gb_l[ix[hh]]
                kdb = kd if nblk == 1 else jnp.where((row // chunk) == blk, kd, 0.0)
                gtot = jnp.exp(gb[(blk + 1) * chunk - 1:(blk + 1) * chunk, :])
                new.append(state[sidx][hh] * gtot + _dot_tn(kdb.astype(BF16), vnb_l[hh]))
            state[sidx] = new
        for hh in heads:
            rows = slice(gi * LANE, (gi + 1) * LANE)
            lanes = slice(hh * LANE, (hh + 1) * LANE)
            z = cur[8 + gi * LANE:8 + (gi + 1) * LANE, COL_ZB * LANE + hh * LANE:COL_ZB * LANE + (hh + 1) * LANE]
            o_ref[rows, lanes] = (_rms(o_l[hh], ng_ref[...]) * (z * _sigmoid(z))).astype(BF16)
    emit(len(pending))
    for sidx in range(nstate):
        for hh in heads:
            s_scr[sidx, hh] = state[sidx][hh]

    def fin():
        for s in range(nseg):
            sout_ref[s] = s_scr[s]

    if carry:
        pl.when(i == pl.num_programs(0) - 1)(fin)
    else:
        fin()


def _gdn(x, gpre, w_bf16, b, hist, s0, conv_w, alog_row, dt_row, norm_g, tt, seg, chunk, carry):
    t_rows = x.shape[0]
    nt = t_rows // tt
    nseg = tt // seg
    nseq = s0.shape[0]
    if carry:
        seq_idx = lambda i: 0
    else:
        seq_idx = lambda i: i
    swa_cols = (COL_VA + 1 - COL_QA) * LANE
    row_spec = pl.BlockSpec((1, LANE), lambda i: (0, 0))
    sspec = pl.BlockSpec((nseg, GDN_HEADS, GDN_DK, GDN_DV), lambda i: (seq_idx(i), 0, 0, 0))
    cspec = pl.BlockSpec((nseg, 8, 3 * GDN_QK), lambda i: (seq_idx(i), 0, 0))
    return pl.pallas_call(
        functools.partial(_gdn_kernel, tt=tt, seg=seg, chunk=chunk, carry=carry),
        grid=(nt,),
        in_specs=[
            pl.BlockSpec((tt, D_MODEL), lambda i: (0, 0)),
            pl.BlockSpec((tt, D_MODEL), lambda i: (jnp.minimum(i + 1, nt - 1), 0)),
            pl.BlockSpec((1, D_MODEL), lambda i: (0, 0)),
            _resident((D_MODEL, N_PROJ)),
            _resident((1, N_PROJ)),
            pl.BlockSpec((nseg, CONV_W - 1, 3 * GDN_QK), lambda i: (seq_idx(i), 0, 0)),
            pl.BlockSpec((CONV_W, 3 * GDN_QK), lambda i: (0, 0)),
            row_spec, row_spec, row_spec,
            sspec,
        ],
        out_specs=[
            pl.BlockSpec((tt, GDN_V), lambda i: (i, 0)),
            sspec,
            pl.BlockSpec((tt, swa_cols), lambda i: (i, 0)),
            cspec,
        ],
        out_shape=[
            jax.ShapeDtypeStruct((t_rows, GDN_V), BF16),
            jax.ShapeDtypeStruct((nseq, GDN_HEADS, GDN_DK, GDN_DV), F32),
            jax.ShapeDtypeStruct((t_rows, swa_cols), F32),
            jax.ShapeDtypeStruct((nseq, 8, 3 * GDN_QK), F32),
        ],
        scratch_shapes=[
            pltpu.VMEM((2, 8 + tt, N_PROJ), F32),
            pltpu.VMEM((tt, D_MODEL), BF16),
            pltpu.VMEM((nseg, GDN_HEADS, GDN_DK, GDN_DV), F32),
            pltpu.VMEM((nseg, 8, 3 * GDN_QK), F32),
            pltpu.VMEM((8 + seg, 3 * GDN_QK), F32),
            pltpu.VMEM((tt, 3 * GDN_QK), F32),
        ],
        compiler_params=pltpu.CompilerParams(dimension_semantics=("arbitrary",),
                                             vmem_limit_bytes=GDN_VMEM_LIMIT),
        name="gdn",
    )(x, x, gpre, w_bf16, b, hist, conv_w, alog_row, dt_row, norm_g, s0)


def _mix_kernel(x_ref, oa_ref, ob_ref, mk_ref, mv_ref, gpre_ref, wq_ref, bq_ref, wb_ref, wo_ref, gp_ref, o_ref):
    x = x_ref[...]
    tm = x.shape[0]
    nmem = mk_ref.shape[0]
    rpm = tm // nmem
    h = _rms(x, gpre_ref[...]).astype(BF16)
    qc = ((_dot(h, wq_ref[:, :MEM_Q]) + bq_ref[:, :MEM_Q]) * (MEM_HEAD_DIM ** -0.5)).astype(BF16)
    blocks = []
    for b in range(nmem):
        heads = []
        for hh in range(MEM_HEADS):
            sl = slice(hh * MEM_HEAD_DIM, (hh + 1) * MEM_HEAD_DIM)
            s = _dot_nt(qc[b * rpm:(b + 1) * rpm, sl], mk_ref[b, :, sl])
            m = jnp.max(s, axis=-1, keepdims=True)
            e = jnp.exp(s - m)
            p = e * (1.0 / jnp.sum(e, axis=-1, keepdims=True))
            heads.append(_dot(p.astype(BF16), mv_ref[b, :, sl]).astype(BF16))
        blocks.append(jnp.concatenate(heads, axis=1))
    oc = blocks[0] if nmem == 1 else jnp.concatenate(blocks, axis=0)
    merged = None
    for n, br_in in enumerate((oa_ref[...], ob_ref[...], oc)):
        cols = slice(MEM_Q + n * D_MODEL, MEM_Q + (n + 1) * D_MODEL)
        gate = _sigmoid(_dot(h, wq_ref[:, cols]) + bq_ref[:, cols])
        term = gate * _dot(br_in, wb_ref[n])
        merged = term if merged is None else merged + term
    y = _dot(merged.astype(BF16), wo_ref[...])
    o_ref[...] = x + _rms(y, gp_ref[...])


def _mix(x, oa, ob, mk, mv, gpre, wq, bq, wb, wo, gp, tm, rows_per_mem):
    m = x.shape[0]
    tile = pl.BlockSpec((tm, D_MODEL), lambda i: (i, 0))
    vec = pl.BlockSpec((1, D_MODEL), lambda i: (0, 0))
    if mk.shape[0] == 1:
        mspec = _resident((1, N_MEM, MEM_Q))
    else:
        mspec = pl.BlockSpec((tm // rows_per_mem, N_MEM, MEM_Q), lambda i: (i, 0, 0))
    nq = MEM_Q + N_BRANCH * D_MODEL
    return pl.pallas_call(
        _mix_kernel,
        grid=(m // tm,),
        in_specs=[
            tile, tile, tile, mspec, mspec, vec,
            _resident((D_MODEL, nq)),
            _resident((1, nq)),
            _resident((N_BRANCH, D_MODEL, D_MODEL)),
            _resident((D_MODEL, D_MODEL)),
            vec,
        ],
        out_specs=tile,
        out_shape=jax.ShapeDtypeStruct((m, D_MODEL), F32),
        compiler_params=pltpu.CompilerParams(dimension_semantics=("parallel",), vmem_limit_bytes=VMEM_LIMIT),
        name="mix",
    )(x, oa, ob, mk, mv, gpre, wq, bq, wb, wo, gp)


FF_CHUNK = 1024


def _ffn_kernel(x_ref, g1_ref, wu_ref, wd_ref, g2_ref, o_ref):
    x = x_ref[...]
    hf = _rms(x, g1_ref[...]).astype(BF16)
    f = None
    for c in range(D_FF // FF_CHUNK):
        sl = slice(c * FF_CHUNK, (c + 1) * FF_CHUNK)
        up = jnp.maximum(_dot(hf, wu_ref[:, sl]), 0.0)
        part = _dot((up * up).astype(BF16), wd_ref[sl, :])
        f = part if f is None else f + part
    o_ref[...] = x + _rms(f, g2_ref[...])


def _ffn(x, g1, wu, wd, g2, tm):
    m = x.shape[0]
    tile = pl.BlockSpec((tm, D_MODEL), lambda i: (i, 0))
    vec = pl.BlockSpec((1, D_MODEL), lambda i: (0, 0))
    return pl.pallas_call(
        _ffn_kernel,
        grid=(m // tm,),
        in_specs=[tile, vec, _resident((D_MODEL, D_FF)), _resident((D_FF, D_MODEL)), vec],
        out_specs=tile,
        out_shape=jax.ShapeDtypeStruct((m, D_MODEL), F32),
        compiler_params=pltpu.CompilerParams(dimension_semantics=("parallel",), vmem_limit_bytes=VMEM_LIMIT),
        name="ffn",
    )(x, g1, wu, wd, g2)


def _regroup_cols(w):
    parts = [w[..., O_QB:O_AB], w[..., :O_KA], w[..., O_KA:O_QB], w[..., O_AB:O_QC]]
    out = jnp.concatenate(parts, axis=-1)
    pad = N_PROJ - out.shape[-1]
    return jnp.pad(out, [(0, 0)] * (out.ndim - 1) + [(0, pad)])


def _rope_tables(pos0, nt, tq):
    half = SWA_HEAD_DIM // 2
    inv = ROPE_THETA ** (-jnp.arange(half, dtype=F32) / half)
    inv = jnp.tile(inv, LANE // half)
    sgn = jnp.tile(jnp.concatenate([-jnp.ones((half,), F32), jnp.ones((half,), F32)]), LANE // SWA_HEAD_DIM)
    base = (pos0 + tq * jnp.arange(nt, dtype=jnp.int32)).astype(F32)[:, None] * inv[None, :]
    off = jnp.arange(-WINDOW, tq, dtype=jnp.int32).astype(F32)[:, None] * inv[None, :]
    rope_base = jnp.stack([jnp.cos(base), sgn * jnp.sin(base)], axis=1)
    rope_off = jnp.stack([jnp.cos(off), sgn * jnp.sin(off)], axis=0)
    return rope_base, rope_off


def _pad_row(v, offset=0):
    return jnp.zeros((1, LANE), F32).at[0, offset:offset + v.shape[0]].set(v)


def _layer(x, nb, t, pos0, k_hist, v_hist, s0, conv_hist, mk, mv, wts, cfg):
    ob, s_new, swa_in, conv_last = _gdn(x, wts['g_pre_mix'], wts['w_in'], wts['b_in'], conv_hist, s0,
                                        wts['conv_w'], wts['alog'], wts['dt'], wts['norm_g'],
                                        cfg['tt_gdn'], cfg['seg'], cfg['chunk'], cfg['carry'])
    rope_base, rope_off = _rope_tables(pos0, t // cfg['tq'], cfg['tq'])
    oa, k_rot = _swa(swa_in, k_hist, v_hist, rope_base, rope_off, wts['sinks'], nb, t, cfg['tq'], cfg['cache'])
    x1 = _mix(x, oa, ob, mk, mv, wts['g_pre_mix'], wts['w_q'], wts['b_q'], wts['w_branch'], wts['w_out'],
              wts['g_post_mix'], cfg['tm_mix'], t)
    y = _ffn(x1, wts['g_pre_ffn'], wts['w_up'], wts['w_down'], wts['g_post_ffn'], cfg['tm'])
    return y, k_rot, s_new, swa_in, conv_last


def kernel(x_prompt, x_sample, mem_prompt, cache_swa_k, cache_swa_v, state_gdn, state_conv, cache_mem_k, cache_mem_v, g_pre_mix, w_in, b_in, swa_sinks, conv_w, gdn_a_log, gdn_dt_bias, gdn_norm_g, g_mem, w_mem_kv, w_branch, w_out, g_post_mix, g_pre_ffn, w_up, w_down, g_post_ffn):
    depth = w_in.shape[0]
    assert depth == 1
    l = 0
    bp, tp, _ = x_prompt.shape
    bs, ts, _ = x_sample.shape
    assert bp == 1

    wts = {
        'g_pre_mix': g_pre_mix[l][None, :],
        'w_in': _regroup_cols(w_in[l]).astype(BF16),
        'b_in': _regroup_cols(b_in[l])[None, :],
        'w_q': w_in[l][:, O_QC:O_END].astype(BF16),
        'b_q': b_in[l][None, O_QC:O_END],
        'sinks': swa_sinks[l],
        'conv_w': conv_w[l],
        'alog': _pad_row(gdn_a_log[l]),
        'dt': _pad_row(gdn_dt_bias[l]),
        'norm_g': gdn_norm_g[l][None, :],
        'w_branch': w_branch[l].astype(BF16),
        'w_out': w_out[l].astype(BF16),
        'g_post_mix': g_post_mix[l][None, :],
        'g_pre_ffn': g_pre_ffn[l][None, :],
        'w_up': w_up[l].astype(BF16),
        'w_down': w_down[l].astype(BF16),
        'g_post_ffn': g_post_ffn[l][None, :],
    }

    mkv, mkv_bf = _mem_kv(mem_prompt[0], g_mem[l][None, :], w_mem_kv[l].astype(BF16))
    cfg_p = dict(tq=512, cache=False, tt_gdn=256, seg=256, chunk=128, carry=True,
                 tm_mix=512, tm=512)
    zero_hist = jnp.zeros((1, CONV_W - 1, 3 * GDN_QK), F32)
    zero_s = jnp.zeros((1, GDN_HEADS, GDN_DK, GDN_DV), F32)
    yp, kp_rot, sp_new, swa_p, clast_p = _layer(
        x_prompt.reshape(bp * tp, D_MODEL), bp, tp, 0, None, None,
        zero_s, zero_hist, mkv_bf[None, :, :MEM_Q], mkv_bf[None, :, MEM_Q:], wts, cfg_p)

    cfg_s = dict(tq=ts, cache=True, tt_gdn=128, seg=ts, chunk=ts, carry=False,
                 tm_mix=4 * ts, tm=512)
    ys, ks_rot, ss_new, swa_s, clast_s = _layer(
        x_sample.reshape(bs * ts, D_MODEL), bs, ts, PAST_LEN,
        cache_swa_k[l].reshape(bs, WINDOW, SWA_KV), cache_swa_v[l].reshape(bs, WINDOW, SWA_KV),
        state_gdn[l], state_conv[l],
        cache_mem_k[l].astype(BF16).reshape(bs, N_MEM, MEM_Q),
        cache_mem_v[l].astype(BF16).reshape(bs, N_MEM, MEM_Q), wts, cfg_s)

    kv_shape = (SWA_KV_HEADS, SWA_HEAD_DIM)
    va_cols = slice(SWA_VA * LANE, (SWA_VA + 1) * LANE)

    p_k = kp_rot[tp - WINDOW:].reshape((1, bp, WINDOW) + kv_shape)
    p_v = swa_p[tp - WINDOW:, va_cols].reshape((1, bp, WINDOW) + kv_shape)
    p_c = clast_p[:, 8 - (CONV_W - 1):][None]
    p_mk = mkv[:, :MEM_Q].reshape(1, bp, N_MEM, MEM_HEADS, MEM_HEAD_DIM)
    p_mv = mkv[:, MEM_Q:].reshape(1, bp, N_MEM, MEM_HEADS, MEM_HEAD_DIM)

    ks_new = ks_rot.reshape(bs, ts, SWA_KV)
    vs_new = swa_s[:, va_cols].reshape(bs, ts, SWA_KV)
    s_k = jnp.concatenate([cache_swa_k[l].reshape(bs, WINDOW, SWA_KV), ks_new], axis=1)[:, -WINDOW:]
    s_v = jnp.concatenate([cache_swa_v[l].reshape(bs, WINDOW, SWA_KV), vs_new], axis=1)[:, -WINDOW:]
    s_c = clast_s[:, 8 - (CONV_W - 1):]

    return (yp.reshape(bp, tp, D_MODEL), ys.reshape(bs, ts, D_MODEL),
            p_k, p_v, sp_new[None], p_c, p_mk, p_mv,
            s_k.reshape((1, bs, WINDOW) + kv_shape), s_v.reshape((1, bs, WINDOW) + kv_shape),
            ss_new[None], s_c[None])
```

```python
import functools

import jax
import jax.numpy as jnp
from jax import lax
from jax.experimental import pallas as pl
from jax.experimental.pallas import tpu as pltpu

F32 = jnp.float32
BF16 = jnp.bfloat16

D_MODEL = 1024
PAST_LEN = 2048
CHUNK = 64
N_MEM = 256
EPS = 1e-6
SWA_HEADS = 16
SWA_KV_HEADS = 2
SWA_HEAD_DIM = 64
WINDOW = 128
ROPE_THETA = 10000.0
GDN_HEADS = 8
GDN_DK = 128
GDN_DV = 128
CONV_W = 4
MEM_HEADS = 4
MEM_HEAD_DIM = 256
N_BRANCH = 3
D_FF = 4 * D_MODEL

LANE = 128
SWA_Q = SWA_HEADS * SWA_HEAD_DIM
SWA_KV = SWA_KV_HEADS * SWA_HEAD_DIM
GDN_QK = GDN_HEADS * GDN_DK
GDN_V = GDN_HEADS * GDN_DV
MEM_Q = MEM_HEADS * MEM_HEAD_DIM

O_KA = SWA_Q
O_QB = O_KA + 2 * SWA_KV
O_AB = O_QB + 2 * GDN_QK + 2 * GDN_V
O_QC = O_AB + 2 * GDN_HEADS
O_END = O_QC + MEM_Q + N_BRANCH * D_MODEL
COL_QA = 0
COL_KA = O_KA // LANE
COL_VA = COL_KA + 1
COL_QB = O_QB // LANE
COL_KB = COL_QB + GDN_HEADS
COL_VB = COL_KB + GDN_HEADS
COL_ZB = COL_VB + GDN_HEADS
COL_AB = O_AB // LANE
N_PROJ = (COL_AB + 1) * LANE

VMEM_LIMIT = 48 * 1024 * 1024


def _dot(a, b):
    return jnp.dot(a, b, preferred_element_type=F32)


def _dot_nt(a, b):
    return lax.dot_general(a, b, (((1,), (1,)), ((), ())), preferred_element_type=F32)


def _dot_tn(a, b):
    return lax.dot_general(a, b, (((0,), (0,)), ((), ())), preferred_element_type=F32)


def _rms(x, g):
    return x * lax.rsqrt(jnp.mean(x * x, axis=-1, keepdims=True) + EPS) * g


def _sigmoid(x):
    return 1.0 / (1.0 + jnp.exp(-x))


def _resident(shape):
    return pl.BlockSpec(shape, lambda *_: (0,) * len(shape), pipeline_mode=pl.Buffered(1))


def _split3(x):
    hi = x.astype(BF16)
    r = x - hi.astype(F32)
    mid = r.astype(BF16)
    lo = (r - mid.astype(F32)).astype(BF16)
    return hi, mid, lo


def _mem_kv_kernel(m_ref, g_ref, w_ref, o_ref, ob_ref):
    h = _rms(m_ref[...], g_ref[...]).astype(BF16)
    kv = _dot(h, w_ref[...])
    o_ref[...] = kv
    ob_ref[...] = kv.astype(BF16)


def _mem_kv(mem, g_mem, w_bf16):
    n = w_bf16.shape[1]
    tn = 1024
    return pl.pallas_call(
        _mem_kv_kernel,
        grid=(n // tn,),
        in_specs=[
            pl.BlockSpec((N_MEM, D_MODEL), lambda j: (0, 0)),
            pl.BlockSpec((1, D_MODEL), lambda j: (0, 0)),
            pl.BlockSpec((D_MODEL, tn), lambda j: (0, j)),
        ],
        out_specs=[pl.BlockSpec((N_MEM, tn), lambda j: (0, j))] * 2,
        out_shape=[jax.ShapeDtypeStruct((N_MEM, n), F32), jax.ShapeDtypeStruct((N_MEM, n), BF16)],
        compiler_params=pltpu.CompilerParams(dimension_semantics=("arbitrary",), vmem_limit_bytes=VMEM_LIMIT),
        name="mem_kv",
    )(mem, g_mem, w_bf16)


def _rope(x, cos, sin_signed):
    lane = lax.broadcasted_iota(jnp.int32, x.shape, 1)
    first = (lane % SWA_HEAD_DIM) < (SWA_HEAD_DIM // 2)
    rot = jnp.where(first, pltpu.roll(x, LANE - SWA_HEAD_DIM // 2, 1), pltpu.roll(x, SWA_HEAD_DIM // 2, 1))
    return x * cos + rot * sin_signed


SWA_KA = SWA_Q // LANE
SWA_VA = SWA_KA + 1
KEYS = WINDOW + CHUNK
KPAD = 256


def _swa_kernel(sink_ref, q_ref, k_ref, v_ref, kh_ref, vh_ref, base_ref, off_ref,
                o_ref, kout_ref, qs_ref, ka_ref, kb_ref, va_ref, vb_ref, *, tq, cache):
    i = pl.program_id(1)
    cb, sb = base_ref[0:1, :], base_ref[1:2, :]
    co, so = off_ref[0], off_ref[1]
    cos_all = cb * co - sb * so
    sin_all = sb * co + cb * so
    cos, sin = cos_all[WINDOW:], sin_all[WINDOW:]
    kr = _rope(k_ref[...], cos, sin)
    kout_ref[...] = kr
    if cache:
        khr = kh_ref[...]
    else:
        khr = _rope(kh_ref[...], cos_all[:WINDOW], sin_all[:WINDOW])
    zpad = jnp.zeros((KPAD - KEYS, LANE), F32)
    kall = jnp.concatenate([khr, kr, zpad], axis=0)
    vall = jnp.concatenate([vh_ref[...], v_ref[...], zpad], axis=0)
    lo = lax.broadcasted_iota(jnp.int32, kall.shape, 1) < SWA_HEAD_DIM
    zero = jnp.zeros_like(kall)
    for src, a_ref, b_ref in ((kall, ka_ref, kb_ref), (vall, va_ref, vb_ref)):
        rolled = pltpu.roll(src, SWA_HEAD_DIM, 1)
        a_ref[0] = jnp.where(lo, src, zero).astype(BF16)
        b_ref[0] = jnp.where(lo, zero, rolled).astype(BF16)
        a_ref[1] = jnp.where(lo, rolled, zero).astype(BF16)
        b_ref[1] = jnp.where(lo, zero, src).astype(BF16)

    scale = SWA_HEAD_DIM ** -0.5
    for p in range(SWA_HEADS // 2):
        sl = slice(p * LANE, (p + 1) * LANE)
        qs_ref[:, sl] = (_rope(q_ref[:, sl], cos, sin) * scale).astype(BF16)

    jj = lax.broadcasted_iota(jnp.int32, (CHUNK, KPAD), 1)
    pairs_per_kv = SWA_HEADS // SWA_KV_HEADS // 2
    inst = [(c, g) for c in range(tq // CHUNK) for g in range(SWA_KV_HEADS)]

    def scores(c, g):
        rows = slice(c * CHUNK, (c + 1) * CHUNK)
        krows = slice(c * CHUNK, c * CHUNK + KPAD)
        kbd = jnp.concatenate([ka_ref[g, krows, :], kb_ref[g, krows, :]], axis=0)
        qst = jnp.concatenate(
            [qs_ref[rows, (g * pairs_per_kv + j) * LANE:(g * pairs_per_kv + j + 1) * LANE]
             for j in range(pairs_per_kv)], axis=0)
        return _dot_nt(qst, kbd)

    def probs(c, g, s):
        if cache:
            valid = jj < KEYS
        else:
            valid = (jj < KEYS) & ((i > 0) | (jj + c * CHUNK >= WINDOW))
        pblocks = []
        for j in range(pairs_per_kv):
            halves = []
            for half in range(2):
                sh = s[j * CHUNK:(j + 1) * CHUNK, half * KPAD:(half + 1) * KPAD]
                sh = jnp.where(valid, sh, -jnp.inf)
                sink = sink_ref[2 * (g * pairs_per_kv + j) + half]
                m = jnp.maximum(jnp.max(sh, axis=-1, keepdims=True), sink)
                e = jnp.exp(sh - m)
                den = jnp.sum(e, axis=-1, keepdims=True) + jnp.exp(sink - m)
                halves.append((e * (1.0 / den)).astype(BF16))
            pblocks.append(jnp.concatenate(halves, axis=1))
        return jnp.concatenate(pblocks, axis=0)

    def outputs(c, g, pmat):
        rows = slice(c * CHUNK, (c + 1) * CHUNK)
        krows = slice(c * CHUNK, c * CHUNK + KPAD)
        vbd = jnp.concatenate([va_ref[g, krows, :], vb_ref[g, krows, :]], axis=0)
        o = _dot(pmat, vbd)
        for j in range(pairs_per_kv):
            p = g * pairs_per_kv + j
            o_ref[rows, p * LANE:(p + 1) * LANE] = o[j * CHUNK:(j + 1) * CHUNK].astype(BF16)

    s_l, p_l = {}, {}
    for step in range(len(inst) + 2):
        if step < len(inst):
            s_l[step] = scores(*inst[step])
        if 0 <= step - 1 < len(inst):
            p_l[step - 1] = probs(*inst[step - 1], s_l.pop(step - 1))
        if 0 <= step - 2 < len(inst):
            outputs(*inst[step - 2], p_l.pop(step - 2))


def _swa(proj, k_hist, v_hist, rope_base, rope_off, sinks, nb, t, tq, cache):
    nt = t // tq
    kv_rows = WINDOW + tq + (KPAD - KEYS)
    if cache:
        hist_k_spec = pl.BlockSpec((None, WINDOW, LANE), lambda b, i: (b, 0, 0))
        hist_v_spec = pl.BlockSpec((None, WINDOW, LANE), lambda b, i: (b, 0, 0))
        hk, hv = k_hist, v_hist
    else:
        r = tq // WINDOW

        def hidx(b, i):
            return jnp.maximum((b * nt + i) * r - 1, 0)

        hist_k_spec = pl.BlockSpec((WINDOW, LANE), lambda b, i: (hidx(b, i), SWA_KA))
        hist_v_spec = pl.BlockSpec((WINDOW, LANE), lambda b, i: (hidx(b, i), SWA_VA))
        hk, hv = proj, proj
    return pl.pallas_call(
        functools.partial(_swa_kernel, tq=tq, cache=cache),
        grid=(nb, nt),
        in_specs=[
            pl.BlockSpec(memory_space=pltpu.SMEM),
            pl.BlockSpec((tq, SWA_Q), lambda b, i: (b * nt + i, 0)),
            pl.BlockSpec((tq, LANE), lambda b, i: (b * nt + i, SWA_KA)),
            pl.BlockSpec((tq, LANE), lambda b, i: (b * nt + i, SWA_VA)),
            hist_k_spec,
            hist_v_spec,
            pl.BlockSpec((None, 2, LANE), lambda b, i: (i, 0, 0)),
            pl.BlockSpec((2, WINDOW + tq, LANE), lambda b, i: (0, 0, 0)),
        ],
        out_specs=[
            pl.BlockSpec((tq, SWA_Q), lambda b, i: (b * nt + i, 0)),
            pl.BlockSpec((tq, LANE), lambda b, i: (b * nt + i, 0)),
        ],
        out_shape=[
            jax.ShapeDtypeStruct((nb * t, SWA_Q), BF16),
            jax.ShapeDtypeStruct((nb * t, LANE), F32),
        ],
        scratch_shapes=[
            pltpu.VMEM((tq, SWA_Q), BF16),
            pltpu.VMEM((SWA_KV_HEADS, kv_rows, LANE), BF16),
            pltpu.VMEM((SWA_KV_HEADS, kv_rows, LANE), BF16),
            pltpu.VMEM((SWA_KV_HEADS, kv_rows, LANE), BF16),
            pltpu.VMEM((SWA_KV_HEADS, kv_rows, LANE), BF16),
        ],
        compiler_params=pltpu.CompilerParams(dimension_semantics=("parallel", "arbitrary"),
                                             vmem_limit_bytes=VMEM_LIMIT),
        name="swa",
    )(sinks, proj, proj, proj, hk, hv, rope_base, rope_off)


PROJ_CHUNK = 256
GDN_VMEM_LIMIT = 58 * 1024 * 1024
INV_BASE = 16


def _inverse_minus_identity(pms, row, col, chunk, between=lambda: None):
    base = (row // INV_BASE) == (col // INV_BASE)
    diag = row == col
    ms = [jnp.where(base, pm, 0.0) for pm in pms]
    ts = [jnp.where(diag, 1.0, m) for m in ms]
    for _ in range(INV_BASE.bit_length() - 2):
        mbs = [m.astype(BF16) for m in ms]
        ms = [_dot(mb, mb) for mb in mbs]
        between()
        ts = [t + _dot(m.astype(BF16), t.astype(BF16)) for m, t in zip(ms, ts)]
    bs = INV_BASE
    while bs < chunk:
        inner = (row // bs) == (col // bs)
        outer = (row // (2 * bs)) == (col // (2 * bs))
        sib = outer & jnp.logical_not(inner)
        xs = [_dot(jnp.where(sib, pm, 0.0).astype(BF16), t.astype(BF16)) for pm, t in zip(pms, ts)]
        ts = [t + _dot(t.astype(BF16), x.astype(BF16)) for t, x in zip(ts, xs)]
        between()
        bs *= 2
    rs = [jnp.where(diag, 0.0, t) for t in ts]
    return rs


def _gdn_kernel(x0_ref, xn_ref, gpre_ref, w_ref, b_ref, hist_ref, cw_ref, alog_ref, dt_ref, ng_ref, s0_ref,
                o_ref, sout_ref, swa_ref, clast_ref, proj_scr, h_scr, s_scr, carry_scr, ext_scr, conv_scr,
                *, tt, seg, chunk, carry):
    i = pl.program_id(0)
    nseg = tt // seg
    nblk = LANE // chunk
    ngroups = tt // LANE
    hist_rows = CONV_W - 1
    cur = proj_scr.at[i % 2]
    nxt = proj_scr.at[1 - i % 2]

    nchunks = pl.cdiv(N_PROJ, PROJ_CHUNK)

    def project(dst, c):
        cols = slice(c * PROJ_CHUNK, min((c + 1) * PROJ_CHUNK, N_PROJ))
        dst[8:8 + tt, cols] = _dot(h_scr[...], w_ref[:, cols]) + b_ref[:, cols]

    @pl.when(i == 0)
    def _():
        h_scr[...] = _rms(x0_ref[...], gpre_ref[...]).astype(BF16)
        for c in range(nchunks):
            project(proj_scr.at[0], c)

    h_scr[...] = _rms(xn_ref[...], gpre_ref[...]).astype(BF16)
    first = COL_QB * LANE // PROJ_CHUNK
    pending = list(range(first, nchunks - 1)) + list(range(first)) + [nchunks - 1]

    def emit(n=1):
        for _ in range(n):
            if pending:
                project(nxt, pending.pop(0))

    def init():
        for s in range(nseg):
            s_scr[s] = s0_ref[s]
            carry_scr[s] = jnp.zeros((8, 3 * GDN_QK), F32)
            carry_scr[s, 8 - hist_rows:8, :] = hist_ref[s]

    if carry:
        pl.when(i == 0)(init)
    else:
        init()

    swa_ref[...] = cur[8:8 + tt, COL_QA * LANE:(COL_VA + 1) * LANE]

    qkv = slice(COL_QB * LANE, COL_ZB * LANE)
    base = 8 - hist_rows
    for s in range(nseg):
        if carry:
            cur[0:8, qkv] = carry_scr[s]
            src, col0 = cur, COL_QB * LANE
            last = cur[tt:tt + 8, qkv]
            carry_scr[s] = last
        else:
            ext_scr[0:8, :] = carry_scr[s]
            ext_scr[8:8 + seg, :] = cur[8 + s * seg:8 + (s + 1) * seg, qkv]
            src, col0 = ext_scr, 0
            last = ext_scr[seg:seg + 8, :]
        clast_ref[s] = last
        for cb in range(3 * GDN_HEADS):
            lanes = slice(cb * LANE, (cb + 1) * LANE)
            slanes = slice(col0 + cb * LANE, col0 + (cb + 1) * LANE)
            y = src[base:base + seg, slanes] * cw_ref[0:1, lanes]
            for tap in range(1, CONV_W):
                y = y + src[base + tap:base + tap + seg, slanes] * cw_ref[tap:tap + 1, lanes]
            conv_scr[s * seg:(s + 1) * seg, lanes] = y * _sigmoid(y)
            if s == 0 and cb % 2 == 0:
                emit()

    row = lax.broadcasted_iota(jnp.int32, (LANE, LANE), 0)
    col = lax.broadcasted_iota(jnp.int32, (LANE, LANE), 1)
    same = (row // chunk) == (col // chunk)
    incl = same & (row >= col)
    strict = same & (row > col)
    lmask = jnp.where(incl, 1.0, 0.0).astype(BF16)
    row2 = jnp.concatenate([row, row], axis=0)

    inst = [(gi, hh) for gi in range(ngroups) for hh in range(GDN_HEADS)]
    heads = range(GDN_HEADS)

    gates = []
    for gi in range(ngroups):
        ab = cur[8 + gi * LANE:8 + (gi + 1) * LANE, COL_AB * LANE:(COL_AB + 1) * LANE]
        xa = ab + dt_ref[...]
        softplus = jnp.maximum(xa, 0.0) + jnp.log1p(jnp.exp(-jnp.abs(xa)))
        la = -jnp.exp(alog_ref[...]) * softplus
        beta = _sigmoid(ab)
        g_all = sum(_dot(lmask, piece) for piece in _split3(la))
        gates.append((g_all, g_all.T, beta))

    qn_l, kn_l, v_l, gb_l, bb_l, decay_l = [], [], [], [], [], []
    for gi, hh in inst:
        rows = slice(gi * LANE, (gi + 1) * LANE)
        q = conv_scr[rows, hh * LANE:(hh + 1) * LANE]
        k = conv_scr[rows, GDN_QK + hh * LANE:GDN_QK + (hh + 1) * LANE]
        v_l.append(conv_scr[rows, 2 * GDN_QK + hh * LANE:2 * GDN_QK + (hh + 1) * LANE])
        qn_l.append(q * (lax.rsqrt(jnp.sum(q * q, axis=-1, keepdims=True) + EPS) * (GDN_DK ** -0.5)))
        kn_l.append(k * lax.rsqrt(jnp.sum(k * k, axis=-1, keepdims=True) + EPS))
        g_all, g_all_t, beta = gates[gi]
        gb = jnp.broadcast_to(g_all[:, hh:hh + 1], (LANE, LANE))
        grow = jnp.broadcast_to(g_all_t[hh:hh + 1, :], (LANE, LANE))
        gb_l.append(gb)
        bb_l.append(jnp.broadcast_to(beta[:, GDN_HEADS + hh:GDN_HEADS + hh + 1], (LANE, LANE)))
        decay_l.append(jnp.exp(jnp.where(incl, gb - grow, -jnp.inf)))

    emit()
    knb_l = [kn.astype(BF16) for kn in kn_l]
    cat_l = [_dot_nt(jnp.concatenate([qn.astype(BF16), knb], axis=0), knb)
             for qn, knb in zip(qn_l, knb_l)]
    emit()
    qk_l = [cat[:LANE] * decay for cat, decay in zip(cat_l, decay_l)]
    pm_l = [jnp.where(strict, -(bb * cat[LANE:] * decay), 0.0)
            for bb, cat, decay in zip(bb_l, cat_l, decay_l)]
    r_l = _inverse_minus_identity(pm_l, row, col, chunk)
    emit()
    eg_l = [jnp.exp(gb) for gb in gb_l]
    rhs_l = [jnp.concatenate([v * bb, kn * (bb * eg)], axis=1)
             for v, bb, kn, eg in zip(v_l, bb_l, kn_l, eg_l)]
    sol_l = [rhs + _dot(r.astype(BF16), rhs.astype(BF16)) for r, rhs in zip(r_l, rhs_l)]
    wq_l = [jnp.concatenate([sol[:, GDN_DV:], qn * eg], axis=0).astype(BF16)
            for sol, qn, eg in zip(sol_l, qn_l, eg_l)]
    kd_l = []
    for kn, gb in zip(kn_l, gb_l):
        glast = gb[chunk - 1:chunk, :]
        for blk in range(1, nblk):
            glast = jnp.where(row >= blk * chunk, gb[(blk + 1) * chunk - 1:(blk + 1) * chunk, :], glast)
        kd_l.append(kn * jnp.exp(glast - gb))

    emit()
    nstate = 1 if carry else nblk
    state = [[s_scr[sidx, hh] for hh in heads] for sidx in range(nstate)]
    for gi in range(ngroups):
        ix = [gi * GDN_HEADS + hh for hh in heads]
        ws_l = []
        for hh in heads:
            ws = None
            for blk in range(nblk):
                part = _dot(wq_l[ix[hh]], state[0 if carry else blk][hh].astype(BF16))
                ws = part if ws is None else jnp.where(row2 >= blk * chunk, part, ws)
            ws_l.append(ws)
        emit()
        vnb_l = [(sol_l[ix[hh]][:, :GDN_DV] - ws_l[hh][:LANE]).astype(BF16) for hh in heads]
        o_l = [ws_l[hh][LANE:] + _dot(qk_l[ix[hh]].astype(BF16), vnb_l[hh]) for hh in heads]
        emit()
        for blk in range(nblk):
            sidx = 0 if carry else blk
            new = []
            for hh in heads:
                kd = kd_l[ix[hh]]
                gb = gb_l[ix[hh]]
                kdb = kd if nblk == 1 else jnp.where((row // chunk) == blk, kd, 0.0)
                gtot = jnp.exp(gb[(blk + 1) * chunk - 1:(blk + 1) * chunk, :])
                new.append(state[sidx][hh] * gtot + _dot_tn(kdb.astype(BF16), vnb_l[hh]))
            state[sidx] = new
        for hh in heads:
            rows = slice(gi * LANE, (gi + 1) * LANE)
            lanes = slice(hh * LANE, (hh + 1) * LANE)
            z = cur[8 + gi * LANE:8 + (gi + 1) * LANE, COL_ZB * LANE + hh * LANE:COL_ZB * LANE + (hh + 1) * LANE]
            o_ref[rows, lanes] = (_rms(o_l[hh], ng_ref[...]) * (z * _sigmoid(z))).astype(BF16)
    emit(len(pending))
    for sidx in range(nstate):
        for hh in heads:
            s_scr[sidx, hh] = state[sidx][hh]

    def fin():
        for s in range(nseg):
            sout_ref[s] = s_scr[s]

    if carry:
        pl.when(i == pl.num_programs(0) - 1)(fin)
    else:
        fin()


def _gdn(x, gpre, w_bf16, b, hist, s0, conv_w, alog_row, dt_row, norm_g, tt, seg, chunk, carry):
    t_rows = x.shape[0]
    nt = t_rows // tt
    nseg = tt // seg
    nseq = s0.shape[0]
    if carry:
        seq_idx = lambda i: 0
    else:
        seq_idx = lambda i: i
    swa_cols = (COL_VA + 1 - COL_QA) * LANE
    row_spec = pl.BlockSpec((1, LANE), lambda i: (0, 0))
    sspec = pl.BlockSpec((nseg, GDN_HEADS, GDN_DK, GDN_DV), lambda i: (seq_idx(i), 0, 0, 0))
    cspec = pl.BlockSpec((nseg, 8, 3 * GDN_QK), lambda i: (seq_idx(i), 0, 0))
    return pl.pallas_call(
        functools.partial(_gdn_kernel, tt=tt, seg=seg, chunk=chunk, carry=carry),
        grid=(nt,),
        in_specs=[
            pl.BlockSpec((tt, D_MODEL), lambda i: (0, 0)),
            pl.BlockSpec((tt, D_MODEL), lambda i: (jnp.minimum(i + 1, nt - 1), 0)),
            pl.BlockSpec((1, D_MODEL), lambda i: (0, 0)),
            _resident((D_MODEL, N_PROJ)),
            _resident((1, N_PROJ)),
            pl.BlockSpec((nseg, CONV_W - 1, 3 * GDN_QK), lambda i: (seq_idx(i), 0, 0)),
            pl.BlockSpec((CONV_W, 3 * GDN_QK), lambda i: (0, 0)),
            row_spec, row_spec, row_spec,
            sspec,
        ],
        out_specs=[
            pl.BlockSpec((tt, GDN_V), lambda i: (i, 0)),
            sspec,
            pl.BlockSpec((tt, swa_cols), lambda i: (i, 0)),
            cspec,
        ],
        out_shape=[
            jax.ShapeDtypeStruct((t_rows, GDN_V), BF16),
            jax.ShapeDtypeStruct((nseq, GDN_HEADS, GDN_DK, GDN_DV), F32),
            jax.ShapeDtypeStruct((t_rows, swa_cols), F32),
            jax.ShapeDtypeStruct((nseq, 8, 3 * GDN_QK), F32),
        ],
        scratch_shapes=[
            pltpu.VMEM((2, 8 + tt, N_PROJ), F32),
            pltpu.VMEM((tt, D_MODEL), BF16),
            pltpu.VMEM((nseg, GDN_HEADS, GDN_DK, GDN_DV), F32),
            pltpu.VMEM((nseg, 8, 3 * GDN_QK), F32),
            pltpu.VMEM((8 + seg, 3 * GDN_QK), F32),
            pltpu.VMEM((tt, 3 * GDN_QK), F32),
        ],
        compiler_params=pltpu.CompilerParams(dimension_semantics=("arbitrary",),
                                             vmem_limit_bytes=GDN_VMEM_LIMIT),
        name="gdn",
    )(x, x, gpre, w_bf16, b, hist, conv_w, alog_row, dt_row, norm_g, s0)


def _mix_kernel(x_ref, oa_ref, ob_ref, mk_ref, mv_ref, gpre_ref, wq_ref, bq_ref, wb_ref, wo_ref, gp_ref, o_ref):
    x = x_ref[...]
    tm = x.shape[0]
    nmem = mk_ref.shape[0]
    rpm = tm // nmem
    h = _rms(x, gpre_ref[...]).astype(BF16)
    qc = ((_dot(h, wq_ref[:, :MEM_Q]) + bq_ref[:, :MEM_Q]) * (MEM_HEAD_DIM ** -0.5)).astype(BF16)
    blocks = []
    for b in range(nmem):
        heads = []
        for hh in range(MEM_HEADS):
            sl = slice(hh * MEM_HEAD_DIM, (hh + 1) * MEM_HEAD_DIM)
            s = _dot_nt(qc[b * rpm:(b + 1) * rpm, sl], mk_ref[b, :, sl])
            m = jnp.max(s, axis=-1, keepdims=True)
            e = jnp.exp(s - m)
            p = e * (1.0 / jnp.sum(e, axis=-1, keepdims=True))
            heads.append(_dot(p.astype(BF16), mv_ref[b, :, sl]).astype(BF16))
        blocks.append(jnp.concatenate(heads, axis=1))
    oc = blocks[0] if nmem == 1 else jnp.concatenate(blocks, axis=0)
    merged = None
    for n, br_in in enumerate((oa_ref[...], ob_ref[...], oc)):
        cols = slice(MEM_Q + n * D_MODEL, MEM_Q + (n + 1) * D_MODEL)
        gate = _sigmoid(_dot(h, wq_ref[:, cols]) + bq_ref[:, cols])
        term = gate * _dot(br_in, wb_ref[n])
        merged = term if merged is None else merged + term
    y = _dot(merged.astype(BF16), wo_ref[...])
    o_ref[...] = x + _rms(y, gp_ref[...])


def _mix(x, oa, ob, mk, mv, gpre, wq, bq, wb, wo, gp, tm, rows_per_mem):
    m = x.shape[0]
    tile = pl.BlockSpec((tm, D_MODEL), lambda i: (i, 0))
    vec = pl.BlockSpec((1, D_MODEL), lambda i: (0, 0))
    if mk.shape[0] == 1:
        mspec = _resident((1, N_MEM, MEM_Q))
    else:
        mspec = pl.BlockSpec((tm // rows_per_mem, N_MEM, MEM_Q), lambda i: (i, 0, 0))
    nq = MEM_Q + N_BRANCH * D_MODEL
    return pl.pallas_call(
        _mix_kernel,
        grid=(m // tm,),
        in_specs=[
            tile, tile, tile, mspec, mspec, vec,
            _resident((D_MODEL, nq)),
            _resident((1, nq)),
            _resident((N_BRANCH, D_MODEL, D_MODEL)),
            _resident((D_MODEL, D_MODEL)),
            vec,
        ],
        out_specs=tile,
        out_shape=jax.ShapeDtypeStruct((m, D_MODEL), F32),
        compiler_params=pltpu.CompilerParams(dimension_semantics=("parallel",), vmem_limit_bytes=VMEM_LIMIT),
        name="mix",
    )(x, oa, ob, mk, mv, gpre, wq, bq, wb, wo, gp)


FF_CHUNK = 1024


def _ffn_kernel(x_ref, g1_ref, wu_ref, wd_ref, g2_ref, o_ref):
    x = x_ref[...]
    hf = _rms(x, g1_ref[...]).astype(BF16)
    f = None
    for c in range(D_FF // FF_CHUNK):
        sl = slice(c * FF_CHUNK, (c + 1) * FF_CHUNK)
        up = jnp.maximum(_dot(hf, wu_ref[:, sl]), 0.0)
        part = _dot((up * up).astype(BF16), wd_ref[sl, :])
        f = part if f is None else f + part
    o_ref[...] = x + _rms(f, g2_ref[...])


def _ffn(x, g1, wu, wd, g2, tm):
    m = x.shape[0]
    tile = pl.BlockSpec((tm, D_MODEL), lambda i: (i, 0))
    vec = pl.BlockSpec((1, D_MODEL), lambda i: (0, 0))
    return pl.pallas_call(
        _ffn_kernel,
        grid=(m // tm,),
        in_specs=[tile, vec, _resident((D_MODEL, D_FF)), _resident((D_FF, D_MODEL)), vec],
        out_specs=tile,
        out_shape=jax.ShapeDtypeStruct((m, D_MODEL), F32),
        compiler_params=pltpu.CompilerParams(dimension_semantics=("parallel",), vmem_limit_bytes=VMEM_LIMIT),
        name="ffn",
    )(x, g1, wu, wd, g2)


def _rope_tables(pos0, nt, tq):
    half = SWA_HEAD_DIM // 2
    inv = ROPE_THETA ** (-jnp.arange(half, dtype=F32) / half)
    inv = jnp.tile(inv, LANE // half)
    sgn = jnp.tile(jnp.concatenate([-jnp.ones((half,), F32), jnp.ones((half,), F32)]), LANE // SWA_HEAD_DIM)
    base = (pos0 + tq * jnp.arange(nt, dtype=jnp.int32)).astype(F32)[:, None] * inv[None, :]
    off = jnp.arange(-WINDOW, tq, dtype=jnp.int32).astype(F32)[:, None] * inv[None, :]
    rope_base = jnp.stack([jnp.cos(base), sgn * jnp.sin(base)], axis=1)
    rope_off = jnp.stack([jnp.cos(off), sgn * jnp.sin(off)], axis=0)
    return rope_base, rope_off


def _pad_row(v, offset=0):
    return jnp.zeros((1, LANE), F32).at[0, offset:offset + v.shape[0]].set(v)


def _layer(x, nb, t, pos0, k_hist, v_hist, s0, conv_hist, mk, mv, wts, cfg):
    ob, s_new, swa_in, conv_last = _gdn(x, wts['g_pre_mix'], wts['w_in'], wts['b_in'], conv_hist, s0,
                                        wts['conv_w'], wts['alog'], wts['dt'], wts['norm_g'],
                                        cfg['tt_gdn'], cfg['seg'], cfg['chunk'], cfg['carry'])
    rope_base, rope_off = _rope_tables(pos0, t // cfg['tq'], cfg['tq'])
    oa, k_rot = _swa(swa_in, k_hist, v_hist, rope_base, rope_off, wts['sinks'], nb, t, cfg['tq'], cfg['cache'])
    x1 = _mix(x, oa, ob, mk, mv, wts['g_pre_mix'], wts['w_q'], wts['b_q'], wts['w_branch'], wts['w_out'],
              wts['g_post_mix'], cfg['tm_mix'], t)
    y = _ffn(x1, wts['g_pre_ffn'], wts['w_up'], wts['w_down'], wts['g_post_ffn'], cfg['tm'])
    return y, k_rot, s_new, swa_in, conv_last


def kernel(x_prompt, x_sample, mem_prompt, cache_swa_k, cache_swa_v, state_gdn, state_conv, cache_mem_k, cache_mem_v, g_pre_mix, w_in, b_in, swa_sinks, conv_w, gdn_a_log, gdn_dt_bias, gdn_norm_g, g_mem, w_mem_kv, w_branch, w_out, g_post_mix, g_pre_ffn, w_up, w_down, g_post_ffn):
    depth = w_in.shape[0]
    assert depth == 1
    l = 0
    bp, tp, _ = x_prompt.shape
    bs, ts, _ = x_sample.shape
    assert bp == 1

    wts = {
        'g_pre_mix': g_pre_mix[l][None, :],
        'w_in': w_in[l][:, :N_PROJ].astype(BF16),
        'b_in': b_in[l][None, :N_PROJ],
        'w_q': w_in[l][:, O_QC:O_END].astype(BF16),
        'b_q': b_in[l][None, O_QC:O_END],
        'sinks': swa_sinks[l],
        'conv_w': conv_w[l],
        'alog': _pad_row(gdn_a_log[l]),
        'dt': _pad_row(gdn_dt_bias[l]),
        'norm_g': gdn_norm_g[l][None, :],
        'w_branch': w_branch[l].astype(BF16),
        'w_out': w_out[l].astype(BF16),
        'g_post_mix': g_post_mix[l][None, :],
        'g_pre_ffn': g_pre_ffn[l][None, :],
        'w_up': w_up[l].astype(BF16),
        'w_down': w_down[l].astype(BF16),
        'g_post_ffn': g_post_ffn[l][None, :],
    }

    mkv, mkv_bf = _mem_kv(mem_prompt[0], g_mem[l][None, :], w_mem_kv[l].astype(BF16))
    cfg_p = dict(tq=512, cache=False, tt_gdn=256, seg=256, chunk=128, carry=True,
                 tm_mix=512, tm=512)
    zero_hist = jnp.zeros((1, CONV_W - 1, 3 * GDN_QK), F32)
    zero_s = jnp.zeros((1, GDN_HEADS, GDN_DK, GDN_DV), F32)
    yp, kp_rot, sp_new, swa_p, clast_p = _layer(
        x_prompt.reshape(bp * tp, D_MODEL), bp, tp, 0, None, None,
        zero_s, zero_hist, mkv_bf[None, :, :MEM_Q], mkv_bf[None, :, MEM_Q:], wts, cfg_p)

    cfg_s = dict(tq=ts, cache=True, tt_gdn=128, seg=ts, chunk=ts, carry=False,
                 tm_mix=4 * ts, tm=512)
    ys, ks_rot, ss_new, swa_s, clast_s = _layer(
        x_sample.reshape(bs * ts, D_MODEL), bs, ts, PAST_LEN,
        cache_swa_k[l].reshape(bs, WINDOW, SWA_KV), cache_swa_v[l].reshape(bs, WINDOW, SWA_KV),
        state_gdn[l], state_conv[l],
        cache_mem_k[l].astype(BF16).reshape(bs, N_MEM, MEM_Q),
        cache_mem_v[l].astype(BF16).reshape(bs, N_MEM, MEM_Q), wts, cfg_s)

    kv_shape = (SWA_KV_HEADS, SWA_HEAD_DIM)
    va_cols = slice(SWA_VA * LANE, (SWA_VA + 1) * LANE)

    p_k = kp_rot[tp - WINDOW:].reshape((1, bp, WINDOW) + kv_shape)
    p_v = swa_p[tp - WINDOW:, va_cols].reshape((1, bp, WINDOW) + kv_shape)
    p_c = clast_p[:, 8 - (CONV_W - 1):][None]
    p_mk = mkv[:, :MEM_Q].reshape(1, bp, N_MEM, MEM_HEADS, MEM_HEAD_DIM)
    p_mv = mkv[:, MEM_Q:].reshape(1, bp, N_MEM, MEM_HEADS, MEM_HEAD_DIM)

    ks_new = ks_rot.reshape(bs, ts, SWA_KV)
    vs_new = swa_s[:, va_cols].reshape(bs, ts, SWA_KV)
    s_k = jnp.concatenate([cache_swa_k[l].reshape(bs, WINDOW, SWA_KV), ks_new], axis=1)[:, -WINDOW:]
    s_v = jnp.concatenate([cache_swa_v[l].reshape(bs, WINDOW, SWA_KV), vs_new], axis=1)[:, -WINDOW:]
    s_c = clast_s[:, 8 - (CONV_W - 1):]

    return (yp.reshape(bp, tp, D_MODEL), ys.reshape(bs, ts, D_MODEL),
            p_k, p_v, sp_new[None], p_c, p_mk, p_mv,
            s_k.reshape((1, bs, WINDOW) + kv_shape), s_v.reshape((1, bs, WINDOW) + kv_shape),
            ss_new[None], s_c[None])
```

```python
import functools

import jax
import jax.numpy as jnp
from jax import lax
from jax.experimental import pallas as pl
from jax.experimental.pallas import tpu as pltpu

F32 = jnp.float32
BF16 = jnp.bfloat16

D_MODEL = 1024
PAST_LEN = 2048
CHUNK = 64
N_MEM = 256
EPS = 1e-6
SWA_HEADS = 16
SWA_KV_HEADS = 2
SWA_HEAD_DIM = 64
WINDOW = 128
ROPE_THETA = 10000.0
GDN_HEADS = 8
GDN_DK = 128
GDN_DV = 128
CONV_W = 4
MEM_HEADS = 4
MEM_HEAD_DIM = 256
N_BRANCH = 3
D_FF = 4 * D_MODEL

LANE = 128
SWA_Q = SWA_HEADS * SWA_HEAD_DIM
SWA_KV = SWA_KV_HEADS * SWA_HEAD_DIM
GDN_QK = GDN_HEADS * GDN_DK
GDN_V = GDN_HEADS * GDN_DV
MEM_Q = MEM_HEADS * MEM_HEAD_DIM

O_KA = SWA_Q
O_QB = O_KA + 2 * SWA_KV
O_AB = O_QB + 2 * GDN_QK + 2 * GDN_V
O_QC = O_AB + 2 * GDN_HEADS
O_END = O_QC + MEM_Q + N_BRANCH * D_MODEL
COL_QA = 0
COL_KA = O_KA // LANE
COL_VA = COL_KA + 1
COL_QB = O_QB // LANE
COL_KB = COL_QB + GDN_HEADS
COL_VB = COL_KB + GDN_HEADS
COL_ZB = COL_VB + GDN_HEADS
COL_AB = O_AB // LANE
N_PROJ = (COL_AB + 1) * LANE

VMEM_LIMIT = 48 * 1024 * 1024
GDN_VMEM_LIMIT = 58 * 1024 * 1024
PROJ_CHUNK = 256


def _dot(a, b):
    return jnp.dot(a, b, preferred_element_type=F32)


def _dot_nt(a, b):
    return lax.dot_general(a, b, (((1,), (1,)), ((), ())), preferred_element_type=F32)


def _dot_tn(a, b):
    return lax.dot_general(a, b, (((0,), (0,)), ((), ())), preferred_element_type=F32)


def _rms(x, g):
    return x * lax.rsqrt(jnp.mean(x * x, axis=-1, keepdims=True) + EPS) * g


def _sigmoid(x):
    return 1.0 / (1.0 + jnp.exp(-x))


def _resident(shape):
    return pl.BlockSpec(shape, lambda *_: (0,) * len(shape), pipeline_mode=pl.Buffered(1))


def _split3(x):
    hi = x.astype(BF16)
    r = x - hi.astype(F32)
    mid = r.astype(BF16)
    lo = (r - mid.astype(F32)).astype(BF16)
    return hi, mid, lo


def _mem_kv_kernel(m_ref, g_ref, w_ref, o_ref, ob_ref):
    h = _rms(m_ref[...], g_ref[...]).astype(BF16)
    kv = _dot(h, w_ref[...])
    o_ref[...] = kv
    ob_ref[...] = kv.astype(BF16)


def _mem_kv(mem, g_mem, w_bf16):
    n = w_bf16.shape[1]
    tn = 1024
    return pl.pallas_call(
        _mem_kv_kernel,
        grid=(n // tn,),
        in_specs=[
            pl.BlockSpec((N_MEM, D_MODEL), lambda j: (0, 0)),
            pl.BlockSpec((1, D_MODEL), lambda j: (0, 0)),
            pl.BlockSpec((D_MODEL, tn), lambda j: (0, j)),
        ],
        out_specs=[pl.BlockSpec((N_MEM, tn), lambda j: (0, j))] * 2,
        out_shape=[jax.ShapeDtypeStruct((N_MEM, n), F32), jax.ShapeDtypeStruct((N_MEM, n), BF16)],
        compiler_params=pltpu.CompilerParams(dimension_semantics=("arbitrary",), vmem_limit_bytes=VMEM_LIMIT),
        name="mem_kv",
    )(mem, g_mem, w_bf16)


def _rope(x, cos, sin_signed):
    lane = lax.broadcasted_iota(jnp.int32, x.shape, 1)
    first = (lane % SWA_HEAD_DIM) < (SWA_HEAD_DIM // 2)
    rot = jnp.where(first, pltpu.roll(x, LANE - SWA_HEAD_DIM // 2, 1), pltpu.roll(x, SWA_HEAD_DIM // 2, 1))
    return x * cos + rot * sin_signed


SWA_KA = SWA_Q // LANE
SWA_VA = SWA_KA + 1
KEYS = WINDOW + CHUNK
KPAD = 256


def _swa_kernel(sink_ref, q_ref, k_ref, v_ref, kh_ref, vh_ref, base_ref, off_ref, *rest, tq, cache, host_proj):
    if host_proj:
        (x_ref, gpre_ref, wq_ref, bq_ref, o_ref, kout_ref, qc_ref, gl_ref,
         qs_ref, ka_ref, kb_ref, va_ref, vb_ref, h_scr) = rest
    else:
        o_ref, kout_ref, qs_ref, ka_ref, kb_ref, va_ref, vb_ref = rest
    i = pl.program_id(1)
    pending = []
    if host_proj:
        h_scr[...] = _rms(x_ref[...], gpre_ref[...]).astype(BF16)
        pending = list(range((MEM_Q + N_BRANCH * D_MODEL) // PROJ_CHUNK))

    def emit(n=1):
        for _ in range(n):
            if pending:
                c = pending.pop(0)
                cols = slice(c * PROJ_CHUNK, (c + 1) * PROJ_CHUNK)
                y = _dot(h_scr[...], wq_ref[:, cols]) + bq_ref[:, cols]
                if c < MEM_Q // PROJ_CHUNK:
                    qc_ref[:, cols] = (y * (MEM_HEAD_DIM ** -0.5)).astype(BF16)
                else:
                    gl_ref[:, c * PROJ_CHUNK - MEM_Q:(c + 1) * PROJ_CHUNK - MEM_Q] = y

    cb, sb = base_ref[0:1, :], base_ref[1:2, :]
    co, so = off_ref[0], off_ref[1]
    cos_all = cb * co - sb * so
    sin_all = sb * co + cb * so
    cos, sin = cos_all[WINDOW:], sin_all[WINDOW:]
    kr = _rope(k_ref[...], cos, sin)
    kout_ref[...] = kr
    if cache:
        khr = kh_ref[...]
    else:
        khr = _rope(kh_ref[...], cos_all[:WINDOW], sin_all[:WINDOW])
    zpad = jnp.zeros((KPAD - KEYS, LANE), F32)
    kall = jnp.concatenate([khr, kr, zpad], axis=0)
    vall = jnp.concatenate([vh_ref[...], v_ref[...], zpad], axis=0)
    lo = lax.broadcasted_iota(jnp.int32, kall.shape, 1) < SWA_HEAD_DIM
    zero = jnp.zeros_like(kall)
    for src, a_ref, b_ref in ((kall, ka_ref, kb_ref), (vall, va_ref, vb_ref)):
        rolled = pltpu.roll(src, SWA_HEAD_DIM, 1)
        a_ref[0] = jnp.where(lo, src, zero).astype(BF16)
        b_ref[0] = jnp.where(lo, zero, rolled).astype(BF16)
        a_ref[1] = jnp.where(lo, rolled, zero).astype(BF16)
        b_ref[1] = jnp.where(lo, zero, src).astype(BF16)

    scale = SWA_HEAD_DIM ** -0.5
    for p in range(SWA_HEADS // 2):
        sl = slice(p * LANE, (p + 1) * LANE)
        qs_ref[:, sl] = (_rope(q_ref[:, sl], cos, sin) * scale).astype(BF16)

    jj = lax.broadcasted_iota(jnp.int32, (CHUNK, KPAD), 1)
    pairs_per_kv = SWA_HEADS // SWA_KV_HEADS // 2
    inst = [(c, g) for c in range(tq // CHUNK) for g in range(SWA_KV_HEADS)]

    def scores(c, g):
        rows = slice(c * CHUNK, (c + 1) * CHUNK)
        krows = slice(c * CHUNK, c * CHUNK + KPAD)
        kbd = jnp.concatenate([ka_ref[g, krows, :], kb_ref[g, krows, :]], axis=0)
        qst = jnp.concatenate(
            [qs_ref[rows, (g * pairs_per_kv + j) * LANE:(g * pairs_per_kv + j + 1) * LANE]
             for j in range(pairs_per_kv)], axis=0)
        return _dot_nt(qst, kbd)

    def probs(c, g, s):
        if cache:
            valid = jj < KEYS
        else:
            valid = (jj < KEYS) & ((i > 0) | (jj + c * CHUNK >= WINDOW))
        pblocks = []
        for j in range(pairs_per_kv):
            halves = []
            for half in range(2):
                sh = s[j * CHUNK:(j + 1) * CHUNK, half * KPAD:(half + 1) * KPAD]
                sh = jnp.where(valid, sh, -jnp.inf)
                sink = sink_ref[2 * (g * pairs_per_kv + j) + half]
                m = jnp.maximum(jnp.max(sh, axis=-1, keepdims=True), sink)
                e = jnp.exp(sh - m)
                den = jnp.sum(e, axis=-1, keepdims=True) + jnp.exp(sink - m)
                halves.append((e * (1.0 / den)).astype(BF16))
            pblocks.append(jnp.concatenate(halves, axis=1))
        return jnp.concatenate(pblocks, axis=0)

    def outputs(c, g, pmat):
        rows = slice(c * CHUNK, (c + 1) * CHUNK)
        krows = slice(c * CHUNK, c * CHUNK + KPAD)
        vbd = jnp.concatenate([va_ref[g, krows, :], vb_ref[g, krows, :]], axis=0)
        o = _dot(pmat, vbd)
        for j in range(pairs_per_kv):
            p = g * pairs_per_kv + j
            o_ref[rows, p * LANE:(p + 1) * LANE] = o[j * CHUNK:(j + 1) * CHUNK].astype(BF16)

    s_l, p_l = {}, {}
    for step in range(len(inst) + 2):
        if step < len(inst):
            s_l[step] = scores(*inst[step])
        if 0 <= step - 1 < len(inst):
            p_l[step - 1] = probs(*inst[step - 1], s_l.pop(step - 1))
        if 0 <= step - 2 < len(inst):
            outputs(*inst[step - 2], p_l.pop(step - 2))
        emit()
    emit(len(pending))


def _swa(proj, k_hist, v_hist, rope_base, rope_off, sinks, nb, t, tq, cache, host=None):
    nt = t // tq
    kv_rows = WINDOW + tq + (KPAD - KEYS)
    if cache:
        hist_k_spec = pl.BlockSpec((None, WINDOW, LANE), lambda b, i: (b, 0, 0))
        hist_v_spec = pl.BlockSpec((None, WINDOW, LANE), lambda b, i: (b, 0, 0))
        hk, hv = k_hist, v_hist
    else:
        r = tq // WINDOW

        def hidx(b, i):
            return jnp.maximum((b * nt + i) * r - 1, 0)

        hist_k_spec = pl.BlockSpec((WINDOW, LANE), lambda b, i: (hidx(b, i), SWA_KA))
        hist_v_spec = pl.BlockSpec((WINDOW, LANE), lambda b, i: (hidx(b, i), SWA_VA))
        hk, hv = proj, proj

    def rows(width):
        return pl.BlockSpec((tq, width), lambda b, i: (b * nt + i, 0))

    in_specs = [
        pl.BlockSpec(memory_space=pltpu.SMEM),
        rows(SWA_Q),
        pl.BlockSpec((tq, LANE), lambda b, i: (b * nt + i, SWA_KA)),
        pl.BlockSpec((tq, LANE), lambda b, i: (b * nt + i, SWA_VA)),
        hist_k_spec,
        hist_v_spec,
        pl.BlockSpec((None, 2, LANE), lambda b, i: (i, 0, 0)),
        pl.BlockSpec((2, WINDOW + tq, LANE), lambda b, i: (0, 0, 0)),
    ]
    args = [sinks, proj, proj, proj, hk, hv, rope_base, rope_off]
    out_specs = [rows(SWA_Q), rows(LANE)]
    out_shape = [jax.ShapeDtypeStruct((nb * t, SWA_Q), BF16), jax.ShapeDtypeStruct((nb * t, LANE), F32)]
    scratch = [
        pltpu.VMEM((tq, SWA_Q), BF16),
        pltpu.VMEM((SWA_KV_HEADS, kv_rows, LANE), BF16),
        pltpu.VMEM((SWA_KV_HEADS, kv_rows, LANE), BF16),
        pltpu.VMEM((SWA_KV_HEADS, kv_rows, LANE), BF16),
        pltpu.VMEM((SWA_KV_HEADS, kv_rows, LANE), BF16),
    ]
    if host is not None:
        x, gpre, wq, bq = host
        nq = MEM_Q + N_BRANCH * D_MODEL
        in_specs += [rows(D_MODEL), pl.BlockSpec((1, D_MODEL), lambda b, i: (0, 0)),
                     _resident((D_MODEL, nq)), _resident((1, nq))]
        args += [x, gpre, wq, bq]
        out_specs += [rows(MEM_Q), rows(N_BRANCH * D_MODEL)]
        out_shape += [jax.ShapeDtypeStruct((nb * t, MEM_Q), BF16),
                      jax.ShapeDtypeStruct((nb * t, N_BRANCH * D_MODEL), F32)]
        scratch += [pltpu.VMEM((tq, D_MODEL), BF16)]
    return pl.pallas_call(
        functools.partial(_swa_kernel, tq=tq, cache=cache, host_proj=host is not None),
        grid=(nb, nt),
        in_specs=in_specs,
        out_specs=out_specs,
        out_shape=out_shape,
        scratch_shapes=scratch,
        compiler_params=pltpu.CompilerParams(dimension_semantics=("parallel", "arbitrary"),
                                             vmem_limit_bytes=GDN_VMEM_LIMIT if host is not None else VMEM_LIMIT),
        name="swa",
    )(*args)


INV_BASE = 16


def _inverse_minus_identity(pms, row, col, chunk, between=lambda: None):
    base = (row // INV_BASE) == (col // INV_BASE)
    diag = row == col
    ms = [jnp.where(base, pm, 0.0) for pm in pms]
    ts = [jnp.where(diag, 1.0, m) for m in ms]
    for _ in range(INV_BASE.bit_length() - 2):
        mbs = [m.astype(BF16) for m in ms]
        ms = [_dot(mb, mb) for mb in mbs]
        between()
        ts = [t + _dot(m.astype(BF16), t.astype(BF16)) for m, t in zip(ms, ts)]
    bs = INV_BASE
    while bs < chunk:
        inner = (row // bs) == (col // bs)
        outer = (row // (2 * bs)) == (col // (2 * bs))
        sib = outer & jnp.logical_not(inner)
        xs = [_dot(jnp.where(sib, pm, 0.0).astype(BF16), t.astype(BF16)) for pm, t in zip(pms, ts)]
        ts = [t + _dot(t.astype(BF16), x.astype(BF16)) for t, x in zip(ts, xs)]
        between()
        bs *= 2
    rs = [jnp.where(diag, 0.0, t) for t in ts]
    return rs


def _gdn_kernel(x0_ref, xn_ref, gpre_ref, w_ref, b_ref, hist_ref, cw_ref, alog_ref, dt_ref, ng_ref, s0_ref,
                o_ref, sout_ref, swa_ref, clast_ref, proj_scr, h_scr, s_scr, carry_scr, ext_scr, conv_scr,
                *, tt, seg, chunk, carry):
    i = pl.program_id(0)
    nseg = tt // seg
    nblk = LANE // chunk
    ngroups = tt // LANE
    hist_rows = CONV_W - 1
    cur = proj_scr.at[i % 2]
    nxt = proj_scr.at[1 - i % 2]

    nchunks = pl.cdiv(N_PROJ, PROJ_CHUNK)

    def project(dst, c):
        cols = slice(c * PROJ_CHUNK, min((c + 1) * PROJ_CHUNK, N_PROJ))
        dst[8:8 + tt, cols] = _dot(h_scr[...], w_ref[:, cols]) + b_ref[:, cols]

    @pl.when(i == 0)
    def _():
        h_scr[...] = _rms(x0_ref[...], gpre_ref[...]).astype(BF16)
        for c in range(nchunks):
            project(proj_scr.at[0], c)

    h_scr[...] = _rms(xn_ref[...], gpre_ref[...]).astype(BF16)
    first = COL_QB * LANE // PROJ_CHUNK
    pending = list(range(first, nchunks - 1)) + list(range(first)) + [nchunks - 1]

    def emit(n=1):
        for _ in range(n):
            if pending:
                project(nxt, pending.pop(0))

    def init():
        for s in range(nseg):
            s_scr[s] = s0_ref[s]
            carry_scr[s] = jnp.zeros((8, 3 * GDN_QK), F32)
            carry_scr[s, 8 - hist_rows:8, :] = hist_ref[s]

    if carry:
        pl.when(i == 0)(init)
    else:
        init()

    swa_ref[...] = cur[8:8 + tt, COL_QA * LANE:(COL_VA + 1) * LANE]

    qkv = slice(COL_QB * LANE, COL_ZB * LANE)
    base = 8 - hist_rows
    for s in range(nseg):
        if carry:
            cur[0:8, qkv] = carry_scr[s]
            src, col0 = cur, COL_QB * LANE
            last = cur[tt:tt + 8, qkv]
            carry_scr[s] = last
        else:
            ext_scr[0:8, :] = carry_scr[s]
            ext_scr[8:8 + seg, :] = cur[8 + s * seg:8 + (s + 1) * seg, qkv]
            src, col0 = ext_scr, 0
            last = ext_scr[seg:seg + 8, :]
        clast_ref[s] = last
        for cb in range(3 * GDN_HEADS):
            lanes = slice(cb * LANE, (cb + 1) * LANE)
            slanes = slice(col0 + cb * LANE, col0 + (cb + 1) * LANE)
            y = src[base:base + seg, slanes] * cw_ref[0:1, lanes]
            for tap in range(1, CONV_W):
                y = y + src[base + tap:base + tap + seg, slanes] * cw_ref[tap:tap + 1, lanes]
            conv_scr[s * seg:(s + 1) * seg, lanes] = y * _sigmoid(y)
            if s == 0 and cb % 2 == 0:
                emit()

    row = lax.broadcasted_iota(jnp.int32, (LANE, LANE), 0)
    col = lax.broadcasted_iota(jnp.int32, (LANE, LANE), 1)
    same = (row // chunk) == (col // chunk)
    incl = same & (row >= col)
    strict = same & (row > col)
    lmask = jnp.where(incl, 1.0, 0.0).astype(BF16)
    row2 = jnp.concatenate([row, row], axis=0)

    inst = [(gi, hh) for gi in range(ngroups) for hh in range(GDN_HEADS)]
    heads = range(GDN_HEADS)

    gates = []
    for gi in range(ngroups):
        ab = cur[8 + gi * LANE:8 + (gi + 1) * LANE, COL_AB * LANE:(COL_AB + 1) * LANE]
        xa = ab + dt_ref[...]
        softplus = jnp.maximum(xa, 0.0) + jnp.log1p(jnp.exp(-jnp.abs(xa)))
        la = -jnp.exp(alog_ref[...]) * softplus
        beta = _sigmoid(ab)
        g_all = sum(_dot(lmask, piece) for piece in _split3(la))
        gates.append((g_all, g_all.T, beta))

    qn_l, kn_l, v_l, gb_l, bb_l, decay_l = [], [], [], [], [], []
    for gi, hh in inst:
        rows = slice(gi * LANE, (gi + 1) * LANE)
        q = conv_scr[rows, hh * LANE:(hh + 1) * LANE]
        k = conv_scr[rows, GDN_QK + hh * LANE:GDN_QK + (hh + 1) * LANE]
        v_l.append(conv_scr[rows, 2 * GDN_QK + hh * LANE:2 * GDN_QK + (hh + 1) * LANE])
        qn_l.append(q * (lax.rsqrt(jnp.sum(q * q, axis=-1, keepdims=True) + EPS) * (GDN_DK ** -0.5)))
        kn_l.append(k * lax.rsqrt(jnp.sum(k * k, axis=-1, keepdims=True) + EPS))
        g_all, g_all_t, beta = gates[gi]
        gb = jnp.broadcast_to(g_all[:, hh:hh + 1], (LANE, LANE))
        grow = jnp.broadcast_to(g_all_t[hh:hh + 1, :], (LANE, LANE))
        gb_l.append(gb)
        bb_l.append(jnp.broadcast_to(beta[:, GDN_HEADS + hh:GDN_HEADS + hh + 1], (LANE, LANE)))
        decay_l.append(jnp.exp(jnp.where(incl, gb - grow, -jnp.inf)))

    emit()
    knb_l = [kn.astype(BF16) for kn in kn_l]
    cat_l = [_dot_nt(jnp.concatenate([qn.astype(BF16), knb], axis=0), knb)
             for qn, knb in zip(qn_l, knb_l)]
    emit()
    qk_l = [cat[:LANE] * decay for cat, decay in zip(cat_l, decay_l)]
    pm_l = [jnp.where(strict, -(bb * cat[LANE:] * decay), 0.0)
            for bb, cat, decay in zip(bb_l, cat_l, decay_l)]
    r_l = _inverse_minus_identity(pm_l, row, col, chunk)
    emit()
    eg_l = [jnp.exp(gb) for gb in gb_l]
    rhs_l = [jnp.concatenate([v * bb, kn * (bb * eg)], axis=1)
             for v, bb, kn, eg in zip(v_l, bb_l, kn_l, eg_l)]
    sol_l = [rhs + _dot(r.astype(BF16), rhs.astype(BF16)) for r, rhs in zip(r_l, rhs_l)]
    wq_l = [jnp.concatenate([sol[:, GDN_DV:], qn * eg], axis=0).astype(BF16)
            for sol, qn, eg in zip(sol_l, qn_l, eg_l)]
    kd_l = []
    for kn, gb in zip(kn_l, gb_l):
        glast = gb[chunk - 1:chunk, :]
        for blk in range(1, nblk):
            glast = jnp.where(row >= blk * chunk, gb[(blk + 1) * chunk - 1:(blk + 1) * chunk, :], glast)
        kd_l.append(kn * jnp.exp(glast - gb))

    emit()
    nstate = 1 if carry else nblk
    state = [[s_scr[sidx, hh] for hh in heads] for sidx in range(nstate)]
    for gi in range(ngroups):
        ix = [gi * GDN_HEADS + hh for hh in heads]
        ws_l = []
        for hh in heads:
            ws = None
            for blk in range(nblk):
                part = _dot(wq_l[ix[hh]], state[0 if carry else blk][hh].astype(BF16))
                ws = part if ws is None else jnp.where(row2 >= blk * chunk, part, ws)
            ws_l.append(ws)
        emit()
        vnb_l = [(sol_l[ix[hh]][:, :GDN_DV] - ws_l[hh][:LANE]).astype(BF16) for hh in heads]
        o_l = [ws_l[hh][LANE:] + _dot(qk_l[ix[hh]].astype(BF16), vnb_l[hh]) for hh in heads]
        emit()
        for blk in range(nblk):
            sidx = 0 if carry else blk
            new = []
            for hh in heads:
                kd = kd_l[ix[hh]]
                gb = gb_l[ix[hh]]
                kdb = kd if nblk == 1 else jnp.where((row // chunk) == blk, kd, 0.0)
                gtot = jnp.exp(gb[(blk + 1) * chunk - 1:(blk + 1) * chunk, :])
                new.append(state[sidx][hh] * gtot + _dot_tn(kdb.astype(BF16), vnb_l[hh]))
            state[sidx] = new
        for hh in heads:
            rows = slice(gi * LANE, (gi + 1) * LANE)
            lanes = slice(hh * LANE, (hh + 1) * LANE)
            z = cur[8 + gi * LANE:8 + (gi + 1) * LANE, COL_ZB * LANE + hh * LANE:COL_ZB * LANE + (hh + 1) * LANE]
            o_ref[rows, lanes] = (_rms(o_l[hh], ng_ref[...]) * (z * _sigmoid(z))).astype(BF16)
    emit(len(pending))
    for sidx in range(nstate):
        for hh in heads:
            s_scr[sidx, hh] = state[sidx][hh]

    def fin():
        for s in range(nseg):
            sout_ref[s] = s_scr[s]

    if carry:
        pl.when(i == pl.num_programs(0) - 1)(fin)
    else:
        fin()


def _gdn(x, gpre, w_bf16, b, hist, s0, conv_w, alog_row, dt_row, norm_g, tt, seg, chunk, carry):
    t_rows = x.shape[0]
    nt = t_rows // tt
    nseg = tt // seg
    nseq = s0.shape[0]
    if carry:
        seq_idx = lambda i: 0
    else:
        seq_idx = lambda i: i
    swa_cols = (COL_VA + 1 - COL_QA) * LANE
    row_spec = pl.BlockSpec((1, LANE), lambda i: (0, 0))
    sspec = pl.BlockSpec((nseg, GDN_HEADS, GDN_DK, GDN_DV), lambda i: (seq_idx(i), 0, 0, 0))
    cspec = pl.BlockSpec((nseg, 8, 3 * GDN_QK), lambda i: (seq_idx(i), 0, 0))
    return pl.pallas_call(
        functools.partial(_gdn_kernel, tt=tt, seg=seg, chunk=chunk, carry=carry),
        grid=(nt,),
        in_specs=[
            pl.BlockSpec((tt, D_MODEL), lambda i: (0, 0)),
            pl.BlockSpec((tt, D_MODEL), lambda i: (jnp.minimum(i + 1, nt - 1), 0)),
            pl.BlockSpec((1, D_MODEL), lambda i: (0, 0)),
            _resident((D_MODEL, N_PROJ)),
            _resident((1, N_PROJ)),
            pl.BlockSpec((nseg, CONV_W - 1, 3 * GDN_QK), lambda i: (seq_idx(i), 0, 0)),
            pl.BlockSpec((CONV_W, 3 * GDN_QK), lambda i: (0, 0)),
            row_spec, row_spec, row_spec,
            sspec,
        ],
        out_specs=[
            pl.BlockSpec((tt, GDN_V), lambda i: (i, 0)),
            sspec,
            pl.BlockSpec((tt, swa_cols), lambda i: (i, 0)),
            cspec,
        ],
        out_shape=[
            jax.ShapeDtypeStruct((t_rows, GDN_V), BF16),
            jax.ShapeDtypeStruct((nseq, GDN_HEADS, GDN_DK, GDN_DV), F32),
            jax.ShapeDtypeStruct((t_rows, swa_cols), F32),
            jax.ShapeDtypeStruct((nseq, 8, 3 * GDN_QK), F32),
        ],
        scratch_shapes=[
            pltpu.VMEM((2, 8 + tt, N_PROJ), F32),
            pltpu.VMEM((tt, D_MODEL), BF16),
            pltpu.VMEM((nseg, GDN_HEADS, GDN_DK, GDN_DV), F32),
            pltpu.VMEM((nseg, 8, 3 * GDN_QK), F32),
            pltpu.VMEM((8 + seg, 3 * GDN_QK), F32),
            pltpu.VMEM((tt, 3 * GDN_QK), F32),
        ],
        compiler_params=pltpu.CompilerParams(dimension_semantics=("arbitrary",),
                                             vmem_limit_bytes=GDN_VMEM_LIMIT),
        name="gdn",
    )(x, x, gpre, w_bf16, b, hist, conv_w, alog_row, dt_row, norm_g, s0)


def _mix_kernel(x_ref, oa_ref, ob_ref, mk_ref, mv_ref, *rest, hosted):
    if hosted:
        qc_ref, gl_ref, wb_ref, wo_ref, gp_ref, o_ref = rest
    else:
        gpre_ref, wq_ref, bq_ref, wb_ref, wo_ref, gp_ref, o_ref = rest
    x = x_ref[...]
    tm = x.shape[0]
    nmem = mk_ref.shape[0]
    rpm = tm // nmem
    if hosted:
        qc = qc_ref[...]
    else:
        h = _rms(x, gpre_ref[...]).astype(BF16)
        qc = ((_dot(h, wq_ref[:, :MEM_Q]) + bq_ref[:, :MEM_Q]) * (MEM_HEAD_DIM ** -0.5)).astype(BF16)
    blocks = []
    for b in range(nmem):
        heads = []
        for hh in range(MEM_HEADS):
            sl = slice(hh * MEM_HEAD_DIM, (hh + 1) * MEM_HEAD_DIM)
            s = _dot_nt(qc[b * rpm:(b + 1) * rpm, sl], mk_ref[b, :, sl])
            m = jnp.max(s, axis=-1, keepdims=True)
            e = jnp.exp(s - m)
            p = e * (1.0 / jnp.sum(e, axis=-1, keepdims=True))
            heads.append(_dot(p.astype(BF16), mv_ref[b, :, sl]).astype(BF16))
        blocks.append(jnp.concatenate(heads, axis=1))
    oc = blocks[0] if nmem == 1 else jnp.concatenate(blocks, axis=0)
    merged = None
    for n, br_in in enumerate((oa_ref[...], ob_ref[...], oc)):
        if hosted:
            gl = gl_ref[:, n * D_MODEL:(n + 1) * D_MODEL]
        else:
            cols = slice(MEM_Q + n * D_MODEL, MEM_Q + (n + 1) * D_MODEL)
            gl = _dot(h, wq_ref[:, cols]) + bq_ref[:, cols]
        term = _sigmoid(gl) * _dot(br_in, wb_ref[n])
        merged = term if merged is None else merged + term
    y = _dot(merged.astype(BF16), wo_ref[...])
    o_ref[...] = x + _rms(y, gp_ref[...])


def _mix(x, oa, ob, mk, mv, proj, wb, wo, gp, tm, rows_per_mem):
    m = x.shape[0]
    tile = pl.BlockSpec((tm, D_MODEL), lambda i: (i, 0))
    vec = pl.BlockSpec((1, D_MODEL), lambda i: (0, 0))
    if mk.shape[0] == 1:
        mspec = _resident((1, N_MEM, MEM_Q))
    else:
        mspec = pl.BlockSpec((tm // rows_per_mem, N_MEM, MEM_Q), lambda i: (i, 0, 0))
    nq = MEM_Q + N_BRANCH * D_MODEL
    hosted = len(proj) == 2
    if hosted:
        pspecs = [tile, pl.BlockSpec((tm, N_BRANCH * D_MODEL), lambda i: (i, 0))]
    else:
        pspecs = [vec, _resident((D_MODEL, nq)), _resident((1, nq))]
    return pl.pallas_call(
        functools.partial(_mix_kernel, hosted=hosted),
        grid=(m // tm,),
        in_specs=[tile, tile, tile, mspec, mspec] + pspecs + [
            _resident((N_BRANCH, D_MODEL, D_MODEL)),
            _resident((D_MODEL, D_MODEL)),
            vec,
        ],
        out_specs=tile,
        out_shape=jax.ShapeDtypeStruct((m, D_MODEL), F32),
        compiler_params=pltpu.CompilerParams(dimension_semantics=("parallel",), vmem_limit_bytes=VMEM_LIMIT),
        name="mix",
    )(x, oa, ob, mk, mv, *proj, wb, wo, gp)


FF_CHUNK = 1024


def _ffn_kernel(x_ref, g1_ref, wu_ref, wd_ref, g2_ref, o_ref):
    x = x_ref[...]
    hf = _rms(x, g1_ref[...]).astype(BF16)
    f = None
    for c in range(D_FF // FF_CHUNK):
        sl = slice(c * FF_CHUNK, (c + 1) * FF_CHUNK)
        up = jnp.maximum(_dot(hf, wu_ref[:, sl]), 0.0)
        part = _dot((up * up).astype(BF16), wd_ref[sl, :])
        f = part if f is None else f + part
    o_ref[...] = x + _rms(f, g2_ref[...])


def _ffn(x, g1, wu, wd, g2, tm):
    m = x.shape[0]
    tile = pl.BlockSpec((tm, D_MODEL), lambda i: (i, 0))
    vec = pl.BlockSpec((1, D_MODEL), lambda i: (0, 0))
    return pl.pallas_call(
        _ffn_kernel,
        grid=(m // tm,),
        in_specs=[tile, vec, _resident((D_MODEL, D_FF)), _resident((D_FF, D_MODEL)), vec],
        out_specs=tile,
        out_shape=jax.ShapeDtypeStruct((m, D_MODEL), F32),
        compiler_params=pltpu.CompilerParams(dimension_semantics=("parallel",), vmem_limit_bytes=VMEM_LIMIT),
        name="ffn",
    )(x, g1, wu, wd, g2)


def _rope_tables(pos0, nt, tq):
    half = SWA_HEAD_DIM // 2
    inv = ROPE_THETA ** (-jnp.arange(half, dtype=F32) / half)
    inv = jnp.tile(inv, LANE // half)
    sgn = jnp.tile(jnp.concatenate([-jnp.ones((half,), F32), jnp.ones((half,), F32)]), LANE // SWA_HEAD_DIM)
    base = (pos0 + tq * jnp.arange(nt, dtype=jnp.int32)).astype(F32)[:, None] * inv[None, :]
    off = jnp.arange(-WINDOW, tq, dtype=jnp.int32).astype(F32)[:, None] * inv[None, :]
    rope_base = jnp.stack([jnp.cos(base), sgn * jnp.sin(base)], axis=1)
    rope_off = jnp.stack([jnp.cos(off), sgn * jnp.sin(off)], axis=0)
    return rope_base, rope_off


def _pad_row(v, offset=0):
    return jnp.zeros((1, LANE), F32).at[0, offset:offset + v.shape[0]].set(v)


def _layer(x, nb, t, pos0, k_hist, v_hist, s0, conv_hist, mk, mv, wts, cfg):
    ob, s_new, swa_in, conv_last = _gdn(x, wts['g_pre_mix'], wts['w_in'], wts['b_in'], conv_hist, s0,
                                        wts['conv_w'], wts['alog'], wts['dt'], wts['norm_g'],
                                        cfg['tt_gdn'], cfg['seg'], cfg['chunk'], cfg['carry'])
    rope_base, rope_off = _rope_tables(pos0, t // cfg['tq'], cfg['tq'])
    qproj = (wts['g_pre_mix'], wts['w_q'], wts['b_q'])
    if cfg['host_proj']:
        oa, k_rot, qc, gl = _swa(swa_in, k_hist, v_hist, rope_base, rope_off, wts['sinks'], nb, t, cfg['tq'],
                                 cfg['cache'], host=(x,) + qproj)
        qproj = (qc, gl)
    else:
        oa, k_rot = _swa(swa_in, k_hist, v_hist, rope_base, rope_off, wts['sinks'], nb, t, cfg['tq'],
                         cfg['cache'])
    x1 = _mix(x, oa, ob, mk, mv, qproj, wts['w_branch'], wts['w_out'], wts['g_post_mix'], cfg['tm_mix'], t)
    y = _ffn(x1, wts['g_pre_ffn'], wts['w_up'], wts['w_down'], wts['g_post_ffn'], cfg['tm'])
    return y, k_rot, s_new, swa_in, conv_last


def kernel(x_prompt, x_sample, mem_prompt, cache_swa_k, cache_swa_v, state_gdn, state_conv, cache_mem_k, cache_mem_v, g_pre_mix, w_in, b_in, swa_sinks, conv_w, gdn_a_log, gdn_dt_bias, gdn_norm_g, g_mem, w_mem_kv, w_branch, w_out, g_post_mix, g_pre_ffn, w_up, w_down, g_post_ffn):
    depth = w_in.shape[0]
    assert depth == 1
    l = 0
    bp, tp, _ = x_prompt.shape
    bs, ts, _ = x_sample.shape
    assert bp == 1

    wts = {
        'g_pre_mix': g_pre_mix[l][None, :],
        'w_in': w_in[l][:, :N_PROJ].astype(BF16),
        'b_in': b_in[l][None, :N_PROJ],
        'w_q': w_in[l][:, O_QC:O_END].astype(BF16),
        'b_q': b_in[l][None, O_QC:O_END],
        'sinks': swa_sinks[l],
        'conv_w': conv_w[l],
        'alog': _pad_row(gdn_a_log[l]),
        'dt': _pad_row(gdn_dt_bias[l]),
        'norm_g': gdn_norm_g[l][None, :],
        'w_branch': w_branch[l].astype(BF16),
        'w_out': w_out[l].astype(BF16),
        'g_post_mix': g_post_mix[l][None, :],
        'g_pre_ffn': g_pre_ffn[l][None, :],
        'w_up': w_up[l].astype(BF16),
        'w_down': w_down[l].astype(BF16),
        'g_post_ffn': g_post_ffn[l][None, :],
    }

    mkv, mkv_bf = _mem_kv(mem_prompt[0], g_mem[l][None, :], w_mem_kv[l].astype(BF16))
    cfg_p = dict(tq=512, cache=False, host_proj=True, tt_gdn=256, seg=256, chunk=128, carry=True,
                 tm_mix=512, tm=512)
    zero_hist = jnp.zeros((1, CONV_W - 1, 3 * GDN_QK), F32)
    zero_s = jnp.zeros((1, GDN_HEADS, GDN_DK, GDN_DV), F32)
    yp, kp_rot, sp_new, swa_p, clast_p = _layer(
        x_prompt.reshape(bp * tp, D_MODEL), bp, tp, 0, None, None,
        zero_s, zero_hist, mkv_bf[None, :, :MEM_Q], mkv_bf[None, :, MEM_Q:], wts, cfg_p)

    cfg_s = dict(tq=ts, cache=True, host_proj=False, tt_gdn=128, seg=ts, chunk=ts, carry=False,
                 tm_mix=4 * ts, tm=512)
    ys, ks_rot, ss_new, swa_s, clast_s = _layer(
        x_sample.reshape(bs * ts, D_MODEL), bs, ts, PAST_LEN,
        cache_swa_k[l].reshape(bs, WINDOW, SWA_KV), cache_swa_v[l].reshape(bs, WINDOW, SWA_KV),
        state_gdn[l], state_conv[l],
        cache_mem_k[l].astype(BF16).reshape(bs, N_MEM, MEM_Q),
        cache_mem_v[l].astype(BF16).reshape(bs, N_MEM, MEM_Q), wts, cfg_s)

    kv_shape = (SWA_KV_HEADS, SWA_HEAD_DIM)
    va_cols = slice(SWA_VA * LANE, (SWA_VA + 1) * LANE)

    p_k = kp_rot[tp - WINDOW:].reshape((1, bp, WINDOW) + kv_shape)
    p_v = swa_p[tp - WINDOW:, va_cols].reshape((1, bp, WINDOW) + kv_shape)
    p_c = clast_p[:, 8 - (CONV_W - 1):][None]
    p_mk = mkv[:, :MEM_Q].reshape(1, bp, N_MEM, MEM_HEADS, MEM_HEAD_DIM)
    p_mv = mkv[:, MEM_Q:].reshape(1, bp, N_MEM, MEM_HEADS, MEM_HEAD_DIM)

    ks_new = ks_rot.reshape(bs, ts, SWA_KV)
    vs_new = swa_s[:, va_cols].reshape(bs, ts, SWA_KV)
    s_k = jnp.concatenate([cache_swa_k[l].reshape(bs, WINDOW, SWA_KV), ks_new], axis=1)[:, -WINDOW:]
    s_v = jnp.concatenate([cache_swa_v[l].reshape(bs, WINDOW, SWA_KV), vs_new], axis=1)[:, -WINDOW:]
    s_c = clast_s[:, 8 - (CONV_W - 1):]

    return (yp.reshape(bp, tp, D_MODEL), ys.reshape(bs, ts, D_MODEL),
            p_k, p_v, sp_new[None], p_c, p_mk, p_mv,
            s_k.reshape((1, bs, WINDOW) + kv_shape), s_v.reshape((1, bs, WINDOW) + kv_shape),
            ss_new[None], s_c[None])
```

```python
import functools

import jax
import jax.numpy as jnp
from jax import lax
from jax.experimental import pallas as pl
from jax.experimental.pallas import tpu as pltpu

F32 = jnp.float32
BF16 = jnp.bfloat16

D_MODEL = 1024
PAST_LEN = 2048
CHUNK = 64
N_MEM = 256
EPS = 1e-6
SWA_HEADS = 16
SWA_KV_HEADS = 2
SWA_HEAD_DIM = 64
WINDOW = 128
ROPE_THETA = 10000.0
GDN_HEADS = 8
GDN_DK = 128
GDN_DV = 128
CONV_W = 4
MEM_HEADS = 4
MEM_HEAD_DIM = 256
N_BRANCH = 3
D_FF = 4 * D_MODEL

LANE = 128
SWA_Q = SWA_HEADS * SWA_HEAD_DIM
SWA_KV = SWA_KV_HEADS * SWA_HEAD_DIM
GDN_QK = GDN_HEADS * GDN_DK
GDN_V = GDN_HEADS * GDN_DV
MEM_Q = MEM_HEADS * MEM_HEAD_DIM

O_KA = SWA_Q
O_QB = O_KA + 2 * SWA_KV
O_AB = O_QB + 2 * GDN_QK + 2 * GDN_V
O_QC = O_AB + 2 * GDN_HEADS
O_END = O_QC + MEM_Q + N_BRANCH * D_MODEL
COL_QA = 0
COL_KA = O_KA // LANE
COL_VA = COL_KA + 1
COL_QB = O_QB // LANE
COL_KB = COL_QB + GDN_HEADS
COL_VB = COL_KB + GDN_HEADS
COL_ZB = COL_VB + GDN_HEADS
COL_AB = O_AB // LANE
N_PROJ = (COL_AB + 1) * LANE

VMEM_LIMIT = 48 * 1024 * 1024
GDN_VMEM_LIMIT = 58 * 1024 * 1024
PROJ_CHUNK = 256


def _dot(a, b):
    return jnp.dot(a, b, preferred_element_type=F32)


def _dot_nt(a, b):
    return lax.dot_general(a, b, (((1,), (1,)), ((), ())), preferred_element_type=F32)


def _dot_tn(a, b):
    return lax.dot_general(a, b, (((0,), (0,)), ((), ())), preferred_element_type=F32)


def _rms(x, g):
    return x * lax.rsqrt(jnp.mean(x * x, axis=-1, keepdims=True) + EPS) * g


def _sigmoid(x):
    return 1.0 / (1.0 + jnp.exp(-x))


def _resident(shape):
    return pl.BlockSpec(shape, lambda *_: (0,) * len(shape), pipeline_mode=pl.Buffered(1))


def _split3(x):
    hi = x.astype(BF16)
    r = x - hi.astype(F32)
    mid = r.astype(BF16)
    lo = (r - mid.astype(F32)).astype(BF16)
    return hi, mid, lo


def _mem_kv_kernel(m_ref, g_ref, w_ref, o_ref, ob_ref):
    h = _rms(m_ref[...], g_ref[...]).astype(BF16)
    kv = _dot(h, w_ref[...])
    o_ref[...] = kv
    ob_ref[...] = kv.astype(BF16)


def _mem_kv(mem, g_mem, w_bf16):
    n = w_bf16.shape[1]
    tn = 1024
    return pl.pallas_call(
        _mem_kv_kernel,
        grid=(n // tn,),
        in_specs=[
            pl.BlockSpec((N_MEM, D_MODEL), lambda j: (0, 0)),
            pl.BlockSpec((1, D_MODEL), lambda j: (0, 0)),
            pl.BlockSpec((D_MODEL, tn), lambda j: (0, j)),
        ],
        out_specs=[pl.BlockSpec((N_MEM, tn), lambda j: (0, j))] * 2,
        out_shape=[jax.ShapeDtypeStruct((N_MEM, n), F32), jax.ShapeDtypeStruct((N_MEM, n), BF16)],
        compiler_params=pltpu.CompilerParams(dimension_semantics=("arbitrary",), vmem_limit_bytes=VMEM_LIMIT),
        name="mem_kv",
    )(mem, g_mem, w_bf16)


def _rope(x, cos, sin_signed):
    lane = lax.broadcasted_iota(jnp.int32, x.shape, 1)
    first = (lane % SWA_HEAD_DIM) < (SWA_HEAD_DIM // 2)
    rot = jnp.where(first, pltpu.roll(x, LANE - SWA_HEAD_DIM // 2, 1), pltpu.roll(x, SWA_HEAD_DIM // 2, 1))
    return x * cos + rot * sin_signed


SWA_KA = SWA_Q // LANE
SWA_VA = SWA_KA + 1
KEYS = WINDOW + CHUNK
KPAD = 256


def _swa_kernel(sink_ref, q_ref, k_ref, v_ref, kh_ref, vh_ref, base_ref, off_ref, *rest, tq, cache, host_proj):
    if host_proj:
        (x_ref, gpre_ref, wq_ref, bq_ref, o_ref, kout_ref, qc_ref, gl_ref,
         qs_ref, ka_ref, kb_ref, va_ref, vb_ref, h_scr) = rest
    else:
        o_ref, kout_ref, qs_ref, ka_ref, kb_ref, va_ref, vb_ref = rest
    i = pl.program_id(1)
    pending = []
    if host_proj:
        h_scr[...] = _rms(x_ref[...], gpre_ref[...]).astype(BF16)
        pending = list(range((MEM_Q + N_BRANCH * D_MODEL) // PROJ_CHUNK))

    def emit(n=1):
        for _ in range(n):
            if pending:
                c = pending.pop(0)
                cols = slice(c * PROJ_CHUNK, (c + 1) * PROJ_CHUNK)
                y = _dot(h_scr[...], wq_ref[:, cols]) + bq_ref[:, cols]
                if c < MEM_Q // PROJ_CHUNK:
                    qc_ref[:, cols] = (y * (MEM_HEAD_DIM ** -0.5)).astype(BF16)
                else:
                    gl_ref[:, c * PROJ_CHUNK - MEM_Q:(c + 1) * PROJ_CHUNK - MEM_Q] = y

    cb, sb = base_ref[0:1, :], base_ref[1:2, :]
    co, so = off_ref[0], off_ref[1]
    cos_all = cb * co - sb * so
    sin_all = sb * co + cb * so
    cos, sin = cos_all[WINDOW:], sin_all[WINDOW:]
    kr = _rope(k_ref[...], cos, sin)
    kout_ref[...] = kr
    if cache:
        khr = kh_ref[...]
    else:
        khr = _rope(kh_ref[...], cos_all[:WINDOW], sin_all[:WINDOW])
    zpad = jnp.zeros((KPAD - KEYS, LANE), F32)
    kall = jnp.concatenate([khr, kr, zpad], axis=0)
    vall = jnp.concatenate([vh_ref[...], v_ref[...], zpad], axis=0)
    lo = lax.broadcasted_iota(jnp.int32, kall.shape, 1) < SWA_HEAD_DIM
    zero = jnp.zeros_like(kall)
    for src, a_ref, b_ref in ((kall, ka_ref, kb_ref), (vall, va_ref, vb_ref)):
        rolled = pltpu.roll(src, SWA_HEAD_DIM, 1)
        a_ref[0] = jnp.where(lo, src, zero).astype(BF16)
        b_ref[0] = jnp.where(lo, zero, rolled).astype(BF16)
        a_ref[1] = jnp.where(lo, rolled, zero).astype(BF16)
        b_ref[1] = jnp.where(lo, zero, src).astype(BF16)

    scale = SWA_HEAD_DIM ** -0.5
    for p in range(SWA_HEADS // 2):
        sl = slice(p * LANE, (p + 1) * LANE)
        qs_ref[:, sl] = (_rope(q_ref[:, sl], cos, sin) * scale).astype(BF16)

    jj = lax.broadcasted_iota(jnp.int32, (CHUNK, KPAD), 1)
    pairs_per_kv = SWA_HEADS // SWA_KV_HEADS // 2
    inst = [(c, g) for c in range(tq // CHUNK) for g in range(SWA_KV_HEADS)]

    def scores(c, g):
        rows = slice(c * CHUNK, (c + 1) * CHUNK)
        krows = slice(c * CHUNK, c * CHUNK + KPAD)
        kbd = jnp.concatenate([ka_ref[g, krows, :], kb_ref[g, krows, :]], axis=0)
        qst = jnp.concatenate(
            [qs_ref[rows, (g * pairs_per_kv + j) * LANE:(g * pairs_per_kv + j + 1) * LANE]
             for j in range(pairs_per_kv)], axis=0)
        return _dot_nt(qst, kbd)

    def probs(c, g, s):
        if cache:
            valid = jj < KEYS
        else:
            valid = (jj < KEYS) & ((i > 0) | (jj + c * CHUNK >= WINDOW))
        pblocks = []
        for j in range(pairs_per_kv):
            halves = []
            for half in range(2):
                sh = s[j * CHUNK:(j + 1) * CHUNK, half * KPAD:(half + 1) * KPAD]
                sh = jnp.where(valid, sh, -jnp.inf)
                sink = sink_ref[2 * (g * pairs_per_kv + j) + half]
                m = jnp.maximum(jnp.max(sh, axis=-1, keepdims=True), sink)
                e = jnp.exp(sh - m)
                den = jnp.sum(e, axis=-1, keepdims=True) + jnp.exp(sink - m)
                halves.append((e * (1.0 / den)).astype(BF16))
            pblocks.append(jnp.concatenate(halves, axis=1))
        return jnp.concatenate(pblocks, axis=0)

    def outputs(c, g, pmat):
        rows = slice(c * CHUNK, (c + 1) * CHUNK)
        krows = slice(c * CHUNK, c * CHUNK + KPAD)
        vbd = jnp.concatenate([va_ref[g, krows, :], vb_ref[g, krows, :]], axis=0)
        o = _dot(pmat, vbd)
        for j in range(pairs_per_kv):
            p = g * pairs_per_kv + j
            o_ref[rows, p * LANE:(p + 1) * LANE] = o[j * CHUNK:(j + 1) * CHUNK].astype(BF16)

    s_l, p_l = {}, {}
    for step in range(len(inst) + 2):
        if step < len(inst):
            s_l[step] = scores(*inst[step])
        if 0 <= step - 1 < len(inst):
            p_l[step - 1] = probs(*inst[step - 1], s_l.pop(step - 1))
        if 0 <= step - 2 < len(inst):
            outputs(*inst[step - 2], p_l.pop(step - 2))
        emit()
    emit(len(pending))


def _swa(proj, k_hist, v_hist, rope_base, rope_off, sinks, nb, t, tq, cache, host=None):
    nt = t // tq
    kv_rows = WINDOW + tq + (KPAD - KEYS)
    if cache:
        hist_k_spec = pl.BlockSpec((None, WINDOW, LANE), lambda b, i: (b, 0, 0))
        hist_v_spec = pl.BlockSpec((None, WINDOW, LANE), lambda b, i: (b, 0, 0))
        hk, hv = k_hist, v_hist
    else:
        r = tq // WINDOW

        def hidx(b, i):
            return jnp.maximum((b * nt + i) * r - 1, 0)

        hist_k_spec = pl.BlockSpec((WINDOW, LANE), lambda b, i: (hidx(b, i), SWA_KA))
        hist_v_spec = pl.BlockSpec((WINDOW, LANE), lambda b, i: (hidx(b, i), SWA_VA))
        hk, hv = proj, proj

    def rows(width):
        return pl.BlockSpec((tq, width), lambda b, i: (b * nt + i, 0))

    in_specs = [
        pl.BlockSpec(memory_space=pltpu.SMEM),
        rows(SWA_Q),
        pl.BlockSpec((tq, LANE), lambda b, i: (b * nt + i, SWA_KA)),
        pl.BlockSpec((tq, LANE), lambda b, i: (b * nt + i, SWA_VA)),
        hist_k_spec,
        hist_v_spec,
        pl.BlockSpec((None, 2, LANE), lambda b, i: (i, 0, 0)),
        pl.BlockSpec((2, WINDOW + tq, LANE), lambda b, i: (0, 0, 0)),
    ]
    args = [sinks, proj, proj, proj, hk, hv, rope_base, rope_off]
    out_specs = [rows(SWA_Q), rows(LANE)]
    out_shape = [jax.ShapeDtypeStruct((nb * t, SWA_Q), BF16), jax.ShapeDtypeStruct((nb * t, LANE), F32)]
    scratch = [
        pltpu.VMEM((tq, SWA_Q), BF16),
        pltpu.VMEM((SWA_KV_HEADS, kv_rows, LANE), BF16),
        pltpu.VMEM((SWA_KV_HEADS, kv_rows, LANE), BF16),
        pltpu.VMEM((SWA_KV_HEADS, kv_rows, LANE), BF16),
        pltpu.VMEM((SWA_KV_HEADS, kv_rows, LANE), BF16),
    ]
    if host is not None:
        x, gpre, wq, bq = host
        nq = MEM_Q + N_BRANCH * D_MODEL
        in_specs += [rows(D_MODEL), pl.BlockSpec((1, D_MODEL), lambda b, i: (0, 0)),
                     _resident((D_MODEL, nq)), _resident((1, nq))]
        args += [x, gpre, wq, bq]
        out_specs += [rows(MEM_Q), rows(N_BRANCH * D_MODEL)]
        out_shape += [jax.ShapeDtypeStruct((nb * t, MEM_Q), BF16),
                      jax.ShapeDtypeStruct((nb * t, N_BRANCH * D_MODEL), F32)]
        scratch += [pltpu.VMEM((tq, D_MODEL), BF16)]
    return pl.pallas_call(
        functools.partial(_swa_kernel, tq=tq, cache=cache, host_proj=host is not None),
        grid=(nb, nt),
        in_specs=in_specs,
        out_specs=out_specs,
        out_shape=out_shape,
        scratch_shapes=scratch,
        compiler_params=pltpu.CompilerParams(dimension_semantics=("parallel", "arbitrary"),
                                             vmem_limit_bytes=GDN_VMEM_LIMIT if host is not None else VMEM_LIMIT),
        name="swa",
    )(*args)


INV_BASE = 16


def _inverse_minus_identity(pms, row, col, chunk, between=lambda: None):
    base = (row // INV_BASE) == (col // INV_BASE)
    diag = row == col
    ms = [jnp.where(base, pm, 0.0) for pm in pms]
    ts = [jnp.where(diag, 1.0, m) for m in ms]
    for _ in range(INV_BASE.bit_length() - 2):
        mbs = [m.astype(BF16) for m in ms]
        ms = [_dot(mb, mb) for mb in mbs]
        between()
        ts = [t + _dot(m.astype(BF16), t.astype(BF16)) for m, t in zip(ms, ts)]
    bs = INV_BASE
    while bs < chunk:
        inner = (row // bs) == (col // bs)
        outer = (row // (2 * bs)) == (col // (2 * bs))
        sib = outer & jnp.logical_not(inner)
        xs = [_dot(jnp.where(sib, pm, 0.0).astype(BF16), t.astype(BF16)) for pm, t in zip(pms, ts)]
        ts = [t + _dot(t.astype(BF16), x.astype(BF16)) for t, x in zip(ts, xs)]
        between()
        bs *= 2
    rs = [jnp.where(diag, 0.0, t) for t in ts]
    return rs


def _gdn_kernel(x0_ref, xn_ref, gpre_ref, w_ref, b_ref, hist_ref, cw_ref, alog_ref, dt_ref, ng_ref, s0_ref,
                o_ref, sout_ref, swa_ref, clast_ref, proj_scr, h_scr, s_scr, carry_scr, ext_scr, conv_scr,
                *, tt, seg, chunk, carry):
    i = pl.program_id(0)
    nseg = tt // seg
    nblk = LANE // chunk
    ngroups = tt // LANE
    hist_rows = CONV_W - 1
    cur = proj_scr.at[i % 2]
    nxt = proj_scr.at[1 - i % 2]

    nchunks = pl.cdiv(N_PROJ, PROJ_CHUNK)

    def project(dst, c):
        cols = slice(c * PROJ_CHUNK, min((c + 1) * PROJ_CHUNK, N_PROJ))
        dst[8:8 + tt, cols] = _dot(h_scr[...], w_ref[:, cols]) + b_ref[:, cols]

    @pl.when(i == 0)
    def _():
        h_scr[...] = _rms(x0_ref[...], gpre_ref[...]).astype(BF16)
        for c in range(nchunks):
            project(proj_scr.at[0], c)

    h_scr[...] = _rms(xn_ref[...], gpre_ref[...]).astype(BF16)
    first = COL_QB * LANE // PROJ_CHUNK
    pending = list(range(first, nchunks - 1)) + list(range(first)) + [nchunks - 1]

    def emit(n=1):
        for _ in range(n):
            if pending:
                project(nxt, pending.pop(0))

    def init():
        for s in range(nseg):
            s_scr[s] = s0_ref[s]
            carry_scr[s] = jnp.zeros((8, 3 * GDN_QK), F32)
            carry_scr[s, 8 - hist_rows:8, :] = hist_ref[s]

    if carry:
        pl.when(i == 0)(init)
    else:
        init()

    swa_ref[...] = cur[8:8 + tt, COL_QA * LANE:(COL_VA + 1) * LANE]

    qkv = slice(COL_QB * LANE, COL_ZB * LANE)
    base = 8 - hist_rows
    for s in range(nseg):
        if carry:
            cur[0:8, qkv] = carry_scr[s]
            src, col0 = cur, COL_QB * LANE
            last = cur[tt:tt + 8, qkv]
            carry_scr[s] = last
        else:
            ext_scr[0:8, :] = carry_scr[s]
            ext_scr[8:8 + seg, :] = cur[8 + s * seg:8 + (s + 1) * seg, qkv]
            src, col0 = ext_scr, 0
            last = ext_scr[seg:seg + 8, :]
        clast_ref[s] = last
        for cb in range(3 * GDN_HEADS):
            lanes = slice(cb * LANE, (cb + 1) * LANE)
            slanes = slice(col0 + cb * LANE, col0 + (cb + 1) * LANE)
            y = src[base:base + seg, slanes] * cw_ref[0:1, lanes]
            for tap in range(1, CONV_W):
                y = y + src[base + tap:base + tap + seg, slanes] * cw_ref[tap:tap + 1, lanes]
            conv_scr[s * seg:(s + 1) * seg, lanes] = y * _sigmoid(y)
            if s == 0 and cb % 2 == 0:
                emit()

    row = lax.broadcasted_iota(jnp.int32, (LANE, LANE), 0)
    col = lax.broadcasted_iota(jnp.int32, (LANE, LANE), 1)
    same = (row // chunk) == (col // chunk)
    incl = same & (row >= col)
    strict = same & (row > col)
    lmask = jnp.where(incl, 1.0, 0.0).astype(BF16)
    row2 = jnp.concatenate([row, row], axis=0)

    inst = [(gi, hh) for gi in range(ngroups) for hh in range(GDN_HEADS)]
    heads = range(GDN_HEADS)

    gates = []
    for gi in range(ngroups):
        ab = cur[8 + gi * LANE:8 + (gi + 1) * LANE, COL_AB * LANE:(COL_AB + 1) * LANE]
        xa = ab + dt_ref[...]
        softplus = jnp.maximum(xa, 0.0) + jnp.log1p(jnp.exp(-jnp.abs(xa)))
        la = -jnp.exp(alog_ref[...]) * softplus
        beta = _sigmoid(ab)
        g_all = sum(_dot(lmask, piece) for piece in _split3(la))
        gates.append((g_all, g_all.T, beta))

    qn_l, kn_l, v_l, gb_l, bb_l, decay_l = [], [], [], [], [], []
    for gi, hh in inst:
        rows = slice(gi * LANE, (gi + 1) * LANE)
        q = conv_scr[rows, hh * LANE:(hh + 1) * LANE]
        k = conv_scr[rows, GDN_QK + hh * LANE:GDN_QK + (hh + 1) * LANE]
        v_l.append(conv_scr[rows, 2 * GDN_QK + hh * LANE:2 * GDN_QK + (hh + 1) * LANE])
        qn_l.append(q * (lax.rsqrt(jnp.sum(q * q, axis=-1, keepdims=True) + EPS) * (GDN_DK ** -0.5)))
        kn_l.append(k * lax.rsqrt(jnp.sum(k * k, axis=-1, keepdims=True) + EPS))
        g_all, g_all_t, beta = gates[gi]
        gb = jnp.broadcast_to(g_all[:, hh:hh + 1], (LANE, LANE))
        grow = jnp.broadcast_to(g_all_t[hh:hh + 1, :], (LANE, LANE))
        gb_l.append(gb)
        bb_l.append(jnp.broadcast_to(beta[:, GDN_HEADS + hh:GDN_HEADS + hh + 1], (LANE, LANE)))
        decay_l.append(jnp.exp(jnp.where(incl, gb - grow, -jnp.inf)))

    emit()
    knb_l = [kn.astype(BF16) for kn in kn_l]
    cat_l = [_dot_nt(jnp.concatenate([qn.astype(BF16), knb], axis=0), knb)
             for qn, knb in zip(qn_l, knb_l)]
    emit()
    qk_l = [cat[:LANE] * decay for cat, decay in zip(cat_l, decay_l)]
    pm_l = [jnp.where(strict, -(bb * cat[LANE:] * decay), 0.0)
            for bb, cat, decay in zip(bb_l, cat_l, decay_l)]
    r_l = _inverse_minus_identity(pm_l, row, col, chunk)
    emit()
    eg_l = [jnp.exp(gb) for gb in gb_l]
    rhs_l = [jnp.concatenate([v * bb, kn * (bb * eg)], axis=1)
             for v, bb, kn, eg in zip(v_l, bb_l, kn_l, eg_l)]
    sol_l = [rhs + _dot(r.astype(BF16), rhs.astype(BF16)) for r, rhs in zip(r_l, rhs_l)]
    wq_l = [jnp.concatenate([sol[:, GDN_DV:], qn * eg], axis=0).astype(BF16)
            for sol, qn, eg in zip(sol_l, qn_l, eg_l)]
    kd_l = []
    for kn, gb in zip(kn_l, gb_l):
        glast = gb[chunk - 1:chunk, :]
        for blk in range(1, nblk):
            glast = jnp.where(row >= blk * chunk, gb[(blk + 1) * chunk - 1:(blk + 1) * chunk, :], glast)
        kd_l.append(kn * jnp.exp(glast - gb))

    emit()
    nstate = 1 if carry else nblk
    state = [[s_scr[sidx, hh] for hh in heads] for sidx in range(nstate)]
    for gi in range(ngroups):
        ix = [gi * GDN_HEADS + hh for hh in heads]
        ws_l = []
        for hh in heads:
            ws = None
            for blk in range(nblk):
                part = _dot(wq_l[ix[hh]], state[0 if carry else blk][hh].astype(BF16))
                ws = part if ws is None else jnp.where(row2 >= blk * chunk, part, ws)
            ws_l.append(ws)
        emit()
        vnb_l = [(sol_l[ix[hh]][:, :GDN_DV] - ws_l[hh][:LANE]).astype(BF16) for hh in heads]
        o_l = [ws_l[hh][LANE:] + _dot(qk_l[ix[hh]].astype(BF16), vnb_l[hh]) for hh in heads]
        emit()
        for blk in range(nblk):
            sidx = 0 if carry else blk
            new = []
            for hh in heads:
                kd = kd_l[ix[hh]]
                gb = gb_l[ix[hh]]
                kdb = kd if nblk == 1 else jnp.where((row // chunk) == blk, kd, 0.0)
                gtot = jnp.exp(gb[(blk + 1) * chunk - 1:(blk + 1) * chunk, :])
                new.append(state[sidx][hh] * gtot + _dot_tn(kdb.astype(BF16), vnb_l[hh]))
            state[sidx] = new
        for hh in heads:
            rows = slice(gi * LANE, (gi + 1) * LANE)
            lanes = slice(hh * LANE, (hh + 1) * LANE)
            z = cur[8 + gi * LANE:8 + (gi + 1) * LANE, COL_ZB * LANE + hh * LANE:COL_ZB * LANE + (hh + 1) * LANE]
            o_ref[rows, lanes] = (_rms(o_l[hh], ng_ref[...]) * (z * _sigmoid(z))).astype(BF16)
    emit(len(pending))
    for sidx in range(nstate):
        for hh in heads:
            s_scr[sidx, hh] = state[sidx][hh]

    def fin():
        for s in range(nseg):
            sout_ref[s] = s_scr[s]

    if carry:
        pl.when(i == pl.num_programs(0) - 1)(fin)
    else:
        fin()


def _gdn(x, gpre, w_bf16, b, hist, s0, conv_w, alog_row, dt_row, norm_g, tt, seg, chunk, carry):
    t_rows = x.shape[0]
    nt = t_rows // tt
    nseg = tt // seg
    nseq = s0.shape[0]
    if carry:
        seq_idx = lambda i: 0
    else:
        seq_idx = lambda i: i
    swa_cols = (COL_VA + 1 - COL_QA) * LANE
    row_spec = pl.BlockSpec((1, LANE), lambda i: (0, 0))
    sspec = pl.BlockSpec((nseg, GDN_HEADS, GDN_DK, GDN_DV), lambda i: (seq_idx(i), 0, 0, 0))
    cspec = pl.BlockSpec((nseg, 8, 3 * GDN_QK), lambda i: (seq_idx(i), 0, 0))
    return pl.pallas_call(
        functools.partial(_gdn_kernel, tt=tt, seg=seg, chunk=chunk, carry=carry),
        grid=(nt,),
        in_specs=[
            pl.BlockSpec((tt, D_MODEL), lambda i: (0, 0)),
            pl.BlockSpec((tt, D_MODEL), lambda i: (jnp.minimum(i + 1, nt - 1), 0)),
            pl.BlockSpec((1, D_MODEL), lambda i: (0, 0)),
            _resident((D_MODEL, N_PROJ)),
            _resident((1, N_PROJ)),
            pl.BlockSpec((nseg, CONV_W - 1, 3 * GDN_QK), lambda i: (seq_idx(i), 0, 0)),
            pl.BlockSpec((CONV_W, 3 * GDN_QK), lambda i: (0, 0)),
            row_spec, row_spec, row_spec,
            sspec,
        ],
        out_specs=[
            pl.BlockSpec((tt, GDN_V), lambda i: (i, 0)),
            sspec,
            pl.BlockSpec((tt, swa_cols), lambda i: (i, 0)),
            cspec,
        ],
        out_shape=[
            jax.ShapeDtypeStruct((t_rows, GDN_V), BF16),
            jax.ShapeDtypeStruct((nseq, GDN_HEADS, GDN_DK, GDN_DV), F32),
            jax.ShapeDtypeStruct((t_rows, swa_cols), F32),
            jax.ShapeDtypeStruct((nseq, 8, 3 * GDN_QK), F32),
        ],
        scratch_shapes=[
            pltpu.VMEM((2, 8 + tt, N_PROJ), F32),
            pltpu.VMEM((tt, D_MODEL), BF16),
            pltpu.VMEM((nseg, GDN_HEADS, GDN_DK, GDN_DV), F32),
            pltpu.VMEM((nseg, 8, 3 * GDN_QK), F32),
            pltpu.VMEM((8 + seg, 3 * GDN_QK), F32),
            pltpu.VMEM((tt, 3 * GDN_QK), F32),
        ],
        compiler_params=pltpu.CompilerParams(dimension_semantics=("arbitrary",),
                                             vmem_limit_bytes=GDN_VMEM_LIMIT),
        name="gdn",
    )(x, x, gpre, w_bf16, b, hist, conv_w, alog_row, dt_row, norm_g, s0)


def _mix_kernel(x_ref, oa_ref, ob_ref, mk_ref, mv_ref, *rest, hosted):
    if hosted:
        qc_ref, gl_ref, wb_ref, wo_ref, gp_ref, o_ref = rest
    else:
        gpre_ref, wq_ref, bq_ref, wb_ref, wo_ref, gp_ref, o_ref = rest
    x = x_ref[...]
    tm = x.shape[0]
    nmem = mk_ref.shape[0]
    rpm = tm // nmem
    if hosted:
        qc = qc_ref[...]
    else:
        h = _rms(x, gpre_ref[...]).astype(BF16)
        qc = ((_dot(h, wq_ref[:, :MEM_Q]) + bq_ref[:, :MEM_Q]) * (MEM_HEAD_DIM ** -0.5)).astype(BF16)
    blocks = []
    for b in range(nmem):
        heads = []
        for hh in range(MEM_HEADS):
            sl = slice(hh * MEM_HEAD_DIM, (hh + 1) * MEM_HEAD_DIM)
            s = _dot_nt(qc[b * rpm:(b + 1) * rpm, sl], mk_ref[b, :, sl])
            m = jnp.max(s, axis=-1, keepdims=True)
            e = jnp.exp(s - m)
            p = e * (1.0 / jnp.sum(e, axis=-1, keepdims=True))
            heads.append(_dot(p.astype(BF16), mv_ref[b, :, sl]).astype(BF16))
        blocks.append(jnp.concatenate(heads, axis=1))
    oc = blocks[0] if nmem == 1 else jnp.concatenate(blocks, axis=0)
    merged = None
    for n, br_in in enumerate((oa_ref[...], ob_ref[...], oc)):
        if hosted:
            gl = gl_ref[:, n * D_MODEL:(n + 1) * D_MODEL]
        else:
            cols = slice(MEM_Q + n * D_MODEL, MEM_Q + (n + 1) * D_MODEL)
            gl = _dot(h, wq_ref[:, cols]) + bq_ref[:, cols]
        term = _sigmoid(gl) * _dot(br_in, wb_ref[n])
        merged = term if merged is None else merged + term
    y = _dot(merged.astype(BF16), wo_ref[...])
    o_ref[...] = x + _rms(y, gp_ref[...])


def _mix(x, oa, ob, mk, mv, proj, wb, wo, gp, tm, rows_per_mem):
    m = x.shape[0]
    tile = pl.BlockSpec((tm, D_MODEL), lambda i: (i, 0))
    vec = pl.BlockSpec((1, D_MODEL), lambda i: (0, 0))
    if mk.shape[0] == 1:
        mspec = _resident((1, N_MEM, MEM_Q))
    else:
        mspec = pl.BlockSpec((tm // rows_per_mem, N_MEM, MEM_Q), lambda i: (i, 0, 0))
    nq = MEM_Q + N_BRANCH * D_MODEL
    hosted = len(proj) == 2
    if hosted:
        pspecs = [tile, pl.BlockSpec((tm, N_BRANCH * D_MODEL), lambda i: (i, 0))]
    else:
        pspecs = [vec, _resident((D_MODEL, nq)), _resident((1, nq))]
    return pl.pallas_call(
        functools.partial(_mix_kernel, hosted=hosted),
        grid=(m // tm,),
        in_specs=[tile, tile, tile, mspec, mspec] + pspecs + [
            _resident((N_BRANCH, D_MODEL, D_MODEL)),
            _resident((D_MODEL, D_MODEL)),
            vec,
        ],
        out_specs=tile,
        out_shape=jax.ShapeDtypeStruct((m, D_MODEL), F32),
        compiler_params=pltpu.CompilerParams(dimension_semantics=("parallel",), vmem_limit_bytes=VMEM_LIMIT),
        name="mix",
    )(x, oa, ob, mk, mv, *proj, wb, wo, gp)


FF_CHUNK = 1024
FFN_TM = 512


def _ffn_kernel(xa_ref, xb_ref, g1_ref, wu_ref, wd_ref, g2_ref, oa_ref, ob_ref, *, na):
    def run(x_ref, o_ref):
        x = x_ref[...]
        hf = _rms(x, g1_ref[...]).astype(BF16)
        f = None
        for c in range(D_FF // FF_CHUNK):
            sl = slice(c * FF_CHUNK, (c + 1) * FF_CHUNK)
            up = jnp.maximum(_dot(hf, wu_ref[:, sl]), 0.0)
            part = _dot((up * up).astype(BF16), wd_ref[sl, :])
            f = part if f is None else f + part
        o_ref[...] = x + _rms(f, g2_ref[...])

    i = pl.program_id(0)
    pl.when(i < na)(functools.partial(run, xa_ref, oa_ref))
    pl.when(i >= na)(functools.partial(run, xb_ref, ob_ref))


def _ffn(xa, xb, g1, wu, wd, g2, tm):
    na, nb = xa.shape[0] // tm, xb.shape[0] // tm
    a_tile = pl.BlockSpec((tm, D_MODEL), lambda i: (jnp.minimum(i, na - 1), 0))
    b_tile = pl.BlockSpec((tm, D_MODEL), lambda i: (jnp.maximum(i - na, 0), 0))
    vec = pl.BlockSpec((1, D_MODEL), lambda i: (0, 0))
    return pl.pallas_call(
        functools.partial(_ffn_kernel, na=na),
        grid=(na + nb,),
        in_specs=[a_tile, b_tile, vec, _resident((D_MODEL, D_FF)), _resident((D_FF, D_MODEL)), vec],
        out_specs=[a_tile, b_tile],
        out_shape=[jax.ShapeDtypeStruct(xa.shape, F32), jax.ShapeDtypeStruct(xb.shape, F32)],
        compiler_params=pltpu.CompilerParams(dimension_semantics=("arbitrary",), vmem_limit_bytes=VMEM_LIMIT),
        name="ffn",
    )(xa, xb, g1, wu, wd, g2)


def _rope_tables(pos0, nt, tq):
    half = SWA_HEAD_DIM // 2
    inv = ROPE_THETA ** (-jnp.arange(half, dtype=F32) / half)
    inv = jnp.tile(inv, LANE // half)
    sgn = jnp.tile(jnp.concatenate([-jnp.ones((half,), F32), jnp.ones((half,), F32)]), LANE // SWA_HEAD_DIM)
    base = (pos0 + tq * jnp.arange(nt, dtype=jnp.int32)).astype(F32)[:, None] * inv[None, :]
    off = jnp.arange(-WINDOW, tq, dtype=jnp.int32).astype(F32)[:, None] * inv[None, :]
    rope_base = jnp.stack([jnp.cos(base), sgn * jnp.sin(base)], axis=1)
    rope_off = jnp.stack([jnp.cos(off), sgn * jnp.sin(off)], axis=0)
    return rope_base, rope_off


def _pad_row(v, offset=0):
    return jnp.zeros((1, LANE), F32).at[0, offset:offset + v.shape[0]].set(v)


def _layer(x, nb, t, pos0, k_hist, v_hist, s0, conv_hist, mk, mv, wts, cfg):
    ob, s_new, swa_in, conv_last = _gdn(x, wts['g_pre_mix'], wts['w_in'], wts['b_in'], conv_hist, s0,
                                        wts['conv_w'], wts['alog'], wts['dt'], wts['norm_g'],
                                        cfg['tt_gdn'], cfg['seg'], cfg['chunk'], cfg['carry'])
    rope_base, rope_off = _rope_tables(pos0, t // cfg['tq'], cfg['tq'])
    qproj = (wts['g_pre_mix'], wts['w_q'], wts['b_q'])
    if cfg['host_proj']:
        oa, k_rot, qc, gl = _swa(swa_in, k_hist, v_hist, rope_base, rope_off, wts['sinks'], nb, t, cfg['tq'],
                                 cfg['cache'], host=(x,) + qproj)
        qproj = (qc, gl)
    else:
        oa, k_rot = _swa(swa_in, k_hist, v_hist, rope_base, rope_off, wts['sinks'], nb, t, cfg['tq'],
                         cfg['cache'])
    x1 = _mix(x, oa, ob, mk, mv, qproj, wts['w_branch'], wts['w_out'], wts['g_post_mix'], cfg['tm_mix'], t)
    return x1, k_rot, s_new, swa_in, conv_last


def kernel(x_prompt, x_sample, mem_prompt, cache_swa_k, cache_swa_v, state_gdn, state_conv, cache_mem_k, cache_mem_v, g_pre_mix, w_in, b_in, swa_sinks, conv_w, gdn_a_log, gdn_dt_bias, gdn_norm_g, g_mem, w_mem_kv, w_branch, w_out, g_post_mix, g_pre_ffn, w_up, w_down, g_post_ffn):
    depth = w_in.shape[0]
    assert depth == 1
    l = 0
    bp, tp, _ = x_prompt.shape
    bs, ts, _ = x_sample.shape
    assert bp == 1

    wts = {
        'g_pre_mix': g_pre_mix[l][None, :],
        'w_in': w_in[l][:, :N_PROJ].astype(BF16),
        'b_in': b_in[l][None, :N_PROJ],
        'w_q': w_in[l][:, O_QC:O_END].astype(BF16),
        'b_q': b_in[l][None, O_QC:O_END],
        'sinks': swa_sinks[l],
        'conv_w': conv_w[l],
        'alog': _pad_row(gdn_a_log[l]),
        'dt': _pad_row(gdn_dt_bias[l]),
        'norm_g': gdn_norm_g[l][None, :],
        'w_branch': w_branch[l].astype(BF16),
        'w_out': w_out[l].astype(BF16),
        'g_post_mix': g_post_mix[l][None, :],
        'g_pre_ffn': g_pre_ffn[l][None, :],
        'w_up': w_up[l].astype(BF16),
        'w_down': w_down[l].astype(BF16),
        'g_post_ffn': g_post_ffn[l][None, :],
    }

    mkv, mkv_bf = _mem_kv(mem_prompt[0], g_mem[l][None, :], w_mem_kv[l].astype(BF16))
    cfg_p = dict(tq=512, cache=False, host_proj=True, tt_gdn=256, seg=256, chunk=128, carry=True,
                 tm_mix=512)
    zero_hist = jnp.zeros((1, CONV_W - 1, 3 * GDN_QK), F32)
    zero_s = jnp.zeros((1, GDN_HEADS, GDN_DK, GDN_DV), F32)
    x1p, kp_rot, sp_new, swa_p, clast_p = _layer(
        x_prompt.reshape(bp * tp, D_MODEL), bp, tp, 0, None, None,
        zero_s, zero_hist, mkv_bf[None, :, :MEM_Q], mkv_bf[None, :, MEM_Q:], wts, cfg_p)

    cfg_s = dict(tq=ts, cache=True, host_proj=False, tt_gdn=128, seg=ts, chunk=ts, carry=False,
                 tm_mix=4 * ts)
    x1s, ks_rot, ss_new, swa_s, clast_s = _layer(
        x_sample.reshape(bs * ts, D_MODEL), bs, ts, PAST_LEN,
        cache_swa_k[l].reshape(bs, WINDOW, SWA_KV), cache_swa_v[l].reshape(bs, WINDOW, SWA_KV),
        state_gdn[l], state_conv[l],
        cache_mem_k[l].astype(BF16).reshape(bs, N_MEM, MEM_Q),
        cache_mem_v[l].astype(BF16).reshape(bs, N_MEM, MEM_Q), wts, cfg_s)

    yp, ys = _ffn(x1p, x1s, wts['g_pre_ffn'], wts['w_up'], wts['w_down'], wts['g_post_ffn'], FFN_TM)

    kv_shape = (SWA_KV_HEADS, SWA_HEAD_DIM)
    va_cols = slice(SWA_VA * LANE, (SWA_VA + 1) * LANE)

    p_k = kp_rot[tp - WINDOW:].reshape((1, bp, WINDOW) + kv_shape)
    p_v = swa_p[tp - WINDOW:, va_cols].reshape((1, bp, WINDOW) + kv_shape)
    p_c = clast_p[:, 8 - (CONV_W - 1):][None]
    p_mk = mkv[:, :MEM_Q].reshape(1, bp, N_MEM, MEM_HEADS, MEM_HEAD_DIM)
    p_mv = mkv[:, MEM_Q:].reshape(1, bp, N_MEM, MEM_HEADS, MEM_HEAD_DIM)

    ks_new = ks_rot.reshape(bs, ts, SWA_KV)
    vs_new = swa_s[:, va_cols].reshape(bs, ts, SWA_KV)
    s_k = jnp.concatenate([cache_swa_k[l].reshape(bs, WINDOW, SWA_KV), ks_new], axis=1)[:, -WINDOW:]
    s_v = jnp.concatenate([cache_swa_v[l].reshape(bs, WINDOW, SWA_KV), vs_new], axis=1)[:, -WINDOW:]
    s_c = clast_s[:, 8 - (CONV_W - 1):]

    return (yp.reshape(bp, tp, D_MODEL), ys.reshape(bs, ts, D_MODEL),
            p_k, p_v, sp_new[None], p_c, p_mk, p_mv,
            s_k.reshape((1, bs, WINDOW) + kv_shape), s_v.reshape((1, bs, WINDOW) + kv_shape),
            ss_new[None], s_c[None])
```

```python
import functools

import jax
import jax.numpy as jnp
from jax import lax
from jax.experimental import pallas as pl
from jax.experimental.pallas import tpu as pltpu

F32 = jnp.float32
BF16 = jnp.bfloat16

D_MODEL = 1024
PAST_LEN = 2048
CHUNK = 64
N_MEM = 256
EPS = 1e-6
SWA_HEADS = 16
SWA_KV_HEADS = 2
SWA_HEAD_DIM = 64
WINDOW = 128
ROPE_THETA = 10000.0
GDN_HEADS = 8
GDN_DK = 128
GDN_DV = 128
CONV_W = 4
MEM_HEADS = 4
MEM_HEAD_DIM = 256
N_BRANCH = 3
D_FF = 4 * D_MODEL

LANE = 128
SWA_Q = SWA_HEADS * SWA_HEAD_DIM
SWA_KV = SWA_KV_HEADS * SWA_HEAD_DIM
GDN_QK = GDN_HEADS * GDN_DK
GDN_V = GDN_HEADS * GDN_DV
MEM_Q = MEM_HEADS * MEM_HEAD_DIM

O_KA = SWA_Q
O_QB = O_KA + 2 * SWA_KV
O_AB = O_QB + 2 * GDN_QK + 2 * GDN_V
O_QC = O_AB + 2 * GDN_HEADS
O_END = O_QC + MEM_Q + N_BRANCH * D_MODEL
COL_QA = 0
COL_KA = O_KA // LANE
COL_VA = COL_KA + 1
COL_QB = O_QB // LANE
COL_KB = COL_QB + GDN_HEADS
COL_VB = COL_KB + GDN_HEADS
COL_ZB = COL_VB + GDN_HEADS
COL_AB = O_AB // LANE
N_PROJ = (COL_AB + 1) * LANE

VMEM_LIMIT = 48 * 1024 * 1024
GDN_VMEM_LIMIT = 58 * 1024 * 1024
PROJ_CHUNK = 256


def _dot(a, b):
    return jnp.dot(a, b, preferred_element_type=F32)


def _dot_nt(a, b):
    return lax.dot_general(a, b, (((1,), (1,)), ((), ())), preferred_element_type=F32)


def _dot_tn(a, b):
    return lax.dot_general(a, b, (((0,), (0,)), ((), ())), preferred_element_type=F32)


def _rms(x, g):
    return x * lax.rsqrt(jnp.mean(x * x, axis=-1, keepdims=True) + EPS) * g


def _sigmoid(x):
    return 1.0 / (1.0 + jnp.exp(-x))


def _resident(shape):
    return pl.BlockSpec(shape, lambda *_: (0,) * len(shape), pipeline_mode=pl.Buffered(1))


def _split3(x):
    hi = x.astype(BF16)
    r = x - hi.astype(F32)
    mid = r.astype(BF16)
    lo = (r - mid.astype(F32)).astype(BF16)
    return hi, mid, lo


def _mem_kv_kernel(m_ref, g_ref, w_ref, o_ref, ob_ref):
    h = _rms(m_ref[...], g_ref[...]).astype(BF16)
    kv = _dot(h, w_ref[...])
    o_ref[...] = kv
    ob_ref[...] = kv.astype(BF16)


def _mem_kv(mem, g_mem, w_bf16):
    n = w_bf16.shape[1]
    tn = 1024
    return pl.pallas_call(
        _mem_kv_kernel,
        grid=(n // tn,),
        in_specs=[
            pl.BlockSpec((N_MEM, D_MODEL), lambda j: (0, 0)),
            pl.BlockSpec((1, D_MODEL), lambda j: (0, 0)),
            pl.BlockSpec((D_MODEL, tn), lambda j: (0, j)),
        ],
        out_specs=[pl.BlockSpec((N_MEM, tn), lambda j: (0, j))] * 2,
        out_shape=[jax.ShapeDtypeStruct((N_MEM, n), F32), jax.ShapeDtypeStruct((N_MEM, n), BF16)],
        compiler_params=pltpu.CompilerParams(dimension_semantics=("arbitrary",), vmem_limit_bytes=VMEM_LIMIT),
        name="mem_kv",
    )(mem, g_mem, w_bf16)


def _rope(x, cos, sin_signed):
    lane = lax.broadcasted_iota(jnp.int32, x.shape, 1)
    first = (lane % SWA_HEAD_DIM) < (SWA_HEAD_DIM // 2)
    rot = jnp.where(first, pltpu.roll(x, LANE - SWA_HEAD_DIM // 2, 1), pltpu.roll(x, SWA_HEAD_DIM // 2, 1))
    return x * cos + rot * sin_signed


SWA_KA = SWA_Q // LANE
SWA_VA = SWA_KA + 1
KEYS = WINDOW + CHUNK
KPAD = 256


def _swa_kernel(sink_ref, q_ref, k_ref, v_ref, kh_ref, vh_ref, base_ref, off_ref, *rest, tq, cache, host_proj):
    if host_proj:
        (x_ref, gpre_ref, wq_ref, bq_ref, o_ref, kout_ref, qc_ref, gl_ref,
         qs_ref, ka_ref, kb_ref, va_ref, vb_ref, h_scr) = rest
    else:
        o_ref, kout_ref, qs_ref, ka_ref, kb_ref, va_ref, vb_ref = rest
    i = pl.program_id(1)
    pending = []
    if host_proj:
        h_scr[...] = _rms(x_ref[...], gpre_ref[...]).astype(BF16)
        pending = list(range((MEM_Q + N_BRANCH * D_MODEL) // PROJ_CHUNK))

    def emit(n=1):
        for _ in range(n):
            if pending:
                c = pending.pop(0)
                cols = slice(c * PROJ_CHUNK, (c + 1) * PROJ_CHUNK)
                y = _dot(h_scr[...], wq_ref[:, cols]) + bq_ref[:, cols]
                if c < MEM_Q // PROJ_CHUNK:
                    qc_ref[:, cols] = (y * (MEM_HEAD_DIM ** -0.5)).astype(BF16)
                else:
                    gl_ref[:, c * PROJ_CHUNK - MEM_Q:(c + 1) * PROJ_CHUNK - MEM_Q] = y

    cb, sb = base_ref[0:1, :], base_ref[1:2, :]
    co, so = off_ref[0], off_ref[1]
    cos_all = cb * co - sb * so
    sin_all = sb * co + cb * so
    cos, sin = cos_all[WINDOW:], sin_all[WINDOW:]
    kr = _rope(k_ref[...], cos, sin)
    kout_ref[...] = kr
    if cache:
        khr = kh_ref[...]
    else:
        khr = _rope(kh_ref[...], cos_all[:WINDOW], sin_all[:WINDOW])
    zpad = jnp.zeros((KPAD - KEYS, LANE), F32)
    kall = jnp.concatenate([khr, kr, zpad], axis=0)
    vall = jnp.concatenate([vh_ref[...], v_ref[...], zpad], axis=0)
    lo = lax.broadcasted_iota(jnp.int32, kall.shape, 1) < SWA_HEAD_DIM
    zero = jnp.zeros_like(kall)
    for src, a_ref, b_ref in ((kall, ka_ref, kb_ref), (vall, va_ref, vb_ref)):
        rolled = pltpu.roll(src, SWA_HEAD_DIM, 1)
        a_ref[0] = jnp.where(lo, src, zero).astype(BF16)
        b_ref[0] = jnp.where(lo, zero, rolled).astype(BF16)
        a_ref[1] = jnp.where(lo, rolled, zero).astype(BF16)
        b_ref[1] = jnp.where(lo, zero, src).astype(BF16)

    scale = SWA_HEAD_DIM ** -0.5
    for p in range(SWA_HEADS // 2):
        sl = slice(p * LANE, (p + 1) * LANE)
        qs_ref[:, sl] = (_rope(q_ref[:, sl], cos, sin) * scale).astype(BF16)

    jj = lax.broadcasted_iota(jnp.int32, (CHUNK, KPAD), 1)
    pairs_per_kv = SWA_HEADS // SWA_KV_HEADS // 2
    inst = [(c, g) for c in range(tq // CHUNK) for g in range(SWA_KV_HEADS)]

    def scores(c, g):
        rows = slice(c * CHUNK, (c + 1) * CHUNK)
        krows = slice(c * CHUNK, c * CHUNK + KPAD)
        kbd = jnp.concatenate([ka_ref[g, krows, :], kb_ref[g, krows, :]], axis=0)
        qst = jnp.concatenate(
            [qs_ref[rows, (g * pairs_per_kv + j) * LANE:(g * pairs_per_kv + j + 1) * LANE]
             for j in range(pairs_per_kv)], axis=0)
        return _dot_nt(qst, kbd)

    def probs(c, g, s):
        if cache:
            valid = jj < KEYS
        else:
            valid = (jj < KEYS) & ((i > 0) | (jj + c * CHUNK >= WINDOW))
        pblocks = []
        for j in range(pairs_per_kv):
            halves = []
            for half in range(2):
                sh = s[j * CHUNK:(j + 1) * CHUNK, half * KPAD:(half + 1) * KPAD]
                sh = jnp.where(valid, sh, -jnp.inf)
                sink = sink_ref[2 * (g * pairs_per_kv + j) + half]
                m = jnp.maximum(jnp.max(sh, axis=-1, keepdims=True), sink)
                e = jnp.exp(sh - m)
                den = jnp.sum(e, axis=-1, keepdims=True) + jnp.exp(sink - m)
                halves.append((e * (1.0 / den)).astype(BF16))
            pblocks.append(jnp.concatenate(halves, axis=1))
        return jnp.concatenate(pblocks, axis=0)

    def outputs(c, g, pmat):
        rows = slice(c * CHUNK, (c + 1) * CHUNK)
        krows = slice(c * CHUNK, c * CHUNK + KPAD)
        vbd = jnp.concatenate([va_ref[g, krows, :], vb_ref[g, krows, :]], axis=0)
        o = _dot(pmat, vbd)
        for j in range(pairs_per_kv):
            p = g * pairs_per_kv + j
            o_ref[rows, p * LANE:(p + 1) * LANE] = o[j * CHUNK:(j + 1) * CHUNK].astype(BF16)

    s_l, p_l = {}, {}
    for step in range(len(inst) + 2):
        if step < len(inst):
            s_l[step] = scores(*inst[step])
        if 0 <= step - 1 < len(inst):
            p_l[step - 1] = probs(*inst[step - 1], s_l.pop(step - 1))
        if 0 <= step - 2 < len(inst):
            outputs(*inst[step - 2], p_l.pop(step - 2))
        emit()
    emit(len(pending))


def _swa(proj, k_hist, v_hist, rope_base, rope_off, sinks, nb, t, tq, cache, host=None):
    nt = t // tq
    kv_rows = WINDOW + tq + (KPAD - KEYS)
    if cache:
        hist_k_spec = pl.BlockSpec((None, WINDOW, LANE), lambda b, i: (b, 0, 0))
        hist_v_spec = pl.BlockSpec((None, WINDOW, LANE), lambda b, i: (b, 0, 0))
        hk, hv = k_hist, v_hist
    else:
        r = tq // WINDOW

        def hidx(b, i):
            return jnp.maximum((b * nt + i) * r - 1, 0)

        hist_k_spec = pl.BlockSpec((WINDOW, LANE), lambda b, i: (hidx(b, i), SWA_KA))
        hist_v_spec = pl.BlockSpec((WINDOW, LANE), lambda b, i: (hidx(b, i), SWA_VA))
        hk, hv = proj, proj

    def rows(width):
        return pl.BlockSpec((tq, width), lambda b, i: (b * nt + i, 0))

    in_specs = [
        pl.BlockSpec(memory_space=pltpu.SMEM),
        rows(SWA_Q),
        pl.BlockSpec((tq, LANE), lambda b, i: (b * nt + i, SWA_KA)),
        pl.BlockSpec((tq, LANE), lambda b, i: (b * nt + i, SWA_VA)),
        hist_k_spec,
        hist_v_spec,
        pl.BlockSpec((None, 2, LANE), lambda b, i: (i, 0, 0)),
        pl.BlockSpec((2, WINDOW + tq, LANE), lambda b, i: (0, 0, 0)),
    ]
    args = [sinks, proj, proj, proj, hk, hv, rope_base, rope_off]
    out_specs = [rows(SWA_Q), rows(LANE)]
    out_shape = [jax.ShapeDtypeStruct((nb * t, SWA_Q), BF16), jax.ShapeDtypeStruct((nb * t, LANE), F32)]
    scratch = [
        pltpu.VMEM((tq, SWA_Q), BF16),
        pltpu.VMEM((SWA_KV_HEADS, kv_rows, LANE), BF16),
        pltpu.VMEM((SWA_KV_HEADS, kv_rows, LANE), BF16),
        pltpu.VMEM((SWA_KV_HEADS, kv_rows, LANE), BF16),
        pltpu.VMEM((SWA_KV_HEADS, kv_rows, LANE), BF16),
    ]
    if host is not None:
        x, gpre, wq, bq = host
        nq = MEM_Q + N_BRANCH * D_MODEL
        in_specs += [rows(D_MODEL), pl.BlockSpec((1, D_MODEL), lambda b, i: (0, 0)),
                     _resident((D_MODEL, nq)), _resident((1, nq))]
        args += [x, gpre, wq, bq]
        out_specs += [rows(MEM_Q), rows(N_BRANCH * D_MODEL)]
        out_shape += [jax.ShapeDtypeStruct((nb * t, MEM_Q), BF16),
                      jax.ShapeDtypeStruct((nb * t, N_BRANCH * D_MODEL), F32)]
        scratch += [pltpu.VMEM((tq, D_MODEL), BF16)]
    return pl.pallas_call(
        functools.partial(_swa_kernel, tq=tq, cache=cache, host_proj=host is not None),
        grid=(nb, nt),
        in_specs=in_specs,
        out_specs=out_specs,
        out_shape=out_shape,
        scratch_shapes=scratch,
        compiler_params=pltpu.CompilerParams(dimension_semantics=("parallel", "arbitrary"),
                                             vmem_limit_bytes=GDN_VMEM_LIMIT if host is not None else VMEM_LIMIT),
        name="swa",
    )(*args)


INV_BASE = 16


def _inverse_minus_identity(pms, row, col, chunk, between=lambda: None):
    base = (row // INV_BASE) == (col // INV_BASE)
    diag = row == col
    ms = [jnp.where(base, pm, 0.0) for pm in pms]
    ts = [jnp.where(diag, 1.0, m) for m in ms]
    for _ in range(INV_BASE.bit_length() - 2):
        mbs = [m.astype(BF16) for m in ms]
        ms = [_dot(mb, mb) for mb in mbs]
        between()
        ts = [t + _dot(m.astype(BF16), t.astype(BF16)) for m, t in zip(ms, ts)]
    bs = INV_BASE
    while bs < chunk:
        inner = (row // bs) == (col // bs)
        outer = (row // (2 * bs)) == (col // (2 * bs))
        sib = outer & jnp.logical_not(inner)
        xs = [_dot(jnp.where(sib, pm, 0.0).astype(BF16), t.astype(BF16)) for pm, t in zip(pms, ts)]
        ts = [t + _dot(t.astype(BF16), x.astype(BF16)) for t, x in zip(ts, xs)]
        between()
        bs *= 2
    rs = [jnp.where(diag, 0.0, t) for t in ts]
    return rs


def _gdn_kernel(x0_ref, xn_ref, gpre_ref, w_ref, b_ref, hist_ref, cw_ref, alog_ref, dt_ref, ng_ref, s0_ref,
                o_ref, sout_ref, swa_ref, clast_ref, proj_scr, h_scr, s_scr, carry_scr, ext_scr, conv_scr,
                *, tt, seg, chunk, carry):
    i = pl.program_id(0)
    nseg = tt // seg
    nblk = LANE // chunk
    ngroups = tt // LANE
    hist_rows = CONV_W - 1
    cur = proj_scr.at[i % 2]
    nxt = proj_scr.at[1 - i % 2]

    nchunks = pl.cdiv(N_PROJ, PROJ_CHUNK)

    def project(dst, c):
        cols = slice(c * PROJ_CHUNK, min((c + 1) * PROJ_CHUNK, N_PROJ))
        dst[8:8 + tt, cols] = _dot(h_scr[...], w_ref[:, cols]) + b_ref[:, cols]

    @pl.when(i == 0)
    def _():
        h_scr[...] = _rms(x0_ref[...], gpre_ref[...]).astype(BF16)
        for c in range(nchunks):
            project(proj_scr.at[0], c)

    h_scr[...] = _rms(xn_ref[...], gpre_ref[...]).astype(BF16)
    first = COL_QB * LANE // PROJ_CHUNK
    pending = list(range(first, nchunks - 1)) + list(range(first)) + [nchunks - 1]

    def emit(n=1):
        for _ in range(n):
            if pending:
                project(nxt, pending.pop(0))

    def init():
        for s in range(nseg):
            s_scr[s] = s0_ref[s]
            carry_scr[s] = jnp.zeros((8, 3 * GDN_QK), F32)
            carry_scr[s, 8 - hist_rows:8, :] = hist_ref[s]

    if carry:
        pl.when(i == 0)(init)
    else:
        init()

    swa_ref[...] = cur[8:8 + tt, COL_QA * LANE:(COL_VA + 1) * LANE]

    qkv = slice(COL_QB * LANE, COL_ZB * LANE)
    base = 8 - hist_rows
    for s in range(nseg):
        if carry:
            cur[0:8, qkv] = carry_scr[s]
            src, col0 = cur, COL_QB * LANE
            last = cur[tt:tt + 8, qkv]
            carry_scr[s] = last
        else:
            ext_scr[0:8, :] = carry_scr[s]
            ext_scr[8:8 + seg, :] = cur[8 + s * seg:8 + (s + 1) * seg, qkv]
            src, col0 = ext_scr, 0
            last = ext_scr[seg:seg + 8, :]
        clast_ref[s] = last
        for cb in range(3 * GDN_HEADS):
            lanes = slice(cb * LANE, (cb + 1) * LANE)
            slanes = slice(col0 + cb * LANE, col0 + (cb + 1) * LANE)
            y = src[base:base + seg, slanes] * cw_ref[0:1, lanes]
            for tap in range(1, CONV_W):
                y = y + src[base + tap:base + tap + seg, slanes] * cw_ref[tap:tap + 1, lanes]
            conv_scr[s * seg:(s + 1) * seg, lanes] = y * _sigmoid(y)
            if s == 0 and cb % 2 == 0:
                emit()

    row = lax.broadcasted_iota(jnp.int32, (LANE, LANE), 0)
    col = lax.broadcasted_iota(jnp.int32, (LANE, LANE), 1)
    same = (row // chunk) == (col // chunk)
    incl = same & (row >= col)
    strict = same & (row > col)
    lmask = jnp.where(incl, 1.0, 0.0).astype(BF16)
    row2 = jnp.concatenate([row, row], axis=0)

    inst = [(gi, hh) for gi in range(ngroups) for hh in range(GDN_HEADS)]
    heads = range(GDN_HEADS)

    gates = []
    for gi in range(ngroups):
        ab = cur[8 + gi * LANE:8 + (gi + 1) * LANE, COL_AB * LANE:(COL_AB + 1) * LANE]
        xa = ab + dt_ref[...]
        softplus = jnp.maximum(xa, 0.0) + jnp.log1p(jnp.exp(-jnp.abs(xa)))
        la = -jnp.exp(alog_ref[...]) * softplus
        beta = _sigmoid(ab)
        g_all = sum(_dot(lmask, piece) for piece in _split3(la))
        gates.append((g_all, g_all.T, beta))

    qn_l, kn_l, v_l, gb_l, bb_l, decay_l = [], [], [], [], [], []
    for gi, hh in inst:
        rows = slice(gi * LANE, (gi + 1) * LANE)
        q = conv_scr[rows, hh * LANE:(hh + 1) * LANE]
        k = conv_scr[rows, GDN_QK + hh * LANE:GDN_QK + (hh + 1) * LANE]
        v_l.append(conv_scr[rows, 2 * GDN_QK + hh * LANE:2 * GDN_QK + (hh + 1) * LANE])
        qn_l.append(q * (lax.rsqrt(jnp.sum(q * q, axis=-1, keepdims=True) + EPS) * (GDN_DK ** -0.5)))
        kn_l.append(k * lax.rsqrt(jnp.sum(k * k, axis=-1, keepdims=True) + EPS))
        g_all, g_all_t, beta = gates[gi]
        gb = jnp.broadcast_to(g_all[:, hh:hh + 1], (LANE, LANE))
        grow = jnp.broadcast_to(g_all_t[hh:hh + 1, :], (LANE, LANE))
        gb_l.append(gb)
        bb_l.append(jnp.broadcast_to(beta[:, GDN_HEADS + hh:GDN_HEADS + hh + 1], (LANE, LANE)))
        decay_l.append(jnp.exp(jnp.where(incl, gb - grow, -jnp.inf)))

    emit()
    knb_l = [kn.astype(BF16) for kn in kn_l]
    cat_l = [_dot_nt(jnp.concatenate([qn.astype(BF16), knb], axis=0), knb)
             for qn, knb in zip(qn_l, knb_l)]
    emit()
    qk_l = [cat[:LANE] * decay for cat, decay in zip(cat_l, decay_l)]
    pm_l = [jnp.where(strict, -(bb * cat[LANE:] * decay), 0.0)
            for bb, cat, decay in zip(bb_l, cat_l, decay_l)]
    r_l = _inverse_minus_identity(pm_l, row, col, chunk)
    emit()
    eg_l = [jnp.exp(gb) for gb in gb_l]
    rhs_l = [jnp.concatenate([v * bb, kn * (bb * eg)], axis=1)
             for v, bb, kn, eg in zip(v_l, bb_l, kn_l, eg_l)]
    sol_l = [rhs + _dot(r.astype(BF16), rhs.astype(BF16)) for r, rhs in zip(r_l, rhs_l)]
    wq_l = [jnp.concatenate([sol[:, GDN_DV:], qn * eg], axis=0).astype(BF16)
            for sol, qn, eg in zip(sol_l, qn_l, eg_l)]
    kd_l = []
    for kn, gb in zip(kn_l, gb_l):
        glast = gb[chunk - 1:chunk, :]
        for blk in range(1, nblk):
            glast = jnp.where(row >= blk * chunk, gb[(blk + 1) * chunk - 1:(blk + 1) * chunk, :], glast)
        kd_l.append(kn * jnp.exp(glast - gb))

    emit()
    nstate = 1 if carry else nblk
    state = [[s_scr[sidx, hh] for hh in heads] for sidx in range(nstate)]
    for gi in range(ngroups):
        ix = [gi * GDN_HEADS + hh for hh in heads]
        ws_l = []
        for hh in heads:
            ws = None
            for blk in range(nblk):
                part = _dot(wq_l[ix[hh]], state[0 if carry else blk][hh].astype(BF16))
                ws = part if ws is None else jnp.where(row2 >= blk * chunk, part, ws)
            ws_l.append(ws)
        emit()
        vnb_l = [(sol_l[ix[hh]][:, :GDN_DV] - ws_l[hh][:LANE]).astype(BF16) for hh in heads]
        o_l = [ws_l[hh][LANE:] + _dot(qk_l[ix[hh]].astype(BF16), vnb_l[hh]) for hh in heads]
        emit()
        for blk in range(nblk):
            sidx = 0 if carry else blk
            new = []
            for hh in heads:
                kd = kd_l[ix[hh]]
                gb = gb_l[ix[hh]]
                kdb = kd if nblk == 1 else jnp.where((row // chunk) == blk, kd, 0.0)
                gtot = jnp.exp(gb[(blk + 1) * chunk - 1:(blk + 1) * chunk, :])
                new.append(state[sidx][hh] * gtot + _dot_tn(kdb.astype(BF16), vnb_l[hh]))
            state[sidx] = new
        for hh in heads:
            rows = slice(gi * LANE, (gi + 1) * LANE)
            lanes = slice(hh * LANE, (hh + 1) * LANE)
            z = cur[8 + gi * LANE:8 + (gi + 1) * LANE, COL_ZB * LANE + hh * LANE:COL_ZB * LANE + (hh + 1) * LANE]
            o_ref[rows, lanes] = (_rms(o_l[hh], ng_ref[...]) * (z * _sigmoid(z))).astype(BF16)
    emit(len(pending))
    for sidx in range(nstate):
        for hh in heads:
            s_scr[sidx, hh] = state[sidx][hh]

    def fin():
        for s in range(nseg):
            sout_ref[s] = s_scr[s]

    if carry:
        pl.when(i == pl.num_programs(0) - 1)(fin)
    else:
        fin()


def _gdn(x, gpre, w_bf16, b, hist, s0, conv_w, alog_row, dt_row, norm_g, tt, seg, chunk, carry):
    t_rows = x.shape[0]
    nt = t_rows // tt
    nseg = tt // seg
    nseq = s0.shape[0]
    if carry:
        seq_idx = lambda i: 0
    else:
        seq_idx = lambda i: i
    swa_cols = (COL_VA + 1 - COL_QA) * LANE
    row_spec = pl.BlockSpec((1, LANE), lambda i: (0, 0))
    sspec = pl.BlockSpec((nseg, GDN_HEADS, GDN_DK, GDN_DV), lambda i: (seq_idx(i), 0, 0, 0))
    cspec = pl.BlockSpec((nseg, 8, 3 * GDN_QK), lambda i: (seq_idx(i), 0, 0))
    return pl.pallas_call(
        functools.partial(_gdn_kernel, tt=tt, seg=seg, chunk=chunk, carry=carry),
        grid=(nt,),
        in_specs=[
            pl.BlockSpec((tt, D_MODEL), lambda i: (0, 0)),
            pl.BlockSpec((tt, D_MODEL), lambda i: (jnp.minimum(i + 1, nt - 1), 0)),
            pl.BlockSpec((1, D_MODEL), lambda i: (0, 0)),
            _resident((D_MODEL, N_PROJ)),
            _resident((1, N_PROJ)),
            pl.BlockSpec((nseg, CONV_W - 1, 3 * GDN_QK), lambda i: (seq_idx(i), 0, 0)),
            pl.BlockSpec((CONV_W, 3 * GDN_QK), lambda i: (0, 0)),
            row_spec, row_spec, row_spec,
            sspec,
        ],
        out_specs=[
            pl.BlockSpec((tt, GDN_V), lambda i: (i, 0)),
            sspec,
            pl.BlockSpec((tt, swa_cols), lambda i: (i, 0)),
            cspec,
        ],
        out_shape=[
            jax.ShapeDtypeStruct((t_rows, GDN_V), BF16),
            jax.ShapeDtypeStruct((nseq, GDN_HEADS, GDN_DK, GDN_DV), F32),
            jax.ShapeDtypeStruct((t_rows, swa_cols), F32),
            jax.ShapeDtypeStruct((nseq, 8, 3 * GDN_QK), F32),
        ],
        scratch_shapes=[
            pltpu.VMEM((2, 8 + tt, N_PROJ), F32),
            pltpu.VMEM((tt, D_MODEL), BF16),
            pltpu.VMEM((nseg, GDN_HEADS, GDN_DK, GDN_DV), F32),
            pltpu.VMEM((nseg, 8, 3 * GDN_QK), F32),
            pltpu.VMEM((8 + seg, 3 * GDN_QK), F32),
            pltpu.VMEM((tt, 3 * GDN_QK), F32),
        ],
        compiler_params=pltpu.CompilerParams(dimension_semantics=("arbitrary",),
                                             vmem_limit_bytes=GDN_VMEM_LIMIT),
        name="gdn",
    )(x, x, gpre, w_bf16, b, hist, conv_w, alog_row, dt_row, norm_g, s0)


def _mix_kernel(x_ref, oa_ref, ob_ref, mk_ref, mv_ref, *rest, hosted):
    if hosted:
        qc_ref, gl_ref, wb_ref, wo_ref, gp_ref, o_ref = rest
    else:
        gpre_ref, wq_ref, bq_ref, wb_ref, wo_ref, gp_ref, o_ref = rest
    x = x_ref[...]
    tm = x.shape[0]
    nmem = mk_ref.shape[0]
    rpm = tm // nmem
    if hosted:
        qc = qc_ref[...]
    else:
        h = _rms(x, gpre_ref[...]).astype(BF16)
        qc = ((_dot(h, wq_ref[:, :MEM_Q]) + bq_ref[:, :MEM_Q]) * (MEM_HEAD_DIM ** -0.5)).astype(BF16)
    def gate(n):
        if hosted:
            gl = gl_ref[:, n * D_MODEL:(n + 1) * D_MODEL]
        else:
            cols = slice(MEM_Q + n * D_MODEL, MEM_Q + (n + 1) * D_MODEL)
            gl = _dot(h, wq_ref[:, cols]) + bq_ref[:, cols]
        return _sigmoid(gl)

    pairs = [(b, hh) for b in range(nmem) for hh in range(MEM_HEADS)]
    hsl = [slice(hh * MEM_HEAD_DIM, (hh + 1) * MEM_HEAD_DIM) for hh in range(MEM_HEADS)]
    s_l = [_dot_nt(qc[b * rpm:(b + 1) * rpm, hsl[hh]], mk_ref[b, :, hsl[hh]]) for b, hh in pairs]
    merged = gate(0) * _dot(oa_ref[...], wb_ref[0]) + gate(1) * _dot(ob_ref[...], wb_ref[1])
    p_l = []
    for s in s_l:
        m = jnp.max(s, axis=-1, keepdims=True)
        e = jnp.exp(s - m)
        p_l.append((e * (1.0 / jnp.sum(e, axis=-1, keepdims=True))).astype(BF16))
    o_l = [_dot(p, mv_ref[b, :, hsl[hh]]).astype(BF16) for p, (b, hh) in zip(p_l, pairs)]
    blocks = [jnp.concatenate(o_l[b * MEM_HEADS:(b + 1) * MEM_HEADS], axis=1) for b in range(nmem)]
    oc = blocks[0] if nmem == 1 else jnp.concatenate(blocks, axis=0)
    merged = merged + gate(2) * _dot(oc, wb_ref[2])
    y = _dot(merged.astype(BF16), wo_ref[...])
    o_ref[...] = x + _rms(y, gp_ref[...])


def _mix(x, oa, ob, mk, mv, proj, wb, wo, gp, tm, rows_per_mem):
    m = x.shape[0]
    tile = pl.BlockSpec((tm, D_MODEL), lambda i: (i, 0))
    vec = pl.BlockSpec((1, D_MODEL), lambda i: (0, 0))
    if mk.shape[0] == 1:
        mspec = _resident((1, N_MEM, MEM_Q))
    else:
        mspec = pl.BlockSpec((tm // rows_per_mem, N_MEM, MEM_Q), lambda i: (i, 0, 0))
    nq = MEM_Q + N_BRANCH * D_MODEL
    hosted = len(proj) == 2
    if hosted:
        pspecs = [tile, pl.BlockSpec((tm, N_BRANCH * D_MODEL), lambda i: (i, 0))]
    else:
        pspecs = [vec, _resident((D_MODEL, nq)), _resident((1, nq))]
    return pl.pallas_call(
        functools.partial(_mix_kernel, hosted=hosted),
        grid=(m // tm,),
        in_specs=[tile, tile, tile, mspec, mspec] + pspecs + [
            _resident((N_BRANCH, D_MODEL, D_MODEL)),
            _resident((D_MODEL, D_MODEL)),
            vec,
        ],
        out_specs=tile,
        out_shape=jax.ShapeDtypeStruct((m, D_MODEL), F32),
        compiler_params=pltpu.CompilerParams(dimension_semantics=("parallel",), vmem_limit_bytes=VMEM_LIMIT),
        name="mix",
    )(x, oa, ob, mk, mv, *proj, wb, wo, gp)


FF_CHUNK = 1024
FFN_TM = 512


def _ffn_kernel(xa_ref, xb_ref, g1_ref, wu_ref, wd_ref, g2_ref, oa_ref, ob_ref, *, na):
    def run(x_ref, o_ref):
        x = x_ref[...]
        hf = _rms(x, g1_ref[...]).astype(BF16)
        f = None
        for c in range(D_FF // FF_CHUNK):
            sl = slice(c * FF_CHUNK, (c + 1) * FF_CHUNK)
            up = jnp.maximum(_dot(hf, wu_ref[:, sl]), 0.0)
            part = _dot((up * up).astype(BF16), wd_ref[sl, :])
            f = part if f is None else f + part
        o_ref[...] = x + _rms(f, g2_ref[...])

    i = pl.program_id(0)
    pl.when(i < na)(functools.partial(run, xa_ref, oa_ref))
    pl.when(i >= na)(functools.partial(run, xb_ref, ob_ref))


def _ffn(xa, xb, g1, wu, wd, g2, tm):
    na, nb = xa.shape[0] // tm, xb.shape[0] // tm
    a_tile = pl.BlockSpec((tm, D_MODEL), lambda i: (jnp.minimum(i, na - 1), 0))
    b_tile = pl.BlockSpec((tm, D_MODEL), lambda i: (jnp.maximum(i - na, 0), 0))
    vec = pl.BlockSpec((1, D_MODEL), lambda i: (0, 0))
    return pl.pallas_call(
        functools.partial(_ffn_kernel, na=na),
        grid=(na + nb,),
        in_specs=[a_tile, b_tile, vec, _resident((D_MODEL, D_FF)), _resident((D_FF, D_MODEL)), vec],
        out_specs=[a_tile, b_tile],
        out_shape=[jax.ShapeDtypeStruct(xa.shape, F32), jax.ShapeDtypeStruct(xb.shape, F32)],
        compiler_params=pltpu.CompilerParams(dimension_semantics=("arbitrary",), vmem_limit_bytes=VMEM_LIMIT),
        name="ffn",
    )(xa, xb, g1, wu, wd, g2)


def _rope_tables(pos0, nt, tq):
    half = SWA_HEAD_DIM // 2
    inv = ROPE_THETA ** (-jnp.arange(half, dtype=F32) / half)
    inv = jnp.tile(inv, LANE // half)
    sgn = jnp.tile(jnp.concatenate([-jnp.ones((half,), F32), jnp.ones((half,), F32)]), LANE // SWA_HEAD_DIM)
    base = (pos0 + tq * jnp.arange(nt, dtype=jnp.int32)).astype(F32)[:, None] * inv[None, :]
    off = jnp.arange(-WINDOW, tq, dtype=jnp.int32).astype(F32)[:, None] * inv[None, :]
    rope_base = jnp.stack([jnp.cos(base), sgn * jnp.sin(base)], axis=1)
    rope_off = jnp.stack([jnp.cos(off), sgn * jnp.sin(off)], axis=0)
    return rope_base, rope_off


def _pad_row(v, offset=0):
    return jnp.zeros((1, LANE), F32).at[0, offset:offset + v.shape[0]].set(v)


def _layer(x, nb, t, pos0, k_hist, v_hist, s0, conv_hist, mk, mv, wts, cfg):
    ob, s_new, swa_in, conv_last = _gdn(x, wts['g_pre_mix'], wts['w_in'], wts['b_in'], conv_hist, s0,
                                        wts['conv_w'], wts['alog'], wts['dt'], wts['norm_g'],
                                        cfg['tt_gdn'], cfg['seg'], cfg['chunk'], cfg['carry'])
    rope_base, rope_off = _rope_tables(pos0, t // cfg['tq'], cfg['tq'])
    qproj = (wts['g_pre_mix'], wts['w_q'], wts['b_q'])
    if cfg['host_proj']:
        oa, k_rot, qc, gl = _swa(swa_in, k_hist, v_hist, rope_base, rope_off, wts['sinks'], nb, t, cfg['tq'],
                                 cfg['cache'], host=(x,) + qproj)
        qproj = (qc, gl)
    else:
        oa, k_rot = _swa(swa_in, k_hist, v_hist, rope_base, rope_off, wts['sinks'], nb, t, cfg['tq'],
                         cfg['cache'])
    x1 = _mix(x, oa, ob, mk, mv, qproj, wts['w_branch'], wts['w_out'], wts['g_post_mix'], cfg['tm_mix'], t)
    return x1, k_rot, s_new, swa_in, conv_last


def kernel(x_prompt, x_sample, mem_prompt, cache_swa_k, cache_swa_v, state_gdn, state_conv, cache_mem_k, cache_mem_v, g_pre_mix, w_in, b_in, swa_sinks, conv_w, gdn_a_log, gdn_dt_bias, gdn_norm_g, g_mem, w_mem_kv, w_branch, w_out, g_post_mix, g_pre_ffn, w_up, w_down, g_post_ffn):
    depth = w_in.shape[0]
    assert depth == 1
    l = 0
    bp, tp, _ = x_prompt.shape
    bs, ts, _ = x_sample.shape
    assert bp == 1

    wts = {
        'g_pre_mix': g_pre_mix[l][None, :],
        'w_in': w_in[l][:, :N_PROJ].astype(BF16),
        'b_in': b_in[l][None, :N_PROJ],
        'w_q': w_in[l][:, O_QC:O_END].astype(BF16),
        'b_q': b_in[l][None, O_QC:O_END],
        'sinks': swa_sinks[l],
        'conv_w': conv_w[l],
        'alog': _pad_row(gdn_a_log[l]),
        'dt': _pad_row(gdn_dt_bias[l]),
        'norm_g': gdn_norm_g[l][None, :],
        'w_branch': w_branch[l].astype(BF16),
        'w_out': w_out[l].astype(BF16),
        'g_post_mix': g_post_mix[l][None, :],
        'g_pre_ffn': g_pre_ffn[l][None, :],
        'w_up': w_up[l].astype(BF16),
        'w_down': w_down[l].astype(BF16),
        'g_post_ffn': g_post_ffn[l][None, :],
    }

    mkv, mkv_bf = _mem_kv(mem_prompt[0], g_mem[l][None, :], w_mem_kv[l].astype(BF16))
    cfg_p = dict(tq=512, cache=False, host_proj=True, tt_gdn=256, seg=256, chunk=128, carry=True,
                 tm_mix=512)
    zero_hist = jnp.zeros((1, CONV_W - 1, 3 * GDN_QK), F32)
    zero_s = jnp.zeros((1, GDN_HEADS, GDN_DK, GDN_DV), F32)
    x1p, kp_rot, sp_new, swa_p, clast_p = _layer(
        x_prompt.reshape(bp * tp, D_MODEL), bp, tp, 0, None, None,
        zero_s, zero_hist, mkv_bf[None, :, :MEM_Q], mkv_bf[None, :, MEM_Q:], wts, cfg_p)

    cfg_s = dict(tq=ts, cache=True, host_proj=False, tt_gdn=128, seg=ts, chunk=ts, carry=False,
                 tm_mix=4 * ts)
    x1s, ks_rot, ss_new, swa_s, clast_s = _layer(
        x_sample.reshape(bs * ts, D_MODEL), bs, ts, PAST_LEN,
        cache_swa_k[l].reshape(bs, WINDOW, SWA_KV), cache_swa_v[l].reshape(bs, WINDOW, SWA_KV),
        state_gdn[l], state_conv[l],
        cache_mem_k[l].astype(BF16).reshape(bs, N_MEM, MEM_Q),
        cache_mem_v[l].astype(BF16).reshape(bs, N_MEM, MEM_Q), wts, cfg_s)

    yp, ys = _ffn(x1p, x1s, wts['g_pre_ffn'], wts['w_up'], wts['w_down'], wts['g_post_ffn'], FFN_TM)

    kv_shape = (SWA_KV_HEADS, SWA_HEAD_DIM)
    va_cols = slice(SWA_VA * LANE, (SWA_VA + 1) * LANE)

    p_k = kp_rot[tp - WINDOW:].reshape((1, bp, WINDOW) + kv_shape)
    p_v = swa_p[tp - WINDOW:, va_cols].reshape((1, bp, WINDOW) + kv_shape)
    p_c = clast_p[:, 8 - (CONV_W - 1):][None]
    p_mk = mkv[:, :MEM_Q].reshape(1, bp, N_MEM, MEM_HEADS, MEM_HEAD_DIM)
    p_mv = mkv[:, MEM_Q:].reshape(1, bp, N_MEM, MEM_HEADS, MEM_HEAD_DIM)

    ks_new = ks_rot.reshape(bs, ts, SWA_KV)
    vs_new = swa_s[:, va_cols].reshape(bs, ts, SWA_KV)
    s_k = jnp.concatenate([cache_swa_k[l].reshape(bs, WINDOW, SWA_KV), ks_new], axis=1)[:, -WINDOW:]
    s_v = jnp.concatenate([cache_swa_v[l].reshape(bs, WINDOW, SWA_KV), vs_new], axis=1)[:, -WINDOW:]
    s_c = clast_s[:, 8 - (CONV_W - 1):]

    return (yp.reshape(bp, tp, D_MODEL), ys.reshape(bs, ts, D_MODEL),
            p_k, p_v, sp_new[None], p_c, p_mk, p_mv,
            s_k.reshape((1, bs, WINDOW) + kv_shape), s_v.reshape((1, bs, WINDOW) + kv_shape),
            ss_new[None], s_c[None])
```

```python
import functools

import jax
import jax.numpy as jnp
from jax import lax
from jax.experimental import pallas as pl
from jax.experimental.pallas import tpu as pltpu

F32 = jnp.float32
BF16 = jnp.bfloat16

D_MODEL = 1024
PAST_LEN = 2048
CHUNK = 64
N_MEM = 256
EPS = 1e-6
SWA_HEADS = 16
SWA_KV_HEADS = 2
SWA_HEAD_DIM = 64
WINDOW = 128
ROPE_THETA = 10000.0
GDN_HEADS = 8
GDN_DK = 128
GDN_DV = 128
CONV_W = 4
MEM_HEADS = 4
MEM_HEAD_DIM = 256
N_BRANCH = 3
D_FF = 4 * D_MODEL

LANE = 128
SWA_Q = SWA_HEADS * SWA_HEAD_DIM
SWA_KV = SWA_KV_HEADS * SWA_HEAD_DIM
GDN_QK = GDN_HEADS * GDN_DK
GDN_V = GDN_HEADS * GDN_DV
MEM_Q = MEM_HEADS * MEM_HEAD_DIM

O_KA = SWA_Q
O_QB = O_KA + 2 * SWA_KV
O_AB = O_QB + 2 * GDN_QK + 2 * GDN_V
O_QC = O_AB + 2 * GDN_HEADS
O_END = O_QC + MEM_Q + N_BRANCH * D_MODEL
COL_QA = 0
COL_KA = O_KA // LANE
COL_VA = COL_KA + 1
COL_QB = O_QB // LANE
COL_KB = COL_QB + GDN_HEADS
COL_VB = COL_KB + GDN_HEADS
COL_ZB = COL_VB + GDN_HEADS
COL_AB = O_AB // LANE
N_PROJ = (COL_AB + 1) * LANE

VMEM_LIMIT = 48 * 1024 * 1024
GDN_VMEM_LIMIT = 58 * 1024 * 1024
PROJ_CHUNK = 256


def _dot(a, b):
    return jnp.dot(a, b, preferred_element_type=F32)


def _dot_nt(a, b):
    return lax.dot_general(a, b, (((1,), (1,)), ((), ())), preferred_element_type=F32)


def _dot_tn(a, b):
    return lax.dot_general(a, b, (((0,), (0,)), ((), ())), preferred_element_type=F32)


def _rms(x, g):
    return x * lax.rsqrt(jnp.mean(x * x, axis=-1, keepdims=True) + EPS) * g


def _sigmoid(x):
    return 1.0 / (1.0 + jnp.exp(-x))


def _resident(shape):
    return pl.BlockSpec(shape, lambda *_: (0,) * len(shape), pipeline_mode=pl.Buffered(1))


def _split3(x):
    hi = x.astype(BF16)
    r = x - hi.astype(F32)
    mid = r.astype(BF16)
    lo = (r - mid.astype(F32)).astype(BF16)
    return hi, mid, lo


def _mem_kv_kernel(m_ref, g_ref, w_ref, o_ref, ob_ref):
    h = _rms(m_ref[...], g_ref[...]).astype(BF16)
    kv = _dot(h, w_ref[...])
    o_ref[...] = kv
    ob_ref[...] = kv.astype(BF16)


def _mem_kv(mem, g_mem, w_bf16):
    n = w_bf16.shape[1]
    tn = 1024
    return pl.pallas_call(
        _mem_kv_kernel,
        grid=(n // tn,),
        in_specs=[
            pl.BlockSpec((N_MEM, D_MODEL), lambda j: (0, 0)),
            pl.BlockSpec((1, D_MODEL), lambda j: (0, 0)),
            pl.BlockSpec((D_MODEL, tn), lambda j: (0, j)),
        ],
        out_specs=[pl.BlockSpec((N_MEM, tn), lambda j: (0, j))] * 2,
        out_shape=[jax.ShapeDtypeStruct((N_MEM, n), F32), jax.ShapeDtypeStruct((N_MEM, n), BF16)],
        compiler_params=pltpu.CompilerParams(dimension_semantics=("arbitrary",), vmem_limit_bytes=VMEM_LIMIT),
        name="mem_kv",
    )(mem, g_mem, w_bf16)


def _rope(x, cos, sin_signed):
    lane = lax.broadcasted_iota(jnp.int32, x.shape, 1)
    first = (lane % SWA_HEAD_DIM) < (SWA_HEAD_DIM // 2)
    rot = jnp.where(first, pltpu.roll(x, LANE - SWA_HEAD_DIM // 2, 1), pltpu.roll(x, SWA_HEAD_DIM // 2, 1))
    return x * cos + rot * sin_signed


SWA_KA = SWA_Q // LANE
SWA_VA = SWA_KA + 1
KEYS = WINDOW + CHUNK
KPAD = 256


def _swa_kernel(sink_ref, q_ref, k_ref, v_ref, kh_ref, vh_ref, base_ref, off_ref, *rest, tq, cache, host_proj):
    if host_proj:
        (x_ref, gpre_ref, wq_ref, bq_ref, o_ref, kout_ref, qc_ref, gl_ref,
         qs_ref, ka_ref, kb_ref, va_ref, vb_ref, h_scr) = rest
    else:
        o_ref, kout_ref, qs_ref, ka_ref, kb_ref, va_ref, vb_ref = rest
    i = pl.program_id(1)
    pending = []
    if host_proj:
        h_scr[...] = _rms(x_ref[...], gpre_ref[...]).astype(BF16)
        pending = list(range((MEM_Q + N_BRANCH * D_MODEL) // PROJ_CHUNK))

    def emit(n=1):
        for _ in range(n):
            if pending:
                c = pending.pop(0)
                cols = slice(c * PROJ_CHUNK, (c + 1) * PROJ_CHUNK)
                y = _dot(h_scr[...], wq_ref[:, cols]) + bq_ref[:, cols]
                if c < MEM_Q // PROJ_CHUNK:
                    qc_ref[:, cols] = (y * (MEM_HEAD_DIM ** -0.5)).astype(BF16)
                else:
                    gl_ref[:, c * PROJ_CHUNK - MEM_Q:(c + 1) * PROJ_CHUNK - MEM_Q] = y

    cb, sb = base_ref[0:1, :], base_ref[1:2, :]
    co, so = off_ref[0], off_ref[1]
    cos_all = cb * co - sb * so
    sin_all = sb * co + cb * so
    cos, sin = cos_all[WINDOW:], sin_all[WINDOW:]
    kr = _rope(k_ref[...], cos, sin)
    kout_ref[...] = kr
    if cache:
        khr = kh_ref[...]
    else:
        khr = _rope(kh_ref[...], cos_all[:WINDOW], sin_all[:WINDOW])
    zpad = jnp.zeros((KPAD - KEYS, LANE), F32)
    kall = jnp.concatenate([khr, kr, zpad], axis=0)
    vall = jnp.concatenate([vh_ref[...], v_ref[...], zpad], axis=0)
    lo = lax.broadcasted_iota(jnp.int32, kall.shape, 1) < SWA_HEAD_DIM
    zero = jnp.zeros_like(kall)
    for src, a_ref, b_ref in ((kall, ka_ref, kb_ref), (vall, va_ref, vb_ref)):
        rolled = pltpu.roll(src, SWA_HEAD_DIM, 1)
        a_ref[0] = jnp.where(lo, src, zero).astype(BF16)
        b_ref[0] = jnp.where(lo, zero, rolled).astype(BF16)
        a_ref[1] = jnp.where(lo, rolled, zero).astype(BF16)
        b_ref[1] = jnp.where(lo, zero, src).astype(BF16)

    scale = SWA_HEAD_DIM ** -0.5
    for p in range(SWA_HEADS // 2):
        sl = slice(p * LANE, (p + 1) * LANE)
        qs_ref[:, sl] = (_rope(q_ref[:, sl], cos, sin) * scale).astype(BF16)

    jj = lax.broadcasted_iota(jnp.int32, (CHUNK, KPAD), 1)
    pairs_per_kv = SWA_HEADS // SWA_KV_HEADS // 2
    inst = [(c, g) for c in range(tq // CHUNK) for g in range(SWA_KV_HEADS)]

    def scores(c, g):
        rows = slice(c * CHUNK, (c + 1) * CHUNK)
        krows = slice(c * CHUNK, c * CHUNK + KPAD)
        kbd = jnp.concatenate([ka_ref[g, krows, :], kb_ref[g, krows, :]], axis=0)
        qst = jnp.concatenate(
            [qs_ref[rows, (g * pairs_per_kv + j) * LANE:(g * pairs_per_kv + j + 1) * LANE]
             for j in range(pairs_per_kv)], axis=0)
        return _dot_nt(qst, kbd)

    def probs(c, g, s):
        if cache:
            valid = jj < KEYS
        else:
            valid = (jj < KEYS) & ((i > 0) | (jj + c * CHUNK >= WINDOW))
        pblocks = []
        for j in range(pairs_per_kv):
            halves = []
            for half in range(2):
                sh = s[j * CHUNK:(j + 1) * CHUNK, half * KPAD:(half + 1) * KPAD]
                sh = jnp.where(valid, sh, -jnp.inf)
                sink = sink_ref[2 * (g * pairs_per_kv + j) + half]
                m = jnp.maximum(jnp.max(sh, axis=-1, keepdims=True), sink)
                e = jnp.exp(sh - m)
                den = jnp.sum(e, axis=-1, keepdims=True) + jnp.exp(sink - m)
                halves.append((e * (1.0 / den)).astype(BF16))
            pblocks.append(jnp.concatenate(halves, axis=1))
        return jnp.concatenate(pblocks, axis=0)

    def outputs(c, g, pmat):
        rows = slice(c * CHUNK, (c + 1) * CHUNK)
        krows = slice(c * CHUNK, c * CHUNK + KPAD)
        vbd = jnp.concatenate([va_ref[g, krows, :], vb_ref[g, krows, :]], axis=0)
        o = _dot(pmat, vbd)
        for j in range(pairs_per_kv):
            p = g * pairs_per_kv + j
            o_ref[rows, p * LANE:(p + 1) * LANE] = o[j * CHUNK:(j + 1) * CHUNK].astype(BF16)

    s_l, p_l = {}, {}
    for step in range(len(inst) + 2):
        if step < len(inst):
            s_l[step] = scores(*inst[step])
        if 0 <= step - 1 < len(inst):
            p_l[step - 1] = probs(*inst[step - 1], s_l.pop(step - 1))
        if 0 <= step - 2 < len(inst):
            outputs(*inst[step - 2], p_l.pop(step - 2))
        emit()
    emit(len(pending))


def _swa(proj, k_hist, v_hist, rope_base, rope_off, sinks, nb, t, tq, cache, host=None):
    nt = t // tq
    kv_rows = WINDOW + tq + (KPAD - KEYS)
    if cache:
        hist_k_spec = pl.BlockSpec((None, WINDOW, LANE), lambda b, i: (b, 0, 0))
        hist_v_spec = pl.BlockSpec((None, WINDOW, LANE), lambda b, i: (b, 0, 0))
        hk, hv = k_hist, v_hist
    else:
        r = tq // WINDOW

        def hidx(b, i):
            return jnp.maximum((b * nt + i) * r - 1, 0)

        hist_k_spec = pl.BlockSpec((WINDOW, LANE), lambda b, i: (hidx(b, i), SWA_KA))
        hist_v_spec = pl.BlockSpec((WINDOW, LANE), lambda b, i: (hidx(b, i), SWA_VA))
        hk, hv = proj, proj

    def rows(width):
        return pl.BlockSpec((tq, width), lambda b, i: (b * nt + i, 0))

    in_specs = [
        pl.BlockSpec(memory_space=pltpu.SMEM),
        rows(SWA_Q),
        pl.BlockSpec((tq, LANE), lambda b, i: (b * nt + i, SWA_KA)),
        pl.BlockSpec((tq, LANE), lambda b, i: (b * nt + i, SWA_VA)),
        hist_k_spec,
        hist_v_spec,
        pl.BlockSpec((None, 2, LANE), lambda b, i: (i, 0, 0)),
        pl.BlockSpec((2, WINDOW + tq, LANE), lambda b, i: (0, 0, 0)),
    ]
    args = [sinks, proj, proj, proj, hk, hv, rope_base, rope_off]
    out_specs = [rows(SWA_Q), rows(LANE)]
    out_shape = [jax.ShapeDtypeStruct((nb * t, SWA_Q), BF16), jax.ShapeDtypeStruct((nb * t, LANE), F32)]
    scratch = [
        pltpu.VMEM((tq, SWA_Q), BF16),
        pltpu.VMEM((SWA_KV_HEADS, kv_rows, LANE), BF16),
        pltpu.VMEM((SWA_KV_HEADS, kv_rows, LANE), BF16),
        pltpu.VMEM((SWA_KV_HEADS, kv_rows, LANE), BF16),
        pltpu.VMEM((SWA_KV_HEADS, kv_rows, LANE), BF16),
    ]
    if host is not None:
        x, gpre, wq, bq = host
        nq = MEM_Q + N_BRANCH * D_MODEL
        in_specs += [rows(D_MODEL), pl.BlockSpec((1, D_MODEL), lambda b, i: (0, 0)),
                     _resident((D_MODEL, nq)), _resident((1, nq))]
        args += [x, gpre, wq, bq]
        out_specs += [rows(MEM_Q), rows(N_BRANCH * D_MODEL)]
        out_shape += [jax.ShapeDtypeStruct((nb * t, MEM_Q), BF16),
                      jax.ShapeDtypeStruct((nb * t, N_BRANCH * D_MODEL), F32)]
        scratch += [pltpu.VMEM((tq, D_MODEL), BF16)]
    return pl.pallas_call(
        functools.partial(_swa_kernel, tq=tq, cache=cache, host_proj=host is not None),
        grid=(nb, nt),
        in_specs=in_specs,
        out_specs=out_specs,
        out_shape=out_shape,
        scratch_shapes=scratch,
        compiler_params=pltpu.CompilerParams(dimension_semantics=("parallel", "arbitrary"),
                                             vmem_limit_bytes=GDN_VMEM_LIMIT if host is not None else VMEM_LIMIT),
        name="swa",
    )(*args)


INV_BASE = 16


def _inverse_minus_identity(pms, row, col, chunk, between=lambda: None):
    base = (row // INV_BASE) == (col // INV_BASE)
    diag = row == col
    ms = [jnp.where(base, pm, 0.0) for pm in pms]
    ts = [jnp.where(diag, 1.0, m) for m in ms]
    for _ in range(INV_BASE.bit_length() - 2):
        mbs = [m.astype(BF16) for m in ms]
        ms = [_dot(mb, mb) for mb in mbs]
        between()
        ts = [t + _dot(m.astype(BF16), t.astype(BF16)) for m, t in zip(ms, ts)]
    bs = INV_BASE
    while bs < chunk:
        inner = (row // bs) == (col // bs)
        outer = (row // (2 * bs)) == (col // (2 * bs))
        sib = outer & jnp.logical_not(inner)
        xs = [_dot(jnp.where(sib, pm, 0.0).astype(BF16), t.astype(BF16)) for pm, t in zip(pms, ts)]
        ts = [t + _dot(t.astype(BF16), x.astype(BF16)) for t, x in zip(ts, xs)]
        between()
        bs *= 2
    rs = [jnp.where(diag, 0.0, t) for t in ts]
    return rs


def _gdn_kernel(x0_ref, xn_ref, gpre_ref, w_ref, b_ref, hist_ref, cw_ref, alog_ref, dt_ref, ng_ref, s0_ref,
                o_ref, sout_ref, swa_ref, clast_ref, proj_scr, h_scr, s_scr, carry_scr, ext_scr, conv_scr,
                *, tt, seg, chunk, carry):
    i = pl.program_id(0)
    nseg = tt // seg
    nblk = LANE // chunk
    ngroups = tt // LANE
    hist_rows = CONV_W - 1
    cur = proj_scr.at[i % 2]
    nxt = proj_scr.at[1 - i % 2]

    nchunks = pl.cdiv(N_PROJ, PROJ_CHUNK)

    def project(dst, c):
        cols = slice(c * PROJ_CHUNK, min((c + 1) * PROJ_CHUNK, N_PROJ))
        dst[8:8 + tt, cols] = _dot(h_scr[...], w_ref[:, cols]) + b_ref[:, cols]

    @pl.when(i == 0)
    def _():
        h_scr[...] = _rms(x0_ref[...], gpre_ref[...]).astype(BF16)
        for c in range(nchunks):
            project(proj_scr.at[0], c)

    h_scr[...] = _rms(xn_ref[...], gpre_ref[...]).astype(BF16)
    first = COL_QB * LANE // PROJ_CHUNK
    pending = list(range(first, nchunks - 1)) + list(range(first)) + [nchunks - 1]

    def emit(n=1):
        for _ in range(n):
            if pending:
                project(nxt, pending.pop(0))

    def init():
        for s in range(nseg):
            s_scr[s] = s0_ref[s]
            carry_scr[s] = jnp.zeros((8, 3 * GDN_QK), F32)
            carry_scr[s, 8 - hist_rows:8, :] = hist_ref[s]

    if carry:
        pl.when(i == 0)(init)
    else:
        init()

    swa_ref[...] = cur[8:8 + tt, COL_QA * LANE:(COL_VA + 1) * LANE]

    qkv = slice(COL_QB * LANE, COL_ZB * LANE)
    base = 8 - hist_rows
    for s in range(nseg):
        if carry:
            cur[0:8, qkv] = carry_scr[s]
            src, col0 = cur, COL_QB * LANE
            last = cur[tt:tt + 8, qkv]
            carry_scr[s] = last
        else:
            ext_scr[0:8, :] = carry_scr[s]
            ext_scr[8:8 + seg, :] = cur[8 + s * seg:8 + (s + 1) * seg, qkv]
            src, col0 = ext_scr, 0
            last = ext_scr[seg:seg + 8, :]
        clast_ref[s] = last
        for cb in range(3 * GDN_HEADS):
            lanes = slice(cb * LANE, (cb + 1) * LANE)
            slanes = slice(col0 + cb * LANE, col0 + (cb + 1) * LANE)
            y = src[base:base + seg, slanes] * cw_ref[0:1, lanes]
            for tap in range(1, CONV_W):
                y = y + src[base + tap:base + tap + seg, slanes] * cw_ref[tap:tap + 1, lanes]
            conv_scr[s * seg:(s + 1) * seg, lanes] = y * _sigmoid(y)
            if s == 0 and cb % 2 == 0:
                emit()

    row = lax.broadcasted_iota(jnp.int32, (LANE, LANE), 0)
    col = lax.broadcasted_iota(jnp.int32, (LANE, LANE), 1)
    same = (row // chunk) == (col // chunk)
    incl = same & (row >= col)
    strict = same & (row > col)
    lmask = jnp.where(incl, 1.0, 0.0).astype(BF16)
    row2 = jnp.concatenate([row, row], axis=0)

    inst = [(gi, hh) for gi in range(ngroups) for hh in range(GDN_HEADS)]
    heads = range(GDN_HEADS)

    gates = []
    for gi in range(ngroups):
        ab = cur[8 + gi * LANE:8 + (gi + 1) * LANE, COL_AB * LANE:(COL_AB + 1) * LANE]
        xa = ab + dt_ref[...]
        softplus = jnp.maximum(xa, 0.0) + jnp.log1p(jnp.exp(-jnp.abs(xa)))
        la = -jnp.exp(alog_ref[...]) * softplus
        beta = _sigmoid(ab)
        g_all = sum(_dot(lmask, piece) for piece in _split3(la))
        gates.append((g_all, g_all.T, beta))

    qn_l, kn_l, v_l, gb_l, bb_l, decay_l = [], [], [], [], [], []
    for gi, hh in inst:
        rows = slice(gi * LANE, (gi + 1) * LANE)
        q = conv_scr[rows, hh * LANE:(hh + 1) * LANE]
        k = conv_scr[rows, GDN_QK + hh * LANE:GDN_QK + (hh + 1) * LANE]
        v_l.append(conv_scr[rows, 2 * GDN_QK + hh * LANE:2 * GDN_QK + (hh + 1) * LANE])
        qn_l.append(q * (lax.rsqrt(jnp.sum(q * q, axis=-1, keepdims=True) + EPS) * (GDN_DK ** -0.5)))
        kn_l.append(k * lax.rsqrt(jnp.sum(k * k, axis=-1, keepdims=True) + EPS))
        g_all, g_all_t, beta = gates[gi]
        gb = jnp.broadcast_to(g_all[:, hh:hh + 1], (LANE, LANE))
        grow = jnp.broadcast_to(g_all_t[hh:hh + 1, :], (LANE, LANE))
        gb_l.append(gb)
        bb_l.append(jnp.broadcast_to(beta[:, GDN_HEADS + hh:GDN_HEADS + hh + 1], (LANE, LANE)))
        decay_l.append(jnp.exp(jnp.where(incl, gb - grow, -jnp.inf)))

    emit()
    knb_l = [kn.astype(BF16) for kn in kn_l]
    cat_l = [_dot_nt(jnp.concatenate([qn.astype(BF16), knb], axis=0), knb)
             for qn, knb in zip(qn_l, knb_l)]
    emit()
    qk_l = [cat[:LANE] * decay for cat, decay in zip(cat_l, decay_l)]
    pm_l = [jnp.where(strict, -(bb * cat[LANE:] * decay), 0.0)
            for bb, cat, decay in zip(bb_l, cat_l, decay_l)]
    r_l = _inverse_minus_identity(pm_l, row, col, chunk)
    emit()
    eg_l = [jnp.exp(gb) for gb in gb_l]
    rhs_l = [jnp.concatenate([v * bb, kn * (bb * eg)], axis=1)
             for v, bb, kn, eg in zip(v_l, bb_l, kn_l, eg_l)]
    sol_l = [rhs + _dot(r.astype(BF16), rhs.astype(BF16)) for r, rhs in zip(r_l, rhs_l)]
    wq_l = [jnp.concatenate([sol[:, GDN_DV:], qn * eg], axis=0).astype(BF16)
            for sol, qn, eg in zip(sol_l, qn_l, eg_l)]
    kd_l = []
    for kn, gb in zip(kn_l, gb_l):
        glast = gb[chunk - 1:chunk, :]
        for blk in range(1, nblk):
            glast = jnp.where(row >= blk * chunk, gb[(blk + 1) * chunk - 1:(blk + 1) * chunk, :], glast)
        kd_l.append(kn * jnp.exp(glast - gb))

    emit()
    nstate = 1 if carry else nblk
    state = [[s_scr[sidx, hh] for hh in heads] for sidx in range(nstate)]
    for gi in range(ngroups):
        ix = [gi * GDN_HEADS + hh for hh in heads]
        ws_l = []
        for hh in heads:
            ws = None
            for blk in range(nblk):
                part = _dot(wq_l[ix[hh]], state[0 if carry else blk][hh].astype(BF16))
                ws = part if ws is None else jnp.where(row2 >= blk * chunk, part, ws)
            ws_l.append(ws)
        emit()
        vnb_l = [(sol_l[ix[hh]][:, :GDN_DV] - ws_l[hh][:LANE]).astype(BF16) for hh in heads]
        o_l = [ws_l[hh][LANE:] + _dot(qk_l[ix[hh]].astype(BF16), vnb_l[hh]) for hh in heads]
        emit()
        for blk in range(nblk):
            sidx = 0 if carry else blk
            new = []
            for hh in heads:
                kd = kd_l[ix[hh]]
                gb = gb_l[ix[hh]]
                kdb = kd if nblk == 1 else jnp.where((row // chunk) == blk, kd, 0.0)
                gtot = jnp.exp(gb[(blk + 1) * chunk - 1:(blk + 1) * chunk, :])
                new.append(state[sidx][hh] * gtot + _dot_tn(kdb.astype(BF16), vnb_l[hh]))
            state[sidx] = new
        for hh in heads:
            rows = slice(gi * LANE, (gi + 1) * LANE)
            lanes = slice(hh * LANE, (hh + 1) * LANE)
            z = cur[8 + gi * LANE:8 + (gi + 1) * LANE, COL_ZB * LANE + hh * LANE:COL_ZB * LANE + (hh + 1) * LANE]
            o_ref[rows, lanes] = (_rms(o_l[hh], ng_ref[...]) * (z * _sigmoid(z))).astype(BF16)
    emit(len(pending))
    for sidx in range(nstate):
        for hh in heads:
            s_scr[sidx, hh] = state[sidx][hh]

    def fin():
        for s in range(nseg):
            sout_ref[s] = s_scr[s]

    if carry:
        pl.when(i == pl.num_programs(0) - 1)(fin)
    else:
        fin()


def _gdn(x, gpre, w_bf16, b, hist, s0, conv_w, alog_row, dt_row, norm_g, tt, seg, chunk, carry):
    t_rows = x.shape[0]
    nt = t_rows // tt
    nseg = tt // seg
    nseq = s0.shape[0]
    if carry:
        seq_idx = lambda i: 0
    else:
        seq_idx = lambda i: i
    swa_cols = (COL_VA + 1 - COL_QA) * LANE
    row_spec = pl.BlockSpec((1, LANE), lambda i: (0, 0))
    sspec = pl.BlockSpec((nseg, GDN_HEADS, GDN_DK, GDN_DV), lambda i: (seq_idx(i), 0, 0, 0))
    cspec = pl.BlockSpec((nseg, 8, 3 * GDN_QK), lambda i: (seq_idx(i), 0, 0))
    return pl.pallas_call(
        functools.partial(_gdn_kernel, tt=tt, seg=seg, chunk=chunk, carry=carry),
        grid=(nt,),
        in_specs=[
            pl.BlockSpec((tt, D_MODEL), lambda i: (0, 0)),
            pl.BlockSpec((tt, D_MODEL), lambda i: (jnp.minimum(i + 1, nt - 1), 0)),
            pl.BlockSpec((1, D_MODEL), lambda i: (0, 0)),
            _resident((D_MODEL, N_PROJ)),
            _resident((1, N_PROJ)),
            pl.BlockSpec((nseg, CONV_W - 1, 3 * GDN_QK), lambda i: (seq_idx(i), 0, 0)),
            pl.BlockSpec((CONV_W, 3 * GDN_QK), lambda i: (0, 0)),
            row_spec, row_spec, row_spec,
            sspec,
        ],
        out_specs=[
            pl.BlockSpec((tt, GDN_V), lambda i: (i, 0)),
            sspec,
            pl.BlockSpec((tt, swa_cols), lambda i: (i, 0)),
            cspec,
        ],
        out_shape=[
            jax.ShapeDtypeStruct((t_rows, GDN_V), BF16),
            jax.ShapeDtypeStruct((nseq, GDN_HEADS, GDN_DK, GDN_DV), F32),
            jax.ShapeDtypeStruct((t_rows, swa_cols), F32),
            jax.ShapeDtypeStruct((nseq, 8, 3 * GDN_QK), F32),
        ],
        scratch_shapes=[
            pltpu.VMEM((2, 8 + tt, N_PROJ), F32),
            pltpu.VMEM((tt, D_MODEL), BF16),
            pltpu.VMEM((nseg, GDN_HEADS, GDN_DK, GDN_DV), F32),
            pltpu.VMEM((nseg, 8, 3 * GDN_QK), F32),
            pltpu.VMEM((8 + seg, 3 * GDN_QK), F32),
            pltpu.VMEM((tt, 3 * GDN_QK), F32),
        ],
        compiler_params=pltpu.CompilerParams(dimension_semantics=("arbitrary",),
                                             vmem_limit_bytes=GDN_VMEM_LIMIT),
        name="gdn",
    )(x, x, gpre, w_bf16, b, hist, conv_w, alog_row, dt_row, norm_g, s0)


def _mix_kernel(x_ref, oa_ref, ob_ref, mk_ref, mv_ref, *rest, hosted):
    if hosted:
        qc_ref, gl_ref, wb_ref, wo_ref, gp_ref, o_ref = rest
    else:
        gpre_ref, wq_ref, bq_ref, wb_ref, wo_ref, gp_ref, o_ref = rest
    x = x_ref[...]
    tm = x.shape[0]
    nmem = mk_ref.shape[0]
    rpm = tm // nmem
    if hosted:
        qc = qc_ref[...]
    else:
        h = _rms(x, gpre_ref[...]).astype(BF16)
        qc = ((_dot(h, wq_ref[:, :MEM_Q]) + bq_ref[:, :MEM_Q]) * (MEM_HEAD_DIM ** -0.5)).astype(BF16)
    def gate(n):
        if hosted:
            gl = gl_ref[:, n * D_MODEL:(n + 1) * D_MODEL]
        else:
            cols = slice(MEM_Q + n * D_MODEL, MEM_Q + (n + 1) * D_MODEL)
            gl = _dot(h, wq_ref[:, cols]) + bq_ref[:, cols]
        return _sigmoid(gl)

    pairs = [(b, hh) for b in range(nmem) for hh in range(MEM_HEADS)]
    hsl = [slice(hh * MEM_HEAD_DIM, (hh + 1) * MEM_HEAD_DIM) for hh in range(MEM_HEADS)]
    s_l = [_dot_nt(qc[b * rpm:(b + 1) * rpm, hsl[hh]], mk_ref[b, :, hsl[hh]]) for b, hh in pairs]
    merged = gate(0) * _dot(oa_ref[...], wb_ref[0]) + gate(1) * _dot(ob_ref[...], wb_ref[1])
    p_l = []
    for s in s_l:
        m = jnp.max(s, axis=-1, keepdims=True)
        e = jnp.exp(s - m)
        p_l.append((e * (1.0 / jnp.sum(e, axis=-1, keepdims=True))).astype(BF16))
    o_l = [_dot(p, mv_ref[b, :, hsl[hh]]).astype(BF16) for p, (b, hh) in zip(p_l, pairs)]
    blocks = [jnp.concatenate(o_l[b * MEM_HEADS:(b + 1) * MEM_HEADS], axis=1) for b in range(nmem)]
    oc = blocks[0] if nmem == 1 else jnp.concatenate(blocks, axis=0)
    merged = merged + gate(2) * _dot(oc, wb_ref[2])
    y = _dot(merged.astype(BF16), wo_ref[...])
    o_ref[...] = x + _rms(y, gp_ref[...])


def _mix(x, oa, ob, mk, mv, proj, wb, wo, gp, tm, rows_per_mem):
    m = x.shape[0]
    tile = pl.BlockSpec((tm, D_MODEL), lambda i: (i, 0))
    vec = pl.BlockSpec((1, D_MODEL), lambda i: (0, 0))
    if mk.shape[0] == 1:
        mspec = _resident((1, N_MEM, MEM_Q))
    else:
        mspec = pl.BlockSpec((tm // rows_per_mem, N_MEM, MEM_Q), lambda i: (i, 0, 0))
    nq = MEM_Q + N_BRANCH * D_MODEL
    hosted = len(proj) == 2
    if hosted:
        pspecs = [tile, pl.BlockSpec((tm, N_BRANCH * D_MODEL), lambda i: (i, 0))]
    else:
        pspecs = [vec, _resident((D_MODEL, nq)), _resident((1, nq))]
    return pl.pallas_call(
        functools.partial(_mix_kernel, hosted=hosted),
        grid=(m // tm,),
        in_specs=[tile, tile, tile, mspec, mspec] + pspecs + [
            _resident((N_BRANCH, D_MODEL, D_MODEL)),
            _resident((D_MODEL, D_MODEL)),
            vec,
        ],
        out_specs=tile,
        out_shape=jax.ShapeDtypeStruct((m, D_MODEL), F32),
        compiler_params=pltpu.CompilerParams(dimension_semantics=("parallel",), vmem_limit_bytes=VMEM_LIMIT),
        name="mix",
    )(x, oa, ob, mk, mv, *proj, wb, wo, gp)


FF_CHUNK = 1024
FFN_TM = 512


def _ffn_kernel(xa_ref, xb_ref, g1_ref, wu_ref, wd_ref, g2_ref, oa_ref, ob_ref, *, na):
    def run(x_ref, o_ref):
        x = x_ref[...]
        hf = _rms(x, g1_ref[...]).astype(BF16)
        f = None
        for c in range(D_FF // FF_CHUNK):
            sl = slice(c * FF_CHUNK, (c + 1) * FF_CHUNK)
            up = jnp.maximum(_dot(hf, wu_ref[:, sl]), 0.0)
            part = _dot((up * up).astype(BF16), wd_ref[sl, :])
            f = part if f is None else f + part
        o_ref[...] = x + _rms(f, g2_ref[...])

    i = pl.program_id(0)
    pl.when(i < na)(functools.partial(run, xa_ref, oa_ref))
    pl.when(i >= na)(functools.partial(run, xb_ref, ob_ref))


def _ffn(xa, xb, g1, wu, wd, g2, tm):
    na, nb = xa.shape[0] // tm, xb.shape[0] // tm
    a_tile = pl.BlockSpec((tm, D_MODEL), lambda i: (jnp.minimum(i, na - 1), 0))
    b_tile = pl.BlockSpec((tm, D_MODEL), lambda i: (jnp.maximum(i - na, 0), 0))
    vec = pl.BlockSpec((1, D_MODEL), lambda i: (0, 0))
    return pl.pallas_call(
        functools.partial(_ffn_kernel, na=na),
        grid=(na + nb,),
        in_specs=[a_tile, b_tile, vec, _resident((D_MODEL, D_FF)), _resident((D_FF, D_MODEL)), vec],
        out_specs=[a_tile, b_tile],
        out_shape=[jax.ShapeDtypeStruct(xa.shape, F32), jax.ShapeDtypeStruct(xb.shape, F32)],
        compiler_params=pltpu.CompilerParams(dimension_semantics=("arbitrary",), vmem_limit_bytes=VMEM_LIMIT),
        name="ffn",
    )(xa, xb, g1, wu, wd, g2)


def _rope_tables(pos0, nt, tq):
    half = SWA_HEAD_DIM // 2
    inv = ROPE_THETA ** (-jnp.arange(half, dtype=F32) / half)
    inv = jnp.tile(inv, LANE // half)
    sgn = jnp.tile(jnp.concatenate([-jnp.ones((half,), F32), jnp.ones((half,), F32)]), LANE // SWA_HEAD_DIM)
    base = (pos0 + tq * jnp.arange(nt, dtype=jnp.int32)).astype(F32)[:, None] * inv[None, :]
    off = jnp.arange(-WINDOW, tq, dtype=jnp.int32).astype(F32)[:, None] * inv[None, :]
    rope_base = jnp.stack([jnp.cos(base), sgn * jnp.sin(base)], axis=1)
    rope_off = jnp.stack([jnp.cos(off), sgn * jnp.sin(off)], axis=0)
    return rope_base, rope_off


def _pad_row(v, offset=0):
    return jnp.zeros((1, LANE), F32).at[0, offset:offset + v.shape[0]].set(v)


def _layer(x, nb, t, pos0, k_hist, v_hist, s0, conv_hist, mk, mv, wts, cfg):
    ob, s_new, swa_in, conv_last = _gdn(x, wts['g_pre_mix'], wts['w_in'], wts['b_in'], conv_hist, s0,
                                        wts['conv_w'], wts['alog'], wts['dt'], wts['norm_g'],
                                        cfg['tt_gdn'], cfg['seg'], cfg['chunk'], cfg['carry'])
    rope_base, rope_off = _rope_tables(pos0, t // cfg['tq'], cfg['tq'])
    qproj = (wts['g_pre_mix'], wts['w_q'], wts['b_q'])
    if cfg['host_proj']:
        oa, k_rot, qc, gl = _swa(swa_in, k_hist, v_hist, rope_base, rope_off, wts['sinks'], nb, t, cfg['tq'],
                                 cfg['cache'], host=(x,) + qproj)
        qproj = (qc, gl)
    else:
        oa, k_rot = _swa(swa_in, k_hist, v_hist, rope_base, rope_off, wts['sinks'], nb, t, cfg['tq'],
                         cfg['cache'])
    x1 = _mix(x, oa, ob, mk, mv, qproj, wts['w_branch'], wts['w_out'], wts['g_post_mix'], cfg['tm_mix'], t)
    return x1, k_rot, s_new, swa_in, conv_last


def kernel(x_prompt, x_sample, mem_prompt, cache_swa_k, cache_swa_v, state_gdn, state_conv, cache_mem_k, cache_mem_v, g_pre_mix, w_in, b_in, swa_sinks, conv_w, gdn_a_log, gdn_dt_bias, gdn_norm_g, g_mem, w_mem_kv, w_branch, w_out, g_post_mix, g_pre_ffn, w_up, w_down, g_post_ffn):
    depth = w_in.shape[0]
    assert depth == 1
    l = 0
    bp, tp, _ = x_prompt.shape
    bs, ts, _ = x_sample.shape
    assert bp == 1

    w_in_bf = w_in[l].astype(BF16)
    wts = {
        'g_pre_mix': g_pre_mix[l][None, :],
        'w_in': w_in_bf,
        'b_in': b_in[l][None, :],
        'w_q': w_in_bf[:, O_QC:O_END],
        'b_q': b_in[l][None, O_QC:O_END],
        'sinks': swa_sinks[l],
        'conv_w': conv_w[l],
        'alog': _pad_row(gdn_a_log[l]),
        'dt': _pad_row(gdn_dt_bias[l]),
        'norm_g': gdn_norm_g[l][None, :],
        'w_branch': w_branch[l].astype(BF16),
        'w_out': w_out[l].astype(BF16),
        'g_post_mix': g_post_mix[l][None, :],
        'g_pre_ffn': g_pre_ffn[l][None, :],
        'w_up': w_up[l].astype(BF16),
        'w_down': w_down[l].astype(BF16),
        'g_post_ffn': g_post_ffn[l][None, :],
    }

    mkv, mkv_bf = _mem_kv(mem_prompt[0], g_mem[l][None, :], w_mem_kv[l].astype(BF16))
    cfg_p = dict(tq=512, cache=False, host_proj=True, tt_gdn=256, seg=256, chunk=128, carry=True,
                 tm_mix=512)
    zero_hist = jnp.zeros((1, CONV_W - 1, 3 * GDN_QK), F32)
    zero_s = jnp.zeros((1, GDN_HEADS, GDN_DK, GDN_DV), F32)
    x1p, kp_rot, sp_new, swa_p, clast_p = _layer(
        x_prompt.reshape(bp * tp, D_MODEL), bp, tp, 0, None, None,
        zero_s, zero_hist, mkv_bf[None, :, :MEM_Q], mkv_bf[None, :, MEM_Q:], wts, cfg_p)

    cfg_s = dict(tq=ts, cache=True, host_proj=False, tt_gdn=128, seg=ts, chunk=ts, carry=False,
                 tm_mix=4 * ts)
    x1s, ks_rot, ss_new, swa_s, clast_s = _layer(
        x_sample.reshape(bs * ts, D_MODEL), bs, ts, PAST_LEN,
        cache_swa_k[l].reshape(bs, WINDOW, SWA_KV), cache_swa_v[l].reshape(bs, WINDOW, SWA_KV),
        state_gdn[l], state_conv[l],
        cache_mem_k[l].astype(BF16).reshape(bs, N_MEM, MEM_Q),
        cache_mem_v[l].astype(BF16).reshape(bs, N_MEM, MEM_Q), wts, cfg_s)

    yp, ys = _ffn(x1p, x1s, wts['g_pre_ffn'], wts['w_up'], wts['w_down'], wts['g_post_ffn'], FFN_TM)

    kv_shape = (SWA_KV_HEADS, SWA_HEAD_DIM)
    va_cols = slice(SWA_VA * LANE, (SWA_VA + 1) * LANE)

    p_k = kp_rot[tp - WINDOW:].reshape((1, bp, WINDOW) + kv_shape)
    p_v = swa_p[tp - WINDOW:, va_cols].reshape((1, bp, WINDOW) + kv_shape)
    p_c = clast_p[:, 8 - (CONV_W - 1):][None]
    p_mk = mkv[:, :MEM_Q].reshape(1, bp, N_MEM, MEM_HEADS, MEM_HEAD_DIM)
    p_mv = mkv[:, MEM_Q:].reshape(1, bp, N_MEM, MEM_HEADS, MEM_HEAD_DIM)

    ks_new = ks_rot.reshape(bs, ts, SWA_KV)
    vs_new = swa_s[:, va_cols].reshape(bs, ts, SWA_KV)
    s_k = jnp.concatenate([cache_swa_k[l].reshape(bs, WINDOW, SWA_KV), ks_new], axis=1)[:, -WINDOW:]
    s_v = jnp.concatenate([cache_swa_v[l].reshape(bs, WINDOW, SWA_KV), vs_new], axis=1)[:, -WINDOW:]
    s_c = clast_s[:, 8 - (CONV_W - 1):]

    return (yp.reshape(bp, tp, D_MODEL), ys.reshape(bs, ts, D_MODEL),
            p_k, p_v, sp_new[None], p_c, p_mk, p_mv,
            s_k.reshape((1, bs, WINDOW) + kv_shape), s_v.reshape((1, bs, WINDOW) + kv_shape),
            ss_new[None], s_c[None])
```

```python
import functools

import jax
import jax.numpy as jnp
from jax import lax
from jax.experimental import pallas as pl
from jax.experimental.pallas import tpu as pltpu

F32 = jnp.float32
BF16 = jnp.bfloat16

D_MODEL = 1024
PAST_LEN = 2048
CHUNK = 64
N_MEM = 256
EPS = 1e-6
SWA_HEADS = 16
SWA_KV_HEADS = 2
SWA_HEAD_DIM = 64
WINDOW = 128
ROPE_THETA = 10000.0
GDN_HEADS = 8
GDN_DK = 128
GDN_DV = 128
CONV_W = 4
MEM_HEADS = 4
MEM_HEAD_DIM = 256
N_BRANCH = 3
D_FF = 4 * D_MODEL

LANE = 128
SWA_Q = SWA_HEADS * SWA_HEAD_DIM
SWA_KV = SWA_KV_HEADS * SWA_HEAD_DIM
GDN_QK = GDN_HEADS * GDN_DK
GDN_V = GDN_HEADS * GDN_DV
MEM_Q = MEM_HEADS * MEM_HEAD_DIM

O_KA = SWA_Q
O_QB = O_KA + 2 * SWA_KV
O_AB = O_QB + 2 * GDN_QK + 2 * GDN_V
O_QC = O_AB + 2 * GDN_HEADS
O_END = O_QC + MEM_Q + N_BRANCH * D_MODEL
COL_QA = 0
COL_KA = O_KA // LANE
COL_VA = COL_KA + 1
COL_QB = O_QB // LANE
COL_KB = COL_QB + GDN_HEADS
COL_VB = COL_KB + GDN_HEADS
COL_ZB = COL_VB + GDN_HEADS
COL_AB = O_AB // LANE
N_PROJ = (COL_AB + 1) * LANE

VMEM_LIMIT = 48 * 1024 * 1024
GDN_VMEM_LIMIT = 58 * 1024 * 1024
PROJ_CHUNK = 256


def _dot(a, b):
    return jnp.dot(a, b, preferred_element_type=F32)


def _dot_nt(a, b):
    return lax.dot_general(a, b, (((1,), (1,)), ((), ())), preferred_element_type=F32)


def _dot_tn(a, b):
    return lax.dot_general(a, b, (((0,), (0,)), ((), ())), preferred_element_type=F32)


def _rms(x, g):
    return x * lax.rsqrt(jnp.mean(x * x, axis=-1, keepdims=True) + EPS) * g


def _sigmoid(x):
    return 1.0 / (1.0 + jnp.exp(-x))


def _resident(shape):
    return pl.BlockSpec(shape, lambda *_: (0,) * len(shape), pipeline_mode=pl.Buffered(1))


def _split3(x):
    hi = x.astype(BF16)
    r = x - hi.astype(F32)
    mid = r.astype(BF16)
    lo = (r - mid.astype(F32)).astype(BF16)
    return hi, mid, lo


def _mem_kv_kernel(m_ref, g_ref, w_ref, o_ref, ob_ref):
    h = _rms(m_ref[...], g_ref[...]).astype(BF16)
    kv = _dot(h, w_ref[...])
    o_ref[...] = kv
    ob_ref[...] = kv.astype(BF16)


def _mem_kv(mem, g_mem, w_bf16):
    n = w_bf16.shape[1]
    tn = 1024
    return pl.pallas_call(
        _mem_kv_kernel,
        grid=(n // tn,),
        in_specs=[
            pl.BlockSpec((N_MEM, D_MODEL), lambda j: (0, 0)),
            pl.BlockSpec((1, D_MODEL), lambda j: (0, 0)),
            pl.BlockSpec((D_MODEL, tn), lambda j: (0, j)),
        ],
        out_specs=[pl.BlockSpec((N_MEM, tn), lambda j: (0, j))] * 2,
        out_shape=[jax.ShapeDtypeStruct((N_MEM, n), F32), jax.ShapeDtypeStruct((N_MEM, n), BF16)],
        compiler_params=pltpu.CompilerParams(dimension_semantics=("arbitrary",), vmem_limit_bytes=VMEM_LIMIT),
        name="mem_kv",
    )(mem, g_mem, w_bf16)


def _rope(x, cos, sin_signed):
    lane = lax.broadcasted_iota(jnp.int32, x.shape, 1)
    first = (lane % SWA_HEAD_DIM) < (SWA_HEAD_DIM // 2)
    rot = jnp.where(first, pltpu.roll(x, LANE - SWA_HEAD_DIM // 2, 1), pltpu.roll(x, SWA_HEAD_DIM // 2, 1))
    return x * cos + rot * sin_signed


SWA_KA = SWA_Q // LANE
SWA_VA = SWA_KA + 1
KEYS = WINDOW + CHUNK
KPAD = 256


def _swa_kernel(sink_ref, q_ref, k_ref, v_ref, kh_ref, vh_ref, base_ref, off_ref, *rest, tq, cache, host_proj):
    if host_proj:
        (x_ref, gpre_ref, wq_ref, bq_ref, o_ref, kout_ref, qc_ref, gl_ref,
         qs_ref, ka_ref, kb_ref, va_ref, vb_ref, h_scr) = rest
    else:
        o_ref, kout_ref, qs_ref, ka_ref, kb_ref, va_ref, vb_ref = rest
    i = pl.program_id(1)
    pending = []
    if host_proj:
        h_scr[...] = _rms(x_ref[...], gpre_ref[...]).astype(BF16)
        pending = list(range((MEM_Q + N_BRANCH * D_MODEL) // PROJ_CHUNK))

    def emit(n=1):
        for _ in range(n):
            if pending:
                c = pending.pop(0)
                cols = slice(c * PROJ_CHUNK, (c + 1) * PROJ_CHUNK)
                y = _dot(h_scr[...], wq_ref[:, cols]) + bq_ref[:, cols]
                if c < MEM_Q // PROJ_CHUNK:
                    qc_ref[:, cols] = (y * (MEM_HEAD_DIM ** -0.5)).astype(BF16)
                else:
                    gl_ref[:, c * PROJ_CHUNK - MEM_Q:(c + 1) * PROJ_CHUNK - MEM_Q] = y

    cb, sb = base_ref[0:1, :], base_ref[1:2, :]
    co, so = off_ref[0], off_ref[1]
    cos_all = cb * co - sb * so
    sin_all = sb * co + cb * so
    cos, sin = cos_all[WINDOW:], sin_all[WINDOW:]
    kr = _rope(k_ref[...], cos, sin)
    kout_ref[...] = kr
    if cache:
        khr = kh_ref[...]
    else:
        khr = _rope(kh_ref[...], cos_all[:WINDOW], sin_all[:WINDOW])
    zpad = jnp.zeros((KPAD - KEYS, LANE), F32)
    kall = jnp.concatenate([khr, kr, zpad], axis=0)
    vall = jnp.concatenate([vh_ref[...], v_ref[...], zpad], axis=0)
    lo = lax.broadcasted_iota(jnp.int32, kall.shape, 1) < SWA_HEAD_DIM
    zero = jnp.zeros_like(kall)
    for src, a_ref, b_ref in ((kall, ka_ref, kb_ref), (vall, va_ref, vb_ref)):
        rolled = pltpu.roll(src, SWA_HEAD_DIM, 1)
        a_ref[0] = jnp.where(lo, src, zero).astype(BF16)
        b_ref[0] = jnp.where(lo, zero, rolled).astype(BF16)
        a_ref[1] = jnp.where(lo, rolled, zero).astype(BF16)
        b_ref[1] = jnp.where(lo, zero, src).astype(BF16)

    scale = SWA_HEAD_DIM ** -0.5
    for p in range(SWA_HEADS // 2):
        sl = slice(p * LANE, (p + 1) * LANE)
        qs_ref[:, sl] = (_rope(q_ref[:, sl], cos, sin) * scale).astype(BF16)

    jj = lax.broadcasted_iota(jnp.int32, (CHUNK, KPAD), 1)
    pairs_per_kv = SWA_HEADS // SWA_KV_HEADS // 2
    inst = [(c, g) for c in range(tq // CHUNK) for g in range(SWA_KV_HEADS)]

    def scores(c, g):
        rows = slice(c * CHUNK, (c + 1) * CHUNK)
        krows = slice(c * CHUNK, c * CHUNK + KPAD)
        kbd = jnp.concatenate([ka_ref[g, krows, :], kb_ref[g, krows, :]], axis=0)
        qst = jnp.concatenate(
            [qs_ref[rows, (g * pairs_per_kv + j) * LANE:(g * pairs_per_kv + j + 1) * LANE]
             for j in range(pairs_per_kv)], axis=0)
        return _dot_nt(qst, kbd)

    def probs(c, g, s):
        if cache:
            valid = jj < KEYS
        else:
            valid = (jj < KEYS) & ((i > 0) | (jj + c * CHUNK >= WINDOW))
        pblocks = []
        for j in range(pairs_per_kv):
            halves = []
            for half in range(2):
                sh = s[j * CHUNK:(j + 1) * CHUNK, half * KPAD:(half + 1) * KPAD]
                sh = jnp.where(valid, sh, -jnp.inf)
                sink = sink_ref[2 * (g * pairs_per_kv + j) + half]
                m = jnp.maximum(jnp.max(sh, axis=-1, keepdims=True), sink)
                e = jnp.exp(sh - m)
                den = jnp.sum(e, axis=-1, keepdims=True) + jnp.exp(sink - m)
                halves.append((e * (1.0 / den)).astype(BF16))
            pblocks.append(jnp.concatenate(halves, axis=1))
        return jnp.concatenate(pblocks, axis=0)

    def outputs(c, g, pmat):
        rows = slice(c * CHUNK, (c + 1) * CHUNK)
        krows = slice(c * CHUNK, c * CHUNK + KPAD)
        vbd = jnp.concatenate([va_ref[g, krows, :], vb_ref[g, krows, :]], axis=0)
        o = _dot(pmat, vbd)
        for j in range(pairs_per_kv):
            p = g * pairs_per_kv + j
            o_ref[rows, p * LANE:(p + 1) * LANE] = o[j * CHUNK:(j + 1) * CHUNK].astype(BF16)

    s_l, p_l = {}, {}
    for step in range(len(inst) + 2):
        if step < len(inst):
            s_l[step] = scores(*inst[step])
        if 0 <= step - 1 < len(inst):
            p_l[step - 1] = probs(*inst[step - 1], s_l.pop(step - 1))
        if 0 <= step - 2 < len(inst):
            outputs(*inst[step - 2], p_l.pop(step - 2))
        emit()
    emit(len(pending))


def _swa(proj, k_hist, v_hist, rope_base, rope_off, sinks, nb, t, tq, cache, host=None):
    nt = t // tq
    kv_rows = WINDOW + tq + (KPAD - KEYS)
    if cache:
        hist_k_spec = pl.BlockSpec((None, WINDOW, LANE), lambda b, i: (b, 0, 0))
        hist_v_spec = pl.BlockSpec((None, WINDOW, LANE), lambda b, i: (b, 0, 0))
        hk, hv = k_hist, v_hist
    else:
        r = tq // WINDOW

        def hidx(b, i):
            return jnp.maximum((b * nt + i) * r - 1, 0)

        hist_k_spec = pl.BlockSpec((WINDOW, LANE), lambda b, i: (hidx(b, i), SWA_KA))
        hist_v_spec = pl.BlockSpec((WINDOW, LANE), lambda b, i: (hidx(b, i), SWA_VA))
        hk, hv = proj, proj

    def rows(width):
        return pl.BlockSpec((tq, width), lambda b, i: (b * nt + i, 0))

    in_specs = [
        pl.BlockSpec(memory_space=pltpu.SMEM),
        rows(SWA_Q),
        pl.BlockSpec((tq, LANE), lambda b, i: (b * nt + i, SWA_KA)),
        pl.BlockSpec((tq, LANE), lambda b, i: (b * nt + i, SWA_VA)),
        hist_k_spec,
        hist_v_spec,
        pl.BlockSpec((None, 2, LANE), lambda b, i: (i, 0, 0)),
        pl.BlockSpec((2, WINDOW + tq, LANE), lambda b, i: (0, 0, 0)),
    ]
    args = [sinks, proj, proj, proj, hk, hv, rope_base, rope_off]
    out_specs = [rows(SWA_Q), rows(LANE)]
    out_shape = [jax.ShapeDtypeStruct((nb * t, SWA_Q), BF16), jax.ShapeDtypeStruct((nb * t, LANE), F32)]
    scratch = [
        pltpu.VMEM((tq, SWA_Q), BF16),
        pltpu.VMEM((SWA_KV_HEADS, kv_rows, LANE), BF16),
        pltpu.VMEM((SWA_KV_HEADS, kv_rows, LANE), BF16),
        pltpu.VMEM((SWA_KV_HEADS, kv_rows, LANE), BF16),
        pltpu.VMEM((SWA_KV_HEADS, kv_rows, LANE), BF16),
    ]
    if host is not None:
        x, gpre, wq, bq = host
        nq = MEM_Q + N_BRANCH * D_MODEL
        in_specs += [rows(D_MODEL), pl.BlockSpec((1, D_MODEL), lambda b, i: (0, 0)),
                     _resident((D_MODEL, nq)), _resident((1, nq))]
        args += [x, gpre, wq, bq]
        out_specs += [rows(MEM_Q), rows(N_BRANCH * D_MODEL)]
        out_shape += [jax.ShapeDtypeStruct((nb * t, MEM_Q), BF16),
                      jax.ShapeDtypeStruct((nb * t, N_BRANCH * D_MODEL), F32)]
        scratch += [pltpu.VMEM((tq, D_MODEL), BF16)]
    return pl.pallas_call(
        functools.partial(_swa_kernel, tq=tq, cache=cache, host_proj=host is not None),
        grid=(nb, nt),
        in_specs=in_specs,
        out_specs=out_specs,
        out_shape=out_shape,
        scratch_shapes=scratch,
        compiler_params=pltpu.CompilerParams(dimension_semantics=("parallel", "arbitrary"),
                                             vmem_limit_bytes=GDN_VMEM_LIMIT if host is not None else VMEM_LIMIT),
        name="swa",
    )(*args)


INV_BASE = 16


def _inverse_minus_identity(pms, row, col, chunk, between=lambda: None):
    base = (row // INV_BASE) == (col // INV_BASE)
    diag = row == col
    ms = [jnp.where(base, pm, 0.0) for pm in pms]
    ts = [jnp.where(diag, 1.0, m) for m in ms]
    for _ in range(INV_BASE.bit_length() - 2):
        mbs = [m.astype(BF16) for m in ms]
        ms = [_dot(mb, mb) for mb in mbs]
        between()
        ts = [t + _dot(m.astype(BF16), t.astype(BF16)) for m, t in zip(ms, ts)]
    bs = INV_BASE
    while bs < chunk:
        inner = (row // bs) == (col // bs)
        outer = (row // (2 * bs)) == (col // (2 * bs))
        sib = outer & jnp.logical_not(inner)
        xs = [_dot(jnp.where(sib, pm, 0.0).astype(BF16), t.astype(BF16)) for pm, t in zip(pms, ts)]
        ts = [t + _dot(t.astype(BF16), x.astype(BF16)) for t, x in zip(ts, xs)]
        between()
        bs *= 2
    rs = [jnp.where(diag, 0.0, t) for t in ts]
    return rs


def _gdn_kernel(x0_ref, xn_ref, gpre_ref, w_ref, b_ref, hist_ref, cw_ref, alog_ref, dt_ref, ng_ref, s0_ref,
                o_ref, sout_ref, swa_ref, clast_ref, proj_scr, h_scr, s_scr, carry_scr, ext_scr, conv_scr,
                *, tt, seg, chunk, carry):
    i = pl.program_id(0)
    nseg = tt // seg
    nblk = LANE // chunk
    ngroups = tt // LANE
    hist_rows = CONV_W - 1
    cur = proj_scr.at[i % 2]
    nxt = proj_scr.at[1 - i % 2]

    nchunks = pl.cdiv(N_PROJ, PROJ_CHUNK)

    def project(dst, c):
        cols = slice(c * PROJ_CHUNK, min((c + 1) * PROJ_CHUNK, N_PROJ))
        dst[8:8 + tt, cols] = _dot(h_scr[...], w_ref[:, cols]) + b_ref[:, cols]

    @pl.when(i == 0)
    def _():
        h_scr[...] = _rms(x0_ref[...], gpre_ref[...]).astype(BF16)
        for c in range(nchunks):
            project(proj_scr.at[0], c)

    h_scr[...] = _rms(xn_ref[...], gpre_ref[...]).astype(BF16)
    first = COL_QB * LANE // PROJ_CHUNK
    pending = list(range(first, nchunks - 1)) + list(range(first)) + [nchunks - 1]

    def emit(n=1):
        for _ in range(n):
            if pending:
                project(nxt, pending.pop(0))

    def init():
        for s in range(nseg):
            s_scr[s] = s0_ref[s]
            carry_scr[s] = jnp.zeros((8, 3 * GDN_QK), F32)
            carry_scr[s, 8 - hist_rows:8, :] = hist_ref[s]

    if carry:
        pl.when(i == 0)(init)
    else:
        init()

    swa_ref[...] = cur[8:8 + tt, COL_QA * LANE:(COL_VA + 1) * LANE]

    qkv = slice(COL_QB * LANE, COL_ZB * LANE)
    base = 8 - hist_rows
    for s in range(nseg):
        if carry:
            cur[0:8, qkv] = carry_scr[s]
            src, col0 = cur, COL_QB * LANE
            last = cur[tt:tt + 8, qkv]
            carry_scr[s] = last
        else:
            ext_scr[0:8, :] = carry_scr[s]
            ext_scr[8:8 + seg, :] = cur[8 + s * seg:8 + (s + 1) * seg, qkv]
            src, col0 = ext_scr, 0
            last = ext_scr[seg:seg + 8, :]
        clast_ref[s] = last
        for cb in range(3 * GDN_HEADS):
            lanes = slice(cb * LANE, (cb + 1) * LANE)
            slanes = slice(col0 + cb * LANE, col0 + (cb + 1) * LANE)
            y = src[base:base + seg, slanes] * cw_ref[0:1, lanes]
            for tap in range(1, CONV_W):
                y = y + src[base + tap:base + tap + seg, slanes] * cw_ref[tap:tap + 1, lanes]
            conv_scr[s * seg:(s + 1) * seg, lanes] = y * _sigmoid(y)
            if s == 0 and cb % 2 == 0:
                emit()

    row = lax.broadcasted_iota(jnp.int32, (LANE, LANE), 0)
    col = lax.broadcasted_iota(jnp.int32, (LANE, LANE), 1)
    same = (row // chunk) == (col // chunk)
    incl = same & (row >= col)
    strict = same & (row > col)
    lmask = jnp.where(incl, 1.0, 0.0).astype(BF16)
    row2 = jnp.concatenate([row, row], axis=0)

    inst = [(gi, hh) for gi in range(ngroups) for hh in range(GDN_HEADS)]
    heads = range(GDN_HEADS)

    gates = []
    for gi in range(ngroups):
        ab = cur[8 + gi * LANE:8 + (gi + 1) * LANE, COL_AB * LANE:(COL_AB + 1) * LANE]
        xa = ab + dt_ref[...]
        softplus = jnp.maximum(xa, 0.0) + jnp.log1p(jnp.exp(-jnp.abs(xa)))
        la = -jnp.exp(alog_ref[...]) * softplus
        beta = _sigmoid(ab)
        g_all = sum(_dot(lmask, piece) for piece in _split3(la))
        gates.append((g_all, g_all.T, beta))

    qn_l, kn_l, v_l, gb_l, bb_l, decay_l = [], [], [], [], [], []
    for gi, hh in inst:
        rows = slice(gi * LANE, (gi + 1) * LANE)
        q = conv_scr[rows, hh * LANE:(hh + 1) * LANE]
        k = conv_scr[rows, GDN_QK + hh * LANE:GDN_QK + (hh + 1) * LANE]
        v_l.append(conv_scr[rows, 2 * GDN_QK + hh * LANE:2 * GDN_QK + (hh + 1) * LANE])
        qn_l.append(q * (lax.rsqrt(jnp.sum(q * q, axis=-1, keepdims=True) + EPS) * (GDN_DK ** -0.5)))
        kn_l.append(k * lax.rsqrt(jnp.sum(k * k, axis=-1, keepdims=True) + EPS))
        g_all, g_all_t, beta = gates[gi]
        gb = jnp.broadcast_to(g_all[:, hh:hh + 1], (LANE, LANE))
        grow = jnp.broadcast_to(g_all_t[hh:hh + 1, :], (LANE, LANE))
        gb_l.append(gb)
        bb_l.append(jnp.broadcast_to(beta[:, GDN_HEADS + hh:GDN_HEADS + hh + 1], (LANE, LANE)))
        decay_l.append(jnp.exp(jnp.where(incl, gb - grow, -jnp.inf)))

    emit()
    knb_l = [kn.astype(BF16) for kn in kn_l]
    cat_l = [_dot_nt(jnp.concatenate([qn.astype(BF16), knb], axis=0), knb)
             for qn, knb in zip(qn_l, knb_l)]
    emit()
    qk_l = [cat[:LANE] * decay for cat, decay in zip(cat_l, decay_l)]
    pm_l = [jnp.where(strict, -(bb * cat[LANE:] * decay), 0.0)
            for bb, cat, decay in zip(bb_l, cat_l, decay_l)]
    r_l = _inverse_minus_identity(pm_l, row, col, chunk)
    emit()
    eg_l = [jnp.exp(gb) for gb in gb_l]
    rhs_l = [jnp.concatenate([v * bb, kn * (bb * eg)], axis=1)
             for v, bb, kn, eg in zip(v_l, bb_l, kn_l, eg_l)]
    sol_l = [rhs + _dot(r.astype(BF16), rhs.astype(BF16)) for r, rhs in zip(r_l, rhs_l)]
    wq_l = [jnp.concatenate([sol[:, GDN_DV:], qn * eg], axis=0).astype(BF16)
            for sol, qn, eg in zip(sol_l, qn_l, eg_l)]
    kd_l = []
    for kn, gb in zip(kn_l, gb_l):
        glast = gb[chunk - 1:chunk, :]
        for blk in range(1, nblk):
            glast = jnp.where(row >= blk * chunk, gb[(blk + 1) * chunk - 1:(blk + 1) * chunk, :], glast)
        kd_l.append(kn * jnp.exp(glast - gb))

    emit()
    nstate = 1 if carry else nblk
    state = [[s_scr[sidx, hh] for hh in heads] for sidx in range(nstate)]
    for gi in range(ngroups):
        ix = [gi * GDN_HEADS + hh for hh in heads]
        ws_l = []
        for hh in heads:
            ws = None
            for blk in range(nblk):
                part = _dot(wq_l[ix[hh]], state[0 if carry else blk][hh].astype(BF16))
                ws = part if ws is None else jnp.where(row2 >= blk * chunk, part, ws)
            ws_l.append(ws)
        emit()
        vnb_l = [(sol_l[ix[hh]][:, :GDN_DV] - ws_l[hh][:LANE]).astype(BF16) for hh in heads]
        o_l = [ws_l[hh][LANE:] + _dot(qk_l[ix[hh]].astype(BF16), vnb_l[hh]) for hh in heads]
        emit()
        for blk in range(nblk):
            sidx = 0 if carry else blk
            new = []
            for hh in heads:
                kd = kd_l[ix[hh]]
                gb = gb_l[ix[hh]]
                kdb = kd if nblk == 1 else jnp.where((row // chunk) == blk, kd, 0.0)
                gtot = jnp.exp(gb[(blk + 1) * chunk - 1:(blk + 1) * chunk, :])
                new.append(state[sidx][hh] * gtot + _dot_tn(kdb.astype(BF16), vnb_l[hh]))
            state[sidx] = new
        for hh in heads:
            rows = slice(gi * LANE, (gi + 1) * LANE)
            lanes = slice(hh * LANE, (hh + 1) * LANE)
            z = cur[8 + gi * LANE:8 + (gi + 1) * LANE, COL_ZB * LANE + hh * LANE:COL_ZB * LANE + (hh + 1) * LANE]
            o_ref[rows, lanes] = (_rms(o_l[hh], ng_ref[...]) * (z * _sigmoid(z))).astype(BF16)
    emit(len(pending))
    for sidx in range(nstate):
        for hh in heads:
            s_scr[sidx, hh] = state[sidx][hh]

    def fin():
        for s in range(nseg):
            sout_ref[s] = s_scr[s]

    if carry:
        pl.when(i == pl.num_programs(0) - 1)(fin)
    else:
        fin()


def _gdn(x, gpre, w_bf16, b, hist, s0, conv_w, alog_row, dt_row, norm_g, tt, seg, chunk, carry):
    t_rows = x.shape[0]
    nt = t_rows // tt
    nseg = tt // seg
    nseq = s0.shape[0]
    if carry:
        seq_idx = lambda i: 0
    else:
        seq_idx = lambda i: i
    swa_cols = (COL_VA + 1 - COL_QA) * LANE
    row_spec = pl.BlockSpec((1, LANE), lambda i: (0, 0))
    sspec = pl.BlockSpec((nseg, GDN_HEADS, GDN_DK, GDN_DV), lambda i: (seq_idx(i), 0, 0, 0))
    cspec = pl.BlockSpec((nseg, 8, 3 * GDN_QK), lambda i: (seq_idx(i), 0, 0))
    return pl.pallas_call(
        functools.partial(_gdn_kernel, tt=tt, seg=seg, chunk=chunk, carry=carry),
        grid=(nt,),
        in_specs=[
            pl.BlockSpec((tt, D_MODEL), lambda i: (0, 0)),
            pl.BlockSpec((tt, D_MODEL), lambda i: (jnp.minimum(i + 1, nt - 1), 0)),
            pl.BlockSpec((1, D_MODEL), lambda i: (0, 0)),
            _resident((D_MODEL, N_PROJ)),
            _resident((1, N_PROJ)),
            pl.BlockSpec((nseg, CONV_W - 1, 3 * GDN_QK), lambda i: (seq_idx(i), 0, 0)),
            pl.BlockSpec((CONV_W, 3 * GDN_QK), lambda i: (0, 0)),
            row_spec, row_spec, row_spec,
            sspec,
        ],
        out_specs=[
            pl.BlockSpec((tt, GDN_V), lambda i: (i, 0)),
            sspec,
            pl.BlockSpec((tt, swa_cols), lambda i: (i, 0)),
            cspec,
        ],
        out_shape=[
            jax.ShapeDtypeStruct((t_rows, GDN_V), BF16),
            jax.ShapeDtypeStruct((nseq, GDN_HEADS, GDN_DK, GDN_DV), F32),
            jax.ShapeDtypeStruct((t_rows, swa_cols), F32),
            jax.ShapeDtypeStruct((nseq, 8, 3 * GDN_QK), F32),
        ],
        scratch_shapes=[
            pltpu.VMEM((2, 8 + tt, N_PROJ), F32),
            pltpu.VMEM((tt, D_MODEL), BF16),
            pltpu.VMEM((nseg, GDN_HEADS, GDN_DK, GDN_DV), F32),
            pltpu.VMEM((nseg, 8, 3 * GDN_QK), F32),
            pltpu.VMEM((8 + seg, 3 * GDN_QK), F32),
            pltpu.VMEM((tt, 3 * GDN_QK), F32),
        ],
        compiler_params=pltpu.CompilerParams(dimension_semantics=("arbitrary",),
                                             vmem_limit_bytes=GDN_VMEM_LIMIT),
        name="gdn",
    )(x, x, gpre, w_bf16, b, hist, conv_w, alog_row, dt_row, norm_g, s0)


def _mix_kernel(x_ref, oa_ref, ob_ref, mk_ref, mv_ref, *rest, hosted):
    if hosted:
        qc_ref, gl_ref, wb_ref, wo_ref, gp_ref, o_ref = rest
    else:
        gpre_ref, wq_ref, bq_ref, wb_ref, wo_ref, gp_ref, o_ref = rest
    x = x_ref[...]
    tm = x.shape[0]
    nmem = mk_ref.shape[0]
    rpm = tm // nmem
    if hosted:
        qc = qc_ref[...]
    else:
        h = _rms(x, gpre_ref[...]).astype(BF16)
        qc = ((_dot(h, wq_ref[:, :MEM_Q]) + bq_ref[:, :MEM_Q]) * (MEM_HEAD_DIM ** -0.5)).astype(BF16)
    def gate(n):
        if hosted:
            gl = gl_ref[:, n * D_MODEL:(n + 1) * D_MODEL]
        else:
            cols = slice(MEM_Q + n * D_MODEL, MEM_Q + (n + 1) * D_MODEL)
            gl = _dot(h, wq_ref[:, cols]) + bq_ref[:, cols]
        return _sigmoid(gl)

    pairs = [(b, hh) for b in range(nmem) for hh in range(MEM_HEADS)]
    hsl = [slice(hh * MEM_HEAD_DIM, (hh + 1) * MEM_HEAD_DIM) for hh in range(MEM_HEADS)]
    s_l = [_dot_nt(qc[b * rpm:(b + 1) * rpm, hsl[hh]], mk_ref[b, :, hsl[hh]].astype(BF16)) for b, hh in pairs]
    merged = gate(0) * _dot(oa_ref[...], wb_ref[0]) + gate(1) * _dot(ob_ref[...], wb_ref[1])
    p_l = []
    for s in s_l:
        m = jnp.max(s, axis=-1, keepdims=True)
        e = jnp.exp(s - m)
        p_l.append((e * (1.0 / jnp.sum(e, axis=-1, keepdims=True))).astype(BF16))
    o_l = [_dot(p, mv_ref[b, :, hsl[hh]].astype(BF16)).astype(BF16) for p, (b, hh) in zip(p_l, pairs)]
    blocks = [jnp.concatenate(o_l[b * MEM_HEADS:(b + 1) * MEM_HEADS], axis=1) for b in range(nmem)]
    oc = blocks[0] if nmem == 1 else jnp.concatenate(blocks, axis=0)
    merged = merged + gate(2) * _dot(oc, wb_ref[2])
    y = _dot(merged.astype(BF16), wo_ref[...])
    o_ref[...] = x + _rms(y, gp_ref[...])


def _mix(x, oa, ob, mk, mv, proj, wb, wo, gp, tm, rows_per_mem):
    m = x.shape[0]
    tile = pl.BlockSpec((tm, D_MODEL), lambda i: (i, 0))
    vec = pl.BlockSpec((1, D_MODEL), lambda i: (0, 0))
    if mk.shape[0] == 1:
        mspec = _resident((1, N_MEM, MEM_Q))
    else:
        mspec = pl.BlockSpec((tm // rows_per_mem, N_MEM, MEM_Q), lambda i: (i, 0, 0))
    nq = MEM_Q + N_BRANCH * D_MODEL
    hosted = len(proj) == 2
    if hosted:
        pspecs = [tile, pl.BlockSpec((tm, N_BRANCH * D_MODEL), lambda i: (i, 0))]
    else:
        pspecs = [vec, _resident((D_MODEL, nq)), _resident((1, nq))]
    return pl.pallas_call(
        functools.partial(_mix_kernel, hosted=hosted),
        grid=(m // tm,),
        in_specs=[tile, tile, tile, mspec, mspec] + pspecs + [
            _resident((N_BRANCH, D_MODEL, D_MODEL)),
            _resident((D_MODEL, D_MODEL)),
            vec,
        ],
        out_specs=tile,
        out_shape=jax.ShapeDtypeStruct((m, D_MODEL), F32),
        compiler_params=pltpu.CompilerParams(dimension_semantics=("parallel",), vmem_limit_bytes=VMEM_LIMIT),
        name="mix",
    )(x, oa, ob, mk, mv, *proj, wb, wo, gp)


FF_CHUNK = 1024
FFN_TM = 512


def _ffn_kernel(xa_ref, xb_ref, g1_ref, wu_ref, wd_ref, g2_ref, oa_ref, ob_ref, *, na):
    def run(x_ref, o_ref):
        x = x_ref[...]
        hf = _rms(x, g1_ref[...]).astype(BF16)
        f = None
        for c in range(D_FF // FF_CHUNK):
            sl = slice(c * FF_CHUNK, (c + 1) * FF_CHUNK)
            up = jnp.maximum(_dot(hf, wu_ref[:, sl]), 0.0)
            part = _dot((up * up).astype(BF16), wd_ref[sl, :])
            f = part if f is None else f + part
        o_ref[...] = x + _rms(f, g2_ref[...])

    i = pl.program_id(0)
    pl.when(i < na)(functools.partial(run, xa_ref, oa_ref))
    pl.when(i >= na)(functools.partial(run, xb_ref, ob_ref))


def _ffn(xa, xb, g1, wu, wd, g2, tm):
    na, nb = xa.shape[0] // tm, xb.shape[0] // tm
    a_tile = pl.BlockSpec((tm, D_MODEL), lambda i: (jnp.minimum(i, na - 1), 0))
    b_tile = pl.BlockSpec((tm, D_MODEL), lambda i: (jnp.maximum(i - na, 0), 0))
    vec = pl.BlockSpec((1, D_MODEL), lambda i: (0, 0))
    return pl.pallas_call(
        functools.partial(_ffn_kernel, na=na),
        grid=(na + nb,),
        in_specs=[a_tile, b_tile, vec, _resident((D_MODEL, D_FF)), _resident((D_FF, D_MODEL)), vec],
        out_specs=[a_tile, b_tile],
        out_shape=[jax.ShapeDtypeStruct(xa.shape, F32), jax.ShapeDtypeStruct(xb.shape, F32)],
        compiler_params=pltpu.CompilerParams(dimension_semantics=("arbitrary",), vmem_limit_bytes=VMEM_LIMIT),
        name="ffn",
    )(xa, xb, g1, wu, wd, g2)


def _rope_tables(pos0, nt, tq):
    half = SWA_HEAD_DIM // 2
    inv = ROPE_THETA ** (-jnp.arange(half, dtype=F32) / half)
    inv = jnp.tile(inv, LANE // half)
    sgn = jnp.tile(jnp.concatenate([-jnp.ones((half,), F32), jnp.ones((half,), F32)]), LANE // SWA_HEAD_DIM)
    base = (pos0 + tq * jnp.arange(nt, dtype=jnp.int32)).astype(F32)[:, None] * inv[None, :]
    off = jnp.arange(-WINDOW, tq, dtype=jnp.int32).astype(F32)[:, None] * inv[None, :]
    rope_base = jnp.stack([jnp.cos(base), sgn * jnp.sin(base)], axis=1)
    rope_off = jnp.stack([jnp.cos(off), sgn * jnp.sin(off)], axis=0)
    return rope_base, rope_off


def _pad_row(v, offset=0):
    return jnp.zeros((1, LANE), F32).at[0, offset:offset + v.shape[0]].set(v)


def _layer(x, nb, t, pos0, k_hist, v_hist, s0, conv_hist, mk, mv, wts, cfg):
    ob, s_new, swa_in, conv_last = _gdn(x, wts['g_pre_mix'], wts['w_in'], wts['b_in'], conv_hist, s0,
                                        wts['conv_w'], wts['alog'], wts['dt'], wts['norm_g'],
                                        cfg['tt_gdn'], cfg['seg'], cfg['chunk'], cfg['carry'])
    rope_base, rope_off = _rope_tables(pos0, t // cfg['tq'], cfg['tq'])
    qproj = (wts['g_pre_mix'], wts['w_q'], wts['b_q'])
    if cfg['host_proj']:
        oa, k_rot, qc, gl = _swa(swa_in, k_hist, v_hist, rope_base, rope_off, wts['sinks'], nb, t, cfg['tq'],
                                 cfg['cache'], host=(x,) + qproj)
        qproj = (qc, gl)
    else:
        oa, k_rot = _swa(swa_in, k_hist, v_hist, rope_base, rope_off, wts['sinks'], nb, t, cfg['tq'],
                         cfg['cache'])
    x1 = _mix(x, oa, ob, mk, mv, qproj, wts['w_branch'], wts['w_out'], wts['g_post_mix'], cfg['tm_mix'], t)
    return x1, k_rot, s_new, swa_in, conv_last


def kernel(x_prompt, x_sample, mem_prompt, cache_swa_k, cache_swa_v, state_gdn, state_conv, cache_mem_k, cache_mem_v, g_pre_mix, w_in, b_in, swa_sinks, conv_w, gdn_a_log, gdn_dt_bias, gdn_norm_g, g_mem, w_mem_kv, w_branch, w_out, g_post_mix, g_pre_ffn, w_up, w_down, g_post_ffn):
    depth = w_in.shape[0]
    assert depth == 1
    l = 0
    bp, tp, _ = x_prompt.shape
    bs, ts, _ = x_sample.shape
    assert bp == 1

    w_in_bf = w_in[l].astype(BF16)
    wts = {
        'g_pre_mix': g_pre_mix[l][None, :],
        'w_in': w_in_bf,
        'b_in': b_in[l][None, :],
        'w_q': w_in_bf[:, O_QC:O_END],
        'b_q': b_in[l][None, O_QC:O_END],
        'sinks': swa_sinks[l],
        'conv_w': conv_w[l],
        'alog': _pad_row(gdn_a_log[l]),
        'dt': _pad_row(gdn_dt_bias[l]),
        'norm_g': gdn_norm_g[l][None, :],
        'w_branch': w_branch[l].astype(BF16),
        'w_out': w_out[l].astype(BF16),
        'g_post_mix': g_post_mix[l][None, :],
        'g_pre_ffn': g_pre_ffn[l][None, :],
        'w_up': w_up[l].astype(BF16),
        'w_down': w_down[l].astype(BF16),
        'g_post_ffn': g_post_ffn[l][None, :],
    }

    mkv, mkv_bf = _mem_kv(mem_prompt[0], g_mem[l][None, :], w_mem_kv[l].astype(BF16))
    cfg_p = dict(tq=512, cache=False, host_proj=True, tt_gdn=256, seg=256, chunk=128, carry=True,
                 tm_mix=512)
    zero_hist = jnp.zeros((1, CONV_W - 1, 3 * GDN_QK), F32)
    zero_s = jnp.zeros((1, GDN_HEADS, GDN_DK, GDN_DV), F32)
    x1p, kp_rot, sp_new, swa_p, clast_p = _layer(
        x_prompt.reshape(bp * tp, D_MODEL), bp, tp, 0, None, None,
        zero_s, zero_hist, mkv_bf[None, :, :MEM_Q], mkv_bf[None, :, MEM_Q:], wts, cfg_p)

    cfg_s = dict(tq=ts, cache=True, host_proj=False, tt_gdn=128, seg=ts, chunk=ts, carry=False,
                 tm_mix=4 * ts)
    x1s, ks_rot, ss_new, swa_s, clast_s = _layer(
        x_sample.reshape(bs * ts, D_MODEL), bs, ts, PAST_LEN,
        cache_swa_k[l].reshape(bs, WINDOW, SWA_KV), cache_swa_v[l].reshape(bs, WINDOW, SWA_KV),
        state_gdn[l], state_conv[l],
        cache_mem_k[l].reshape(bs, N_MEM, MEM_Q), cache_mem_v[l].reshape(bs, N_MEM, MEM_Q), wts, cfg_s)

    yp, ys = _ffn(x1p, x1s, wts['g_pre_ffn'], wts['w_up'], wts['w_down'], wts['g_post_ffn'], FFN_TM)

    kv_shape = (SWA_KV_HEADS, SWA_HEAD_DIM)
    va_cols = slice(SWA_VA * LANE, (SWA_VA + 1) * LANE)

    p_k = kp_rot[tp - WINDOW:].reshape((1, bp, WINDOW) + kv_shape)
    p_v = swa_p[tp - WINDOW:, va_cols].reshape((1, bp, WINDOW) + kv_shape)
    p_c = clast_p[:, 8 - (CONV_W - 1):][None]
    p_mk = mkv[:, :MEM_Q].reshape(1, bp, N_MEM, MEM_HEADS, MEM_HEAD_DIM)
    p_mv = mkv[:, MEM_Q:].reshape(1, bp, N_MEM, MEM_HEADS, MEM_HEAD_DIM)

    ks_new = ks_rot.reshape(bs, ts, SWA_KV)
    vs_new = swa_s[:, va_cols].reshape(bs, ts, SWA_KV)
    s_k = jnp.concatenate([cache_swa_k[l].reshape(bs, WINDOW, SWA_KV), ks_new], axis=1)[:, -WINDOW:]
    s_v = jnp.concatenate([cache_swa_v[l].reshape(bs, WINDOW, SWA_KV), vs_new], axis=1)[:, -WINDOW:]
    s_c = clast_s[:, 8 - (CONV_W - 1):]

    return (yp.reshape(bp, tp, D_MODEL), ys.reshape(bs, ts, D_MODEL),
            p_k, p_v, sp_new[None], p_c, p_mk, p_mv,
            s_k.reshape((1, bs, WINDOW) + kv_shape), s_v.reshape((1, bs, WINDOW) + kv_shape),
            ss_new[None], s_c[None])
```

```python
import functools

import jax
import jax.numpy as jnp
from jax import lax
from jax.experimental import pallas as pl
from jax.experimental.pallas import tpu as pltpu

F32 = jnp.float32
BF16 = jnp.bfloat16

D_MODEL = 1024
PAST_LEN = 2048
CHUNK = 64
N_MEM = 256
EPS = 1e-6
SWA_HEADS = 16
SWA_KV_HEADS = 2
SWA_HEAD_DIM = 64
WINDOW = 128
ROPE_THETA = 10000.0
GDN_HEADS = 8
GDN_DK = 128
GDN_DV = 128
CONV_W = 4
MEM_HEADS = 4
MEM_HEAD_DIM = 256
N_BRANCH = 3
D_FF = 4 * D_MODEL

LANE = 128
SWA_Q = SWA_HEADS * SWA_HEAD_DIM
SWA_KV = SWA_KV_HEADS * SWA_HEAD_DIM
GDN_QK = GDN_HEADS * GDN_DK
GDN_V = GDN_HEADS * GDN_DV
MEM_Q = MEM_HEADS * MEM_HEAD_DIM

O_KA = SWA_Q
O_QB = O_KA + 2 * SWA_KV
O_AB = O_QB + 2 * GDN_QK + 2 * GDN_V
O_QC = O_AB + 2 * GDN_HEADS
O_END = O_QC + MEM_Q + N_BRANCH * D_MODEL
COL_QA = 0
COL_KA = O_KA // LANE
COL_VA = COL_KA + 1
COL_QB = O_QB // LANE
COL_KB = COL_QB + GDN_HEADS
COL_VB = COL_KB + GDN_HEADS
COL_ZB = COL_VB + GDN_HEADS
COL_AB = O_AB // LANE
N_PROJ = (COL_AB + 1) * LANE

VMEM_LIMIT = 48 * 1024 * 1024
GDN_VMEM_LIMIT = 58 * 1024 * 1024
PROJ_CHUNK = 256


def _dot(a, b):
    return jnp.dot(a, b, preferred_element_type=F32)


def _dot_nt(a, b):
    return lax.dot_general(a, b, (((1,), (1,)), ((), ())), preferred_element_type=F32)


def _dot_tn(a, b):
    return lax.dot_general(a, b, (((0,), (0,)), ((), ())), preferred_element_type=F32)


def _rms(x, g):
    return x * lax.rsqrt(jnp.mean(x * x, axis=-1, keepdims=True) + EPS) * g


def _sigmoid(x):
    return 1.0 / (1.0 + jnp.exp(-x))


def _resident(shape):
    return pl.BlockSpec(shape, lambda *_: (0,) * len(shape), pipeline_mode=pl.Buffered(1))


def _split3(x):
    hi = x.astype(BF16)
    r = x - hi.astype(F32)
    mid = r.astype(BF16)
    lo = (r - mid.astype(F32)).astype(BF16)
    return hi, mid, lo


def _mem_kv_kernel(m_ref, g_ref, w_ref, o_ref, ob_ref):
    h = _rms(m_ref[...], g_ref[...]).astype(BF16)
    kv = _dot(h, w_ref[...])
    o_ref[...] = kv
    ob_ref[...] = kv.astype(BF16)


def _mem_kv(mem, g_mem, w_bf16):
    n = w_bf16.shape[1]
    tn = 1024
    return pl.pallas_call(
        _mem_kv_kernel,
        grid=(n // tn,),
        in_specs=[
            pl.BlockSpec((N_MEM, D_MODEL), lambda j: (0, 0)),
            pl.BlockSpec((1, D_MODEL), lambda j: (0, 0)),
            pl.BlockSpec((D_MODEL, tn), lambda j: (0, j)),
        ],
        out_specs=[pl.BlockSpec((N_MEM, tn), lambda j: (0, j))] * 2,
        out_shape=[jax.ShapeDtypeStruct((N_MEM, n), F32), jax.ShapeDtypeStruct((N_MEM, n), BF16)],
        compiler_params=pltpu.CompilerParams(dimension_semantics=("arbitrary",), vmem_limit_bytes=VMEM_LIMIT),
        name="mem_kv",
    )(mem, g_mem, w_bf16)


def _rope(x, cos, sin_signed):
    lane = lax.broadcasted_iota(jnp.int32, x.shape, 1)
    first = (lane % SWA_HEAD_DIM) < (SWA_HEAD_DIM // 2)
    rot = jnp.where(first, pltpu.roll(x, LANE - SWA_HEAD_DIM // 2, 1), pltpu.roll(x, SWA_HEAD_DIM // 2, 1))
    return x * cos + rot * sin_signed


SWA_KA = SWA_Q // LANE
SWA_VA = SWA_KA + 1
KEYS = WINDOW + CHUNK
KPAD = 256


def _swa_kernel(sink_ref, q_ref, k_ref, v_ref, kh_ref, vh_ref, base_ref, off_ref, *rest, tq, cache, host_proj):
    if host_proj:
        (x_ref, gpre_ref, wq_ref, bq_ref, o_ref, kout_ref, qc_ref, gl_ref,
         qs_ref, ka_ref, kb_ref, va_ref, vb_ref, h_scr) = rest
    else:
        o_ref, kout_ref, qs_ref, ka_ref, kb_ref, va_ref, vb_ref = rest
    i = pl.program_id(1)
    pending = []
    if host_proj:
        h_scr[...] = _rms(x_ref[...], gpre_ref[...]).astype(BF16)
        pending = list(range((MEM_Q + N_BRANCH * D_MODEL) // PROJ_CHUNK))

    def emit(n=1):
        for _ in range(n):
            if pending:
                c = pending.pop(0)
                cols = slice(c * PROJ_CHUNK, (c + 1) * PROJ_CHUNK)
                y = _dot(h_scr[...], wq_ref[:, cols]) + bq_ref[:, cols]
                if c < MEM_Q // PROJ_CHUNK:
                    qc_ref[:, cols] = (y * (MEM_HEAD_DIM ** -0.5)).astype(BF16)
                else:
                    gl_ref[:, c * PROJ_CHUNK - MEM_Q:(c + 1) * PROJ_CHUNK - MEM_Q] = y

    cb, sb = base_ref[0:1, :], base_ref[1:2, :]
    co, so = off_ref[0], off_ref[1]
    cos_all = cb * co - sb * so
    sin_all = sb * co + cb * so
    cos, sin = cos_all[WINDOW:], sin_all[WINDOW:]
    kr = _rope(k_ref[...], cos, sin)
    kout_ref[...] = kr
    if cache:
        khr = kh_ref[...]
    else:
        khr = _rope(kh_ref[...], cos_all[:WINDOW], sin_all[:WINDOW])
    zpad = jnp.zeros((KPAD - KEYS, LANE), F32)
    kall = jnp.concatenate([khr, kr, zpad], axis=0)
    vall = jnp.concatenate([vh_ref[...], v_ref[...], zpad], axis=0)
    lo = lax.broadcasted_iota(jnp.int32, kall.shape, 1) < SWA_HEAD_DIM
    zero = jnp.zeros_like(kall)
    for src, a_ref, b_ref in ((kall, ka_ref, kb_ref), (vall, va_ref, vb_ref)):
        rolled = pltpu.roll(src, SWA_HEAD_DIM, 1)
        a_ref[0] = jnp.where(lo, src, zero).astype(BF16)
        b_ref[0] = jnp.where(lo, zero, rolled).astype(BF16)
        a_ref[1] = jnp.where(lo, rolled, zero).astype(BF16)
        b_ref[1] = jnp.where(lo, zero, src).astype(BF16)

    scale = SWA_HEAD_DIM ** -0.5
    for p in range(SWA_HEADS // 2):
        sl = slice(p * LANE, (p + 1) * LANE)
        qs_ref[:, sl] = (_rope(q_ref[:, sl], cos, sin) * scale).astype(BF16)

    jj = lax.broadcasted_iota(jnp.int32, (CHUNK, KPAD), 1)
    pairs_per_kv = SWA_HEADS // SWA_KV_HEADS // 2
    inst = [(c, g) for c in range(tq // CHUNK) for g in range(SWA_KV_HEADS)]

    def scores(c, g):
        rows = slice(c * CHUNK, (c + 1) * CHUNK)
        krows = slice(c * CHUNK, c * CHUNK + KPAD)
        kbd = jnp.concatenate([ka_ref[g, krows, :], kb_ref[g, krows, :]], axis=0)
        qst = jnp.concatenate(
            [qs_ref[rows, (g * pairs_per_kv + j) * LANE:(g * pairs_per_kv + j + 1) * LANE]
             for j in range(pairs_per_kv)], axis=0)
        return _dot_nt(qst, kbd)

    def probs(c, g, s):
        if cache:
            valid = jj < KEYS
        else:
            valid = (jj < KEYS) & ((i > 0) | (jj + c * CHUNK >= WINDOW))
        pblocks = []
        for j in range(pairs_per_kv):
            halves = []
            for half in range(2):
                sh = s[j * CHUNK:(j + 1) * CHUNK, half * KPAD:(half + 1) * KPAD]
                sh = jnp.where(valid, sh, -jnp.inf)
                sink = sink_ref[2 * (g * pairs_per_kv + j) + half]
                m = jnp.maximum(jnp.max(sh, axis=-1, keepdims=True), sink)
                e = jnp.exp(sh - m)
                den = jnp.sum(e, axis=-1, keepdims=True) + jnp.exp(sink - m)
                halves.append((e * (1.0 / den)).astype(BF16))
            pblocks.append(jnp.concatenate(halves, axis=1))
        return jnp.concatenate(pblocks, axis=0)

    def outputs(c, g, pmat):
        rows = slice(c * CHUNK, (c + 1) * CHUNK)
        krows = slice(c * CHUNK, c * CHUNK + KPAD)
        vbd = jnp.concatenate([va_ref[g, krows, :], vb_ref[g, krows, :]], axis=0)
        o = _dot(pmat, vbd)
        for j in range(pairs_per_kv):
            p = g * pairs_per_kv + j
            o_ref[rows, p * LANE:(p + 1) * LANE] = o[j * CHUNK:(j + 1) * CHUNK].astype(BF16)

    s_l, p_l = {}, {}
    for step in range(len(inst) + 2):
        if step < len(inst):
            s_l[step] = scores(*inst[step])
        if 0 <= step - 1 < len(inst):
            p_l[step - 1] = probs(*inst[step - 1], s_l.pop(step - 1))
        if 0 <= step - 2 < len(inst):
            outputs(*inst[step - 2], p_l.pop(step - 2))
        emit()
    emit(len(pending))


def _swa(proj, k_hist, v_hist, rope_base, rope_off, sinks, nb, t, tq, cache, host=None):
    nt = t // tq
    kv_rows = WINDOW + tq + (KPAD - KEYS)
    if cache:
        hist_k_spec = pl.BlockSpec((None, WINDOW, LANE), lambda b, i: (b, 0, 0))
        hist_v_spec = pl.BlockSpec((None, WINDOW, LANE), lambda b, i: (b, 0, 0))
        hk, hv = k_hist, v_hist
    else:
        r = tq // WINDOW

        def hidx(b, i):
            return jnp.maximum((b * nt + i) * r - 1, 0)

        hist_k_spec = pl.BlockSpec((WINDOW, LANE), lambda b, i: (hidx(b, i), SWA_KA))
        hist_v_spec = pl.BlockSpec((WINDOW, LANE), lambda b, i: (hidx(b, i), SWA_VA))
        hk, hv = proj, proj

    def rows(width):
        return pl.BlockSpec((tq, width), lambda b, i: (b * nt + i, 0))

    in_specs = [
        pl.BlockSpec(memory_space=pltpu.SMEM),
        rows(SWA_Q),
        pl.BlockSpec((tq, LANE), lambda b, i: (b * nt + i, SWA_KA)),
        pl.BlockSpec((tq, LANE), lambda b, i: (b * nt + i, SWA_VA)),
        hist_k_spec,
        hist_v_spec,
        pl.BlockSpec((None, 2, LANE), lambda b, i: (i, 0, 0)),
        pl.BlockSpec((2, WINDOW + tq, LANE), lambda b, i: (0, 0, 0)),
    ]
    args = [sinks, proj, proj, proj, hk, hv, rope_base, rope_off]
    out_specs = [rows(SWA_Q), rows(LANE)]
    out_shape = [jax.ShapeDtypeStruct((nb * t, SWA_Q), BF16), jax.ShapeDtypeStruct((nb * t, LANE), F32)]
    scratch = [
        pltpu.VMEM((tq, SWA_Q), BF16),
        pltpu.VMEM((SWA_KV_HEADS, kv_rows, LANE), BF16),
        pltpu.VMEM((SWA_KV_HEADS, kv_rows, LANE), BF16),
        pltpu.VMEM((SWA_KV_HEADS, kv_rows, LANE), BF16),
        pltpu.VMEM((SWA_KV_HEADS, kv_rows, LANE), BF16),
    ]
    if host is not None:
        x, gpre, wq, bq = host
        nq = MEM_Q + N_BRANCH * D_MODEL
        in_specs += [rows(D_MODEL), pl.BlockSpec((1, D_MODEL), lambda b, i: (0, 0)),
                     _resident((D_MODEL, nq)), _resident((1, nq))]
        args += [x, gpre, wq, bq]
        out_specs += [rows(MEM_Q), rows(N_BRANCH * D_MODEL)]
        out_shape += [jax.ShapeDtypeStruct((nb * t, MEM_Q), BF16),
                      jax.ShapeDtypeStruct((nb * t, N_BRANCH * D_MODEL), F32)]
        scratch += [pltpu.VMEM((tq, D_MODEL), BF16)]
    return pl.pallas_call(
        functools.partial(_swa_kernel, tq=tq, cache=cache, host_proj=host is not None),
        grid=(nb, nt),
        in_specs=in_specs,
        out_specs=out_specs,
        out_shape=out_shape,
        scratch_shapes=scratch,
        compiler_params=pltpu.CompilerParams(dimension_semantics=("parallel", "arbitrary"),
                                             vmem_limit_bytes=GDN_VMEM_LIMIT if host is not None else VMEM_LIMIT),
        name="swa",
    )(*args)


INV_BASE = 16


def _inverse_minus_identity(pms, row, col, chunk):
    base = (row // INV_BASE) == (col // INV_BASE)
    diag = row == col
    ms = [jnp.where(base, pm, 0.0) for pm in pms]
    ts = [jnp.where(diag, 1.0, m) for m in ms]
    for _ in range(INV_BASE.bit_length() - 2):
        mbs = [m.astype(BF16) for m in ms]
        ms = [_dot(mb, mb) for mb in mbs]
        ts = [t + _dot(m.astype(BF16), t.astype(BF16)) for m, t in zip(ms, ts)]
    bs = INV_BASE
    while bs < chunk:
        inner = (row // bs) == (col // bs)
        outer = (row // (2 * bs)) == (col // (2 * bs))
        sib = outer & jnp.logical_not(inner)
        xs = [_dot(jnp.where(sib, pm, 0.0).astype(BF16), t.astype(BF16)) for pm, t in zip(pms, ts)]
        ts = [t + _dot(t.astype(BF16), x.astype(BF16)) for t, x in zip(ts, xs)]
        bs *= 2
    rs = [jnp.where(diag, 0.0, t) for t in ts]
    return rs


def _gdn_kernel(x0_ref, xn_ref, gpre_ref, w_ref, b_ref, hist_ref, cw_ref, alog_ref, dt_ref, ng_ref, s0_ref,
                o_ref, sout_ref, swa_ref, clast_ref, proj_scr, h_scr, s_scr, carry_scr, ext_scr, conv_scr,
                *, tt, seg, chunk, carry):
    i = pl.program_id(0)
    nseg = tt // seg
    nblk = LANE // chunk
    ngroups = tt // LANE
    hist_rows = CONV_W - 1
    cur = proj_scr.at[i % 2]
    nxt = proj_scr.at[1 - i % 2]

    nchunks = pl.cdiv(N_PROJ, PROJ_CHUNK)

    def project(dst, c):
        cols = slice(c * PROJ_CHUNK, min((c + 1) * PROJ_CHUNK, N_PROJ))
        dst[8:8 + tt, cols] = _dot(h_scr[...], w_ref[:, cols]) + b_ref[:, cols]

    @pl.when(i == 0)
    def _():
        h_scr[...] = _rms(x0_ref[...], gpre_ref[...]).astype(BF16)
        for c in range(nchunks):
            project(proj_scr.at[0], c)

    h_scr[...] = _rms(xn_ref[...], gpre_ref[...]).astype(BF16)
    first = COL_QB * LANE // PROJ_CHUNK
    pending = list(range(first, nchunks - 1)) + list(range(first)) + [nchunks - 1]

    def emit(n=1):
        for _ in range(n):
            if pending:
                project(nxt, pending.pop(0))

    def init():
        for s in range(nseg):
            s_scr[s] = s0_ref[s]
            carry_scr[s] = jnp.zeros((8, 3 * GDN_QK), F32)
            carry_scr[s, 8 - hist_rows:8, :] = hist_ref[s]

    if carry:
        pl.when(i == 0)(init)
    else:
        init()

    swa_ref[...] = cur[8:8 + tt, COL_QA * LANE:(COL_VA + 1) * LANE]

    qkv = slice(COL_QB * LANE, COL_ZB * LANE)
    base = 8 - hist_rows
    for s in range(nseg):
        if carry:
            cur[0:8, qkv] = carry_scr[s]
            src, col0 = cur, COL_QB * LANE
            last = cur[tt:tt + 8, qkv]
            carry_scr[s] = last
        else:
            ext_scr[0:8, :] = carry_scr[s]
            ext_scr[8:8 + seg, :] = cur[8 + s * seg:8 + (s + 1) * seg, qkv]
            src, col0 = ext_scr, 0
            last = ext_scr[seg:seg + 8, :]
        clast_ref[s] = last
        for cb in range(3 * GDN_HEADS):
            lanes = slice(cb * LANE, (cb + 1) * LANE)
            slanes = slice(col0 + cb * LANE, col0 + (cb + 1) * LANE)
            y = src[base:base + seg, slanes] * cw_ref[0:1, lanes]
            for tap in range(1, CONV_W):
                y = y + src[base + tap:base + tap + seg, slanes] * cw_ref[tap:tap + 1, lanes]
            conv_scr[s * seg:(s + 1) * seg, lanes] = y * _sigmoid(y)
            if s == 0 and cb % 2 == 0:
                emit()

    row = lax.broadcasted_iota(jnp.int32, (LANE, LANE), 0)
    col = lax.broadcasted_iota(jnp.int32, (LANE, LANE), 1)
    same = (row // chunk) == (col // chunk)
    incl = same & (row >= col)
    strict = same & (row > col)
    lmask = jnp.where(incl, 1.0, 0.0).astype(BF16)
    row2 = jnp.concatenate([row, row], axis=0)

    inst = [(gi, hh) for gi in range(ngroups) for hh in range(GDN_HEADS)]
    heads = range(GDN_HEADS)

    gates = []
    for gi in range(ngroups):
        ab = cur[8 + gi * LANE:8 + (gi + 1) * LANE, COL_AB * LANE:(COL_AB + 1) * LANE]
        xa = ab + dt_ref[...]
        softplus = jnp.maximum(xa, 0.0) + jnp.log1p(jnp.exp(-jnp.abs(xa)))
        la = -jnp.exp(alog_ref[...]) * softplus
        beta = _sigmoid(ab)
        g_all = sum(_dot(lmask, piece) for piece in _split3(la))
        gates.append((g_all, g_all.T, beta))

    qn_l, kn_l, v_l, gb_l, bb_l, decay_l = [], [], [], [], [], []
    for gi, hh in inst:
        rows = slice(gi * LANE, (gi + 1) * LANE)
        q = conv_scr[rows, hh * LANE:(hh + 1) * LANE]
        k = conv_scr[rows, GDN_QK + hh * LANE:GDN_QK + (hh + 1) * LANE]
        v_l.append(conv_scr[rows, 2 * GDN_QK + hh * LANE:2 * GDN_QK + (hh + 1) * LANE])
        qn_l.append(q * (lax.rsqrt(jnp.sum(q * q, axis=-1, keepdims=True) + EPS) * (GDN_DK ** -0.5)))
        kn_l.append(k * lax.rsqrt(jnp.sum(k * k, axis=-1, keepdims=True) + EPS))
        g_all, g_all_t, beta = gates[gi]
        gb = jnp.broadcast_to(g_all[:, hh:hh + 1], (LANE, LANE))
        grow = jnp.broadcast_to(g_all_t[hh:hh + 1, :], (LANE, LANE))
        gb_l.append(gb)
        bb_l.append(jnp.broadcast_to(beta[:, GDN_HEADS + hh:GDN_HEADS + hh + 1], (LANE, LANE)))
        decay_l.append(jnp.exp(jnp.where(incl, gb - grow, -jnp.inf)))

    emit()
    knb_l = [kn.astype(BF16) for kn in kn_l]
    cat_l = [_dot_nt(jnp.concatenate([qn.astype(BF16), knb], axis=0), knb)
             for qn, knb in zip(qn_l, knb_l)]
    emit()
    qk_l = [cat[:LANE] * decay for cat, decay in zip(cat_l, decay_l)]
    pm_l = [jnp.where(strict, -(bb * cat[LANE:] * decay), 0.0)
            for bb, cat, decay in zip(bb_l, cat_l, decay_l)]
    r_l = _inverse_minus_identity(pm_l, row, col, chunk)
    emit()
    eg_l = [jnp.exp(gb) for gb in gb_l]
    rhs_l = [jnp.concatenate([v * bb, kn * (bb * eg)], axis=1)
             for v, bb, kn, eg in zip(v_l, bb_l, kn_l, eg_l)]
    sol_l = [rhs + _dot(r.astype(BF16), rhs.astype(BF16)) for r, rhs in zip(r_l, rhs_l)]
    wq_l = [jnp.concatenate([sol[:, GDN_DV:], qn * eg], axis=0).astype(BF16)
            for sol, qn, eg in zip(sol_l, qn_l, eg_l)]
    kd_l = []
    for kn, gb in zip(kn_l, gb_l):
        glast = gb[chunk - 1:chunk, :]
        for blk in range(1, nblk):
            glast = jnp.where(row >= blk * chunk, gb[(blk + 1) * chunk - 1:(blk + 1) * chunk, :], glast)
        kd_l.append(kn * jnp.exp(glast - gb))

    emit()
    nstate = 1 if carry else nblk
    state = [[s_scr[sidx, hh] for hh in heads] for sidx in range(nstate)]
    for gi in range(ngroups):
        ix = [gi * GDN_HEADS + hh for hh in heads]
        ws_l = []
        for hh in heads:
            ws = None
            for blk in range(nblk):
                part = _dot(wq_l[ix[hh]], state[0 if carry else blk][hh].astype(BF16))
                ws = part if ws is None else jnp.where(row2 >= blk * chunk, part, ws)
            ws_l.append(ws)
        emit()
        vnb_l = [(sol_l[ix[hh]][:, :GDN_DV] - ws_l[hh][:LANE]).astype(BF16) for hh in heads]
        o_l = [ws_l[hh][LANE:] + _dot(qk_l[ix[hh]].astype(BF16), vnb_l[hh]) for hh in heads]
        emit()
        for blk in range(nblk):
            sidx = 0 if carry else blk
            new = []
            for hh in heads:
                kd = kd_l[ix[hh]]
                gb = gb_l[ix[hh]]
                kdb = kd if nblk == 1 else jnp.where((row // chunk) == blk, kd, 0.0)
                gtot = jnp.exp(gb[(blk + 1) * chunk - 1:(blk + 1) * chunk, :])
                new.append(state[sidx][hh] * gtot + _dot_tn(kdb.astype(BF16), vnb_l[hh]))
            state[sidx] = new
        for hh in heads:
            rows = slice(gi * LANE, (gi + 1) * LANE)
            lanes = slice(hh * LANE, (hh + 1) * LANE)
            z = cur[8 + gi * LANE:8 + (gi + 1) * LANE, COL_ZB * LANE + hh * LANE:COL_ZB * LANE + (hh + 1) * LANE]
            o_ref[rows, lanes] = (_rms(o_l[hh], ng_ref[...]) * (z * _sigmoid(z))).astype(BF16)
    emit(len(pending))
    for sidx in range(nstate):
        for hh in heads:
            s_scr[sidx, hh] = state[sidx][hh]

    def fin():
        for s in range(nseg):
            sout_ref[s] = s_scr[s]

    if carry:
        pl.when(i == pl.num_programs(0) - 1)(fin)
    else:
        fin()


def _gdn(x, gpre, w_bf16, b, hist, s0, conv_w, alog_row, dt_row, norm_g, tt, seg, chunk, carry):
    t_rows = x.shape[0]
    nt = t_rows // tt
    nseg = tt // seg
    nseq = s0.shape[0]
    if carry:
        seq_idx = lambda i: 0
    else:
        seq_idx = lambda i: i
    swa_cols = (COL_VA + 1 - COL_QA) * LANE
    row_spec = pl.BlockSpec((1, LANE), lambda i: (0, 0))
    sspec = pl.BlockSpec((nseg, GDN_HEADS, GDN_DK, GDN_DV), lambda i: (seq_idx(i), 0, 0, 0))
    cspec = pl.BlockSpec((nseg, 8, 3 * GDN_QK), lambda i: (seq_idx(i), 0, 0))
    return pl.pallas_call(
        functools.partial(_gdn_kernel, tt=tt, seg=seg, chunk=chunk, carry=carry),
        grid=(nt,),
        in_specs=[
            pl.BlockSpec((tt, D_MODEL), lambda i: (0, 0)),
            pl.BlockSpec((tt, D_MODEL), lambda i: (jnp.minimum(i + 1, nt - 1), 0)),
            pl.BlockSpec((1, D_MODEL), lambda i: (0, 0)),
            _resident((D_MODEL, N_PROJ)),
            _resident((1, N_PROJ)),
            pl.BlockSpec((nseg, CONV_W - 1, 3 * GDN_QK), lambda i: (seq_idx(i), 0, 0)),
            pl.BlockSpec((CONV_W, 3 * GDN_QK), lambda i: (0, 0)),
            row_spec, row_spec, row_spec,
            sspec,
        ],
        out_specs=[
            pl.BlockSpec((tt, GDN_V), lambda i: (i, 0)),
            sspec,
            pl.BlockSpec((tt, swa_cols), lambda i: (i, 0)),
            cspec,
        ],
        out_shape=[
            jax.ShapeDtypeStruct((t_rows, GDN_V), BF16),
            jax.ShapeDtypeStruct((nseq, GDN_HEADS, GDN_DK, GDN_DV), F32),
            jax.ShapeDtypeStruct((t_rows, swa_cols), F32),
            jax.ShapeDtypeStruct((nseq, 8, 3 * GDN_QK), F32),
        ],
        scratch_shapes=[
            pltpu.VMEM((2, 8 + tt, N_PROJ), F32),
            pltpu.VMEM((tt, D_MODEL), BF16),
            pltpu.VMEM((nseg, GDN_HEADS, GDN_DK, GDN_DV), F32),
            pltpu.VMEM((nseg, 8, 3 * GDN_QK), F32),
            pltpu.VMEM((8 + seg, 3 * GDN_QK), F32),
            pltpu.VMEM((tt, 3 * GDN_QK), F32),
        ],
        compiler_params=pltpu.CompilerParams(dimension_semantics=("arbitrary",),
                                             vmem_limit_bytes=GDN_VMEM_LIMIT),
        name="gdn",
    )(x, x, gpre, w_bf16, b, hist, conv_w, alog_row, dt_row, norm_g, s0)


def _mix_kernel(x_ref, oa_ref, ob_ref, mk_ref, mv_ref, *rest, hosted):
    if hosted:
        qc_ref, gl_ref, wb_ref, wo_ref, gp_ref, o_ref = rest
    else:
        gpre_ref, wq_ref, bq_ref, wb_ref, wo_ref, gp_ref, o_ref = rest
    x = x_ref[...]
    tm = x.shape[0]
    nmem = mk_ref.shape[0]
    rpm = tm // nmem
    if hosted:
        qc = qc_ref[...]
    else:
        h = _rms(x, gpre_ref[...]).astype(BF16)
        qc = ((_dot(h, wq_ref[:, :MEM_Q]) + bq_ref[:, :MEM_Q]) * (MEM_HEAD_DIM ** -0.5)).astype(BF16)
    def gate(n):
        if hosted:
            gl = gl_ref[:, n * D_MODEL:(n + 1) * D_MODEL]
        else:
            cols = slice(MEM_Q + n * D_MODEL, MEM_Q + (n + 1) * D_MODEL)
            gl = _dot(h, wq_ref[:, cols]) + bq_ref[:, cols]
        return _sigmoid(gl)

    pairs = [(b, hh) for b in range(nmem) for hh in range(MEM_HEADS)]
    hsl = [slice(hh * MEM_HEAD_DIM, (hh + 1) * MEM_HEAD_DIM) for hh in range(MEM_HEADS)]
    s_l = [_dot_nt(qc[b * rpm:(b + 1) * rpm, hsl[hh]], mk_ref[b, :, hsl[hh]].astype(BF16)) for b, hh in pairs]
    merged = gate(0) * _dot(oa_ref[...], wb_ref[0]) + gate(1) * _dot(ob_ref[...], wb_ref[1])
    p_l = []
    for s in s_l:
        m = jnp.max(s, axis=-1, keepdims=True)
        e = jnp.exp(s - m)
        p_l.append((e * (1.0 / jnp.sum(e, axis=-1, keepdims=True))).astype(BF16))
    o_l = [_dot(p, mv_ref[b, :, hsl[hh]].astype(BF16)).astype(BF16) for p, (b, hh) in zip(p_l, pairs)]
    blocks = [jnp.concatenate(o_l[b * MEM_HEADS:(b + 1) * MEM_HEADS], axis=1) for b in range(nmem)]
    oc = blocks[0] if nmem == 1 else jnp.concatenate(blocks, axis=0)
    merged = merged + gate(2) * _dot(oc, wb_ref[2])
    y = _dot(merged.astype(BF16), wo_ref[...])
    o_ref[...] = x + _rms(y, gp_ref[...])


def _mix(x, oa, ob, mk, mv, proj, wb, wo, gp, tm, rows_per_mem):
    m = x.shape[0]
    tile = pl.BlockSpec((tm, D_MODEL), lambda i: (i, 0))
    vec = pl.BlockSpec((1, D_MODEL), lambda i: (0, 0))
    if mk.shape[0] == 1:
        mspec = _resident((1, N_MEM, MEM_Q))
    else:
        mspec = pl.BlockSpec((tm // rows_per_mem, N_MEM, MEM_Q), lambda i: (i, 0, 0))
    nq = MEM_Q + N_BRANCH * D_MODEL
    hosted = len(proj) == 2
    if hosted:
        pspecs = [tile, pl.BlockSpec((tm, N_BRANCH * D_MODEL), lambda i: (i, 0))]
    else:
        pspecs = [vec, _resident((D_MODEL, nq)), _resident((1, nq))]
    return pl.pallas_call(
        functools.partial(_mix_kernel, hosted=hosted),
        grid=(m // tm,),
        in_specs=[tile, tile, tile, mspec, mspec] + pspecs + [
            _resident((N_BRANCH, D_MODEL, D_MODEL)),
            _resident((D_MODEL, D_MODEL)),
            vec,
        ],
        out_specs=tile,
        out_shape=jax.ShapeDtypeStruct((m, D_MODEL), F32),
        compiler_params=pltpu.CompilerParams(dimension_semantics=("parallel",), vmem_limit_bytes=VMEM_LIMIT),
        name="mix",
    )(x, oa, ob, mk, mv, *proj, wb, wo, gp)


FF_CHUNK = 1024
FFN_TM = 512


def _ffn_kernel(xa_ref, xb_ref, g1_ref, wu_ref, wd_ref, g2_ref, oa_ref, ob_ref, *, na):
    def run(x_ref, o_ref):
        x = x_ref[...]
        hf = _rms(x, g1_ref[...]).astype(BF16)
        f = None
        for c in range(D_FF // FF_CHUNK):
            sl = slice(c * FF_CHUNK, (c + 1) * FF_CHUNK)
            up = jnp.maximum(_dot(hf, wu_ref[:, sl]), 0.0)
            part = _dot((up * up).astype(BF16), wd_ref[sl, :])
            f = part if f is None else f + part
        o_ref[...] = x + _rms(f, g2_ref[...])

    i = pl.program_id(0)
    pl.when(i < na)(functools.partial(run, xa_ref, oa_ref))
    pl.when(i >= na)(functools.partial(run, xb_ref, ob_ref))


def _ffn(xa, xb, g1, wu, wd, g2, tm):
    na, nb = xa.shape[0] // tm, xb.shape[0] // tm
    a_tile = pl.BlockSpec((tm, D_MODEL), lambda i: (jnp.minimum(i, na - 1), 0))
    b_tile = pl.BlockSpec((tm, D_MODEL), lambda i: (jnp.maximum(i - na, 0), 0))
    vec = pl.BlockSpec((1, D_MODEL), lambda i: (0, 0))
    return pl.pallas_call(
        functools.partial(_ffn_kernel, na=na),
        grid=(na + nb,),
        in_specs=[a_tile, b_tile, vec, _resident((D_MODEL, D_FF)), _resident((D_FF, D_MODEL)), vec],
        out_specs=[a_tile, b_tile],
        out_shape=[jax.ShapeDtypeStruct(xa.shape, F32), jax.ShapeDtypeStruct(xb.shape, F32)],
        compiler_params=pltpu.CompilerParams(dimension_semantics=("arbitrary",), vmem_limit_bytes=VMEM_LIMIT),
        name="ffn",
    )(xa, xb, g1, wu, wd, g2)


def _rope_tables(pos0, nt, tq):
    half = SWA_HEAD_DIM // 2
    inv = ROPE_THETA ** (-jnp.arange(half, dtype=F32) / half)
    inv = jnp.tile(inv, LANE // half)
    sgn = jnp.tile(jnp.concatenate([-jnp.ones((half,), F32), jnp.ones((half,), F32)]), LANE // SWA_HEAD_DIM)
    base = (pos0 + tq * jnp.arange(nt, dtype=jnp.int32)).astype(F32)[:, None] * inv[None, :]
    off = jnp.arange(-WINDOW, tq, dtype=jnp.int32).astype(F32)[:, None] * inv[None, :]
    rope_base = jnp.stack([jnp.cos(base), sgn * jnp.sin(base)], axis=1)
    rope_off = jnp.stack([jnp.cos(off), sgn * jnp.sin(off)], axis=0)
    return rope_base, rope_off


def _pad_row(v):
    return jnp.zeros((1, LANE), F32).at[0, :v.shape[0]].set(v)


def _layer(x, nb, t, pos0, k_hist, v_hist, s0, conv_hist, mk, mv, wts, cfg):
    ob, s_new, swa_in, conv_last = _gdn(x, wts['g_pre_mix'], wts['w_in'], wts['b_in'], conv_hist, s0,
                                        wts['conv_w'], wts['alog'], wts['dt'], wts['norm_g'],
                                        cfg['tt_gdn'], cfg['seg'], cfg['chunk'], cfg['carry'])
    rope_base, rope_off = _rope_tables(pos0, t // cfg['tq'], cfg['tq'])
    qproj = (wts['g_pre_mix'], wts['w_q'], wts['b_q'])
    if cfg['host_proj']:
        oa, k_rot, qc, gl = _swa(swa_in, k_hist, v_hist, rope_base, rope_off, wts['sinks'], nb, t, cfg['tq'],
                                 cfg['cache'], host=(x,) + qproj)
        qproj = (qc, gl)
    else:
        oa, k_rot = _swa(swa_in, k_hist, v_hist, rope_base, rope_off, wts['sinks'], nb, t, cfg['tq'],
                         cfg['cache'])
    x1 = _mix(x, oa, ob, mk, mv, qproj, wts['w_branch'], wts['w_out'], wts['g_post_mix'], cfg['tm_mix'], t)
    return x1, k_rot, s_new, swa_in, conv_last


def kernel(x_prompt, x_sample, mem_prompt, cache_swa_k, cache_swa_v, state_gdn, state_conv, cache_mem_k, cache_mem_v, g_pre_mix, w_in, b_in, swa_sinks, conv_w, gdn_a_log, gdn_dt_bias, gdn_norm_g, g_mem, w_mem_kv, w_branch, w_out, g_post_mix, g_pre_ffn, w_up, w_down, g_post_ffn):
    depth = w_in.shape[0]
    assert depth == 1
    l = 0
    bp, tp, _ = x_prompt.shape
    bs, ts, _ = x_sample.shape
    assert bp == 1

    w_in_bf = w_in[l].astype(BF16)
    wts = {
        'g_pre_mix': g_pre_mix[l][None, :],
        'w_in': w_in_bf,
        'b_in': b_in[l][None, :],
        'w_q': w_in_bf[:, O_QC:O_END],
        'b_q': b_in[l][None, O_QC:O_END],
        'sinks': swa_sinks[l],
        'conv_w': conv_w[l],
        'alog': _pad_row(gdn_a_log[l]),
        'dt': _pad_row(gdn_dt_bias[l]),
        'norm_g': gdn_norm_g[l][None, :],
        'w_branch': w_branch[l].astype(BF16),
        'w_out': w_out[l].astype(BF16),
        'g_post_mix': g_post_mix[l][None, :],
        'g_pre_ffn': g_pre_ffn[l][None, :],
        'w_up': w_up[l].astype(BF16),
        'w_down': w_down[l].astype(BF16),
        'g_post_ffn': g_post_ffn[l][None, :],
    }

    mkv, mkv_bf = _mem_kv(mem_prompt[0], g_mem[l][None, :], w_mem_kv[l].astype(BF16))
    cfg_p = dict(tq=512, cache=False, host_proj=True, tt_gdn=256, seg=256, chunk=128, carry=True,
                 tm_mix=512)
    zero_hist = jnp.zeros((1, CONV_W - 1, 3 * GDN_QK), F32)
    zero_s = jnp.zeros((1, GDN_HEADS, GDN_DK, GDN_DV), F32)
    x1p, kp_rot, sp_new, swa_p, clast_p = _layer(
        x_prompt.reshape(bp * tp, D_MODEL), bp, tp, 0, None, None,
        zero_s, zero_hist, mkv_bf[None, :, :MEM_Q], mkv_bf[None, :, MEM_Q:], wts, cfg_p)

    cfg_s = dict(tq=ts, cache=True, host_proj=False, tt_gdn=128, seg=ts, chunk=ts, carry=False,
                 tm_mix=4 * ts)
    x1s, ks_rot, ss_new, swa_s, clast_s = _layer(
        x_sample.reshape(bs * ts, D_MODEL), bs, ts, PAST_LEN,
        cache_swa_k[l].reshape(bs, WINDOW, SWA_KV), cache_swa_v[l].reshape(bs, WINDOW, SWA_KV),
        state_gdn[l], state_conv[l],
        cache_mem_k[l].reshape(bs, N_MEM, MEM_Q), cache_mem_v[l].reshape(bs, N_MEM, MEM_Q), wts, cfg_s)

    yp, ys = _ffn(x1p, x1s, wts['g_pre_ffn'], wts['w_up'], wts['w_down'], wts['g_post_ffn'], FFN_TM)

    kv_shape = (SWA_KV_HEADS, SWA_HEAD_DIM)
    va_cols = slice(SWA_VA * LANE, (SWA_VA + 1) * LANE)

    p_k = kp_rot[tp - WINDOW:].reshape((1, bp, WINDOW) + kv_shape)
    p_v = swa_p[tp - WINDOW:, va_cols].reshape((1, bp, WINDOW) + kv_shape)
    p_c = clast_p[:, 8 - (CONV_W - 1):][None]
    p_mk = mkv[:, :MEM_Q].reshape(1, bp, N_MEM, MEM_HEADS, MEM_HEAD_DIM)
    p_mv = mkv[:, MEM_Q:].reshape(1, bp, N_MEM, MEM_HEADS, MEM_HEAD_DIM)

    ks_new = ks_rot.reshape(bs, ts, SWA_KV)
    vs_new = swa_s[:, va_cols].reshape(bs, ts, SWA_KV)
    s_k = jnp.concatenate([cache_swa_k[l].reshape(bs, WINDOW, SWA_KV), ks_new], axis=1)[:, -WINDOW:]
    s_v = jnp.concatenate([cache_swa_v[l].reshape(bs, WINDOW, SWA_KV), vs_new], axis=1)[:, -WINDOW:]
    s_c = clast_s[:, 8 - (CONV_W - 1):]

    return (yp.reshape(bp, tp, D_MODEL), ys.reshape(bs, ts, D_MODEL),
            p_k, p_v, sp_new[None], p_c, p_mk, p_mv,
            s_k.reshape((1, bs, WINDOW) + kv_shape), s_v.reshape((1, bs, WINDOW) + kv_shape),
            ss_new[None], s_c[None])
```
